```python
import math
import jax, jax.numpy as jnp
from jax import lax
import numpy as np

D_MODEL = 1024
BATCH = 8
SEQ = 2048
DEPTH = 4
DEC_BATCH = 128
DEC_SEQ = 1
PAST_LEN = 16384
PAGE_SIZE = 128

N_META = 16
N_MIXERS = 2
N_SSD_LAYERS = (DEPTH + 1) // 2
N_LRU_LAYERS = DEPTH // 2
CONV_WIDTH = 4
EPS = 1e-6
SSD_EXPAND = 2
D_INNER = SSD_EXPAND * D_MODEL
SSD_HEAD_DIM = 64
SSD_HEADS = D_INNER // SSD_HEAD_DIM
SSD_GROUPS = 8
SSD_HPG = SSD_HEADS // SSD_GROUPS
SSD_STATE = 128
SSD_CHUNK = 128
SSD_CONV_DIM = D_INNER + 2 * SSD_GROUPS * SSD_STATE
SSD_IN_DIM = D_INNER + SSD_CONV_DIM + SSD_HEADS
D_RNN = D_MODEL
LRU_BLOCKS = 8
LRU_BLOCK_W = D_RNN // LRU_BLOCKS
LRU_C = 8.0
D_FF = 4 * D_MODEL

kernel_name = "hybrid_ssd_rglru_decode_step"


def rms_norm(x, g):
    xf = x.astype(jnp.float32)
    y = xf * lax.rsqrt(jnp.mean(xf * xf, axis=-1, keepdims=True) + EPS)
    return (y * g.astype(jnp.float32)).astype(x.dtype)


def causal_conv(x, prev, w, b):
    L = x.shape[1]
    xp = jnp.concatenate([prev.astype(x.dtype), x], axis=1)
    out = b + sum(xp[:, k:k + L] * w[k] for k in range(CONV_WIDTH))
    return out, xp[:, -(CONV_WIDTH - 1):]


def segsum_exp(cs):
    Q = cs.shape[-1]
    diff = cs[..., :, None] - cs[..., None, :]
    mask = jnp.tril(jnp.ones((Q, Q), dtype=bool))
    return jnp.where(mask, jnp.exp(jnp.where(mask, diff, 0.0)), 0.0)


def ssd_chunked(xdt, a, Bm, Cm, h0, chunk):
    b, l = xdt.shape[:2]
    nc = l // chunk
    f32 = jnp.float32
    x = xdt.astype(f32).reshape(b, nc, chunk, SSD_GROUPS, SSD_HPG, SSD_HEAD_DIM)
    Bc = Bm.astype(f32).reshape(b, nc, chunk, SSD_GROUPS, SSD_STATE)
    Cc = Cm.astype(f32).reshape(b, nc, chunk, SSD_GROUPS, SSD_STATE)
    ac = a.astype(f32).reshape(b, nc, chunk, SSD_GROUPS, SSD_HPG)
    cs = jnp.cumsum(ac, axis=2)
    Lmat = segsum_exp(jnp.moveaxis(cs, 2, -1))
    cb = jnp.einsum("bclgn,bcsgn->bcgls", Cc, Bc)
    scores = cb[:, :, :, None] * Lmat
    y_diag = jnp.einsum("bcgrls,bcsgrp->bclgrp", scores, x)
    decay_to_end = jnp.exp(cs[:, :, -1:] - cs)
    chunk_states = jnp.einsum("bcsgn,bcsgrp->bcgrpn", Bc, x * decay_to_end[..., None])
    chunk_decay = jnp.exp(cs[:, :, -1])

    def step(h, inp):
        dec, st = inp
        return h * dec[..., None, None] + st, h

    h_init = h0.astype(f32).reshape(b, SSD_GROUPS, SSD_HPG, SSD_HEAD_DIM, SSD_STATE)
    h_final, h_prev = lax.scan(step, h_init, (jnp.moveaxis(chunk_decay, 1, 0), jnp.moveaxis(chunk_states, 1, 0)))
    h_prev = jnp.moveaxis(h_prev, 0, 1)
    y_off = jnp.einsum("bclgn,bcgrpn->bclgrp", Cc, h_prev) * jnp.exp(cs)[..., None]
    y = (y_diag + y_off).reshape(b, l, SSD_HEADS, SSD_HEAD_DIM)
    return y, h_final.reshape(b, SSD_HEADS, SSD_HEAD_DIM, SSD_STATE).astype(h0.dtype)


def ssd_mixer(u, conv_prev, h0, w_in, conv_w, conv_b, dt_bias, a_log, d_skip, norm_g, w_out, n_lead, chunk):
    b, l, _ = u.shape
    proj = u @ w_in
    z = proj[..., :D_INNER]
    xbc = proj[..., D_INNER:D_INNER + SSD_CONV_DIM]
    dt = proj[..., D_INNER + SSD_CONV_DIM:]
    xbc, conv_new = causal_conv(xbc, conv_prev, conv_w, conv_b)
    xbc = jax.nn.silu(xbc)
    xs = xbc[..., :D_INNER].reshape(b, l, SSD_HEADS, SSD_HEAD_DIM)
    Bm = xbc[..., D_INNER:D_INNER + SSD_GROUPS * SSD_STATE].reshape(b, l, SSD_GROUPS, SSD_STATE)
    Cm = xbc[..., D_INNER + SSD_GROUPS * SSD_STATE:].reshape(b, l, SSD_GROUPS, SSD_STATE)
    dt = jax.nn.softplus(dt.astype(jnp.float32) + dt_bias.astype(jnp.float32))
    a = dt * (-jnp.exp(a_log.astype(jnp.float32)))
    xdt = xs.astype(jnp.float32) * dt[..., None]
    if n_lead > 0:
        y1, h = ssd_chunked(xdt[:, :n_lead], a[:, :n_lead], Bm[:, :n_lead], Cm[:, :n_lead], h0, n_lead)
        y2, h = ssd_chunked(xdt[:, n_lead:], a[:, n_lead:], Bm[:, n_lead:], Cm[:, n_lead:], h, chunk)
        y = jnp.concatenate([y1, y2], axis=1)
    else:
        y, h = ssd_chunked(xdt, a, Bm, Cm, h0, chunk)
    y = y + xs.astype(jnp.float32) * d_skip.astype(jnp.float32)[:, None]
    y = y.reshape(b, l, D_INNER) * jax.nn.silu(z.astype(jnp.float32))
    yg = y.reshape(b, l, SSD_GROUPS, D_INNER // SSD_GROUPS)
    yg = yg * lax.rsqrt(jnp.mean(yg * yg, axis=-1, keepdims=True) + EPS)
    y = yg.reshape(b, l, D_INNER) * norm_g.astype(jnp.float32)
    return y.astype(u.dtype) @ w_out, conv_new, h


def rglru_mixer(u, conv_prev, h0, w_in, b_in, conv_w, conv_b, w_a, b_a, w_x, b_x, lam, w_out, b_out):
    b, l, _ = u.shape
    proj = u @ w_in + b_in
    gate = jax.nn.gelu(proj[..., :D_RNN], approximate=True)
    xr, conv_new = causal_conv(proj[..., D_RNN:], conv_prev, conv_w, conv_b)
    xb = xr.reshape(b, l, LRU_BLOCKS, LRU_BLOCK_W)
    r = jax.nn.sigmoid((jnp.einsum("blkc,kcd->blkd", xb, w_a).reshape(b, l, D_RNN) + b_a).astype(jnp.float32))
    i = jax.nn.sigmoid((jnp.einsum("blkc,kcd->blkd", xb, w_x).reshape(b, l, D_RNN) + b_x).astype(jnp.float32))
    log_a = -LRU_C * r * jax.nn.softplus(-lam.astype(jnp.float32))
    a = jnp.exp(log_a)
    mult = jnp.sqrt(-jnp.expm1(2.0 * log_a))
    bterm = mult * i * xr.astype(jnp.float32)
    bterm = bterm.at[:, 0].add(a[:, 0] * h0.astype(jnp.float32))

    def combine(c1, c2):
        a1, b1 = c1
        a2, b2 = c2
        return a1 * a2, a2 * b1 + b2

    _, h = lax.associative_scan(combine, (a, bterm), axis=1)
    y = (h * gate.astype(jnp.float32)).astype(u.dtype)
    return y @ w_out + b_out, conv_new, h[:, -1].astype(h0.dtype)


def sq_relu_mlp(x, w1, w2):
    h = jax.nn.relu(x @ w1)
    return (h * h) @ w2


def trunk(x, ssd_conv, ssd_h, lru_conv, lru_h, n_lead, chunk, p):
    n_ssd_conv, n_ssd_h, n_lru_conv, n_lru_h = [], [], [], []
    for i in range(DEPTH):
        h = rms_norm(x, p["norm_mix_pre"][i])
        j = i // N_MIXERS
        if i % N_MIXERS == 0:
            m, c_new, s_new = ssd_mixer(h, ssd_conv[j], ssd_h[j], p["ssd_w_in"][j], p["ssd_conv_w"][j],
                                        p["ssd_conv_b"][j], p["ssd_dt_bias"][j], p["ssd_a_log"][j],
                                        p["ssd_d"][j], p["ssd_norm"][j], p["ssd_w_out"][j], n_lead, chunk)
            n_ssd_conv.append(c_new)
            n_ssd_h.append(s_new)
        else:
            m, c_new, s_new = rglru_mixer(h, lru_conv[j], lru_h[j], p["lru_w_in"][j], p["lru_b_in"][j],
                                          p["lru_conv_w"][j], p["lru_conv_b"][j], p["lru_w_a"][j],
                                          p["lru_b_a"][j], p["lru_w_x"][j], p["lru_b_x"][j],
                                          p["lru_lambda"][j], p["lru_w_out"][j], p["lru_b_out"][j])
            n_lru_conv.append(c_new)
            n_lru_h.append(s_new)
        x = x + rms_norm(m, p["norm_mix_post"][i])
        h = rms_norm(x, p["norm_ffn_pre"][i])
        x = x + rms_norm(sq_relu_mlp(h, p["ffn_w1"][i], p["ffn_w2"][i]), p["norm_ffn_post"][i])
    return x, jnp.stack(n_ssd_conv), jnp.stack(n_ssd_h), jnp.stack(n_lru_conv), jnp.stack(n_lru_h)


def setup_inputs(seed: int = 0) -> dict:
    key = jax.random.key(seed)
    ks = jax.random.split(key, 40)
    nrm = jax.random.normal
    NA, NB = N_SSD_LAYERS, N_LRU_LAYERS
    dt0 = jnp.exp(jax.random.uniform(ks[10], (NA, SSD_HEADS), minval=math.log(1e-3), maxval=math.log(1e-1)))
    a_pow = jax.random.uniform(ks[20], (NB, D_RNN), minval=0.9, maxval=0.999)
    s = a_pow ** (1.0 / LRU_C)
    return {
        "x_prompt": nrm(ks[0], (BATCH, SEQ, D_MODEL), jnp.float32),
        "x_sample": nrm(ks[1], (DEC_BATCH, DEC_SEQ, D_MODEL), jnp.float32),
        "state_ssd_conv": nrm(ks[2], (NA, DEC_BATCH, CONV_WIDTH - 1, SSD_CONV_DIM), jnp.float32),
        "state_ssd_h": 0.1 * nrm(ks[3], (NA, DEC_BATCH, SSD_HEADS, SSD_HEAD_DIM, SSD_STATE), jnp.float32),
        "state_lru_conv": nrm(ks[4], (NB, DEC_BATCH, CONV_WIDTH - 1, D_RNN), jnp.float32),
        "state_lru_h": 0.5 * nrm(ks[5], (NB, DEC_BATCH, D_RNN), jnp.float32),
        "meta_tokens": nrm(ks[6], (N_META, D_MODEL), jnp.float32),
        "norm_mix_pre": 1.0 + 0.05 * nrm(ks[30], (DEPTH, D_MODEL), jnp.float32),
        "norm_mix_post": 1.0 + 0.05 * nrm(ks[31], (DEPTH, D_MODEL), jnp.float32),
        "norm_ffn_pre": 1.0 + 0.05 * nrm(ks[32], (DEPTH, D_MODEL), jnp.float32),
        "norm_ffn_post": 1.0 + 0.05 * nrm(ks[33], (DEPTH, D_MODEL), jnp.float32),
        "ssd_w_in": nrm(ks[7], (NA, D_MODEL, SSD_IN_DIM), jnp.float32) * D_MODEL ** -0.5,
        "ssd_conv_w": nrm(ks[8], (NA, CONV_WIDTH, SSD_CONV_DIM), jnp.float32) * CONV_WIDTH ** -0.5,
        "ssd_conv_b": 0.02 * nrm(ks[9], (NA, SSD_CONV_DIM), jnp.float32),
        "ssd_dt_bias": dt0 + jnp.log(-jnp.expm1(-dt0)),
        "ssd_a_log": jnp.log(jax.random.uniform(ks[11], (NA, SSD_HEADS), minval=1.0, maxval=16.0)),
        "ssd_d": 1.0 + 0.1 * nrm(ks[12], (NA, SSD_HEADS), jnp.float32),
        "ssd_norm": 1.0 + 0.05 * nrm(ks[13], (NA, D_INNER), jnp.float32),
        "ssd_w_out": nrm(ks[14], (NA, D_INNER, D_MODEL), jnp.float32) * D_INNER ** -0.5,
        "lru_w_in": nrm(ks[15], (NB, D_MODEL, 2 * D_RNN), jnp.float32) * D_MODEL ** -0.5,
        "lru_b_in": 0.02 * nrm(ks[16], (NB, 2 * D_RNN), jnp.float32),
        "lru_conv_w": nrm(ks[17], (NB, CONV_WIDTH, D_RNN), jnp.float32) * CONV_WIDTH ** -0.5,
        "lru_conv_b": 0.02 * nrm(ks[18], (NB, D_RNN), jnp.float32),
        "lru_w_a": nrm(ks[19], (NB, LRU_BLOCKS, LRU_BLOCK_W, LRU_BLOCK_W), jnp.float32) * LRU_BLOCK_W ** -0.5,
        "lru_b_a": 0.02 * nrm(ks[21], (NB, D_RNN), jnp.float32),
        "lru_w_x": nrm(ks[22], (NB, LRU_BLOCKS, LRU_BLOCK_W, LRU_BLOCK_W), jnp.float32) * LRU_BLOCK_W ** -0.5,
        "lru_b_x": 0.02 * nrm(ks[23], (NB, D_RNN), jnp.float32),
        "lru_lambda": jnp.log(s / (1.0 - s)),
        "lru_w_out": nrm(ks[24], (NB, D_RNN, D_MODEL), jnp.float32) * D_RNN ** -0.5,
        "lru_b_out": 0.02 * nrm(ks[25], (NB, D_MODEL), jnp.float32),
        "ffn_w1": nrm(ks[26], (DEPTH, D_MODEL, D_FF), jnp.float32) * D_MODEL ** -0.5,
        "ffn_w2": nrm(ks[27], (DEPTH, D_FF, D_MODEL), jnp.float32) * D_FF ** -0.5,
    }


def reference(x_prompt, x_sample, state_ssd_conv, state_ssd_h, state_lru_conv, state_lru_h, meta_tokens,
              norm_mix_pre, norm_mix_post, norm_ffn_pre, norm_ffn_post,
              ssd_w_in, ssd_conv_w, ssd_conv_b, ssd_dt_bias, ssd_a_log, ssd_d, ssd_norm, ssd_w_out,
              lru_w_in, lru_b_in, lru_conv_w, lru_conv_b, lru_w_a, lru_b_a, lru_w_x, lru_b_x, lru_lambda,
              lru_w_out, lru_b_out, ffn_w1, ffn_w2):
    p = dict(norm_mix_pre=norm_mix_pre, norm_mix_post=norm_mix_post, norm_ffn_pre=norm_ffn_pre,
             norm_ffn_post=norm_ffn_post, ssd_w_in=ssd_w_in, ssd_conv_w=ssd_conv_w, ssd_conv_b=ssd_conv_b,
             ssd_dt_bias=ssd_dt_bias, ssd_a_log=ssd_a_log, ssd_d=ssd_d, ssd_norm=ssd_norm, ssd_w_out=ssd_w_out,
             lru_w_in=lru_w_in, lru_b_in=lru_b_in, lru_conv_w=lru_conv_w, lru_conv_b=lru_conv_b,
             lru_w_a=lru_w_a, lru_b_a=lru_b_a, lru_w_x=lru_w_x, lru_b_x=lru_b_x, lru_lambda=lru_lambda,
             lru_w_out=lru_w_out, lru_b_out=lru_b_out, ffn_w1=ffn_w1, ffn_w2=ffn_w2)
    dt = x_prompt.dtype
    meta = jnp.broadcast_to(meta_tokens.astype(dt)[None], (BATCH, N_META, D_MODEL))
    xp = jnp.concatenate([meta, x_prompt], axis=1)
    z_ssd_conv = jnp.zeros((N_SSD_LAYERS, BATCH, CONV_WIDTH - 1, SSD_CONV_DIM), dt)
    z_ssd_h = jnp.zeros((N_SSD_LAYERS, BATCH, SSD_HEADS, SSD_HEAD_DIM, SSD_STATE), dt)
    z_lru_conv = jnp.zeros((N_LRU_LAYERS, BATCH, CONV_WIDTH - 1, D_RNN), dt)
    z_lru_h = jnp.zeros((N_LRU_LAYERS, BATCH, D_RNN), dt)
    yp, p_ssd_conv, p_ssd_h, p_lru_conv, p_lru_h = trunk(xp, z_ssd_conv, z_ssd_h, z_lru_conv, z_lru_h,
                                                         N_META, SSD_CHUNK, p)
    y_prompt = yp[:, N_META:]
    y_sample, s_ssd_conv, s_ssd_h, s_lru_conv, s_lru_h = trunk(x_sample, state_ssd_conv, state_ssd_h,
                                                               state_lru_conv, state_lru_h, 0, DEC_SEQ, p)
    return (y_prompt, y_sample, p_ssd_conv, p_ssd_h, p_lru_conv, p_lru_h,
            s_ssd_conv, s_ssd_h, s_lru_conv, s_lru_h)
```

```python
import functools
import math

import jax
import jax.numpy as jnp
from jax import lax
from jax.experimental import pallas as pl
from jax.experimental.pallas import tpu as pltpu

F32 = jnp.float32
BF16 = jnp.bfloat16

EPS = 1e-6
CONV_WIDTH = 4
SSD_HEAD_DIM = 64
SSD_GROUPS = 8
SSD_STATE = 128
LRU_BLOCKS = 8
LRU_C = 8.0
CHUNK = 128
LANES = 128
SUBLANES = 8
VMEM_LIMIT = 56 * 1024 * 1024


def _cparams(sem):
    return pltpu.CompilerParams(dimension_semantics=sem, vmem_limit_bytes=VMEM_LIMIT)


def _rms(x, g):
    return x * lax.rsqrt(jnp.mean(x * x, axis=-1, keepdims=True) + EPS) * g


def _softplus(x):
    return jnp.maximum(x, 0.0) + jnp.log1p(jnp.exp(-jnp.abs(x)))


def _silu(x):
    return x * jax.nn.sigmoid(x)


def _gelu_tanh(x):
    return 0.5 * x * (1.0 + jnp.tanh(math.sqrt(2.0 / math.pi) * (x + 0.044715 * (x * x * x))))


def _dot(a, b):
    return jnp.dot(a, b, preferred_element_type=F32)


def _dot_nt(a, b):
    return lax.dot_general(a, b, (((1,), (1,)), ((), ())), preferred_element_type=F32)


def _pair_expand(q, j, lane):
    return jnp.where(lane < SSD_HEAD_DIM, q[:, 2 * j:2 * j + 1], q[:, 2 * j + 1:2 * j + 2])


def _conv_seq(x, xp_s, carry_s, cw_ref, cb_ref, T):
    xp_s[0:SUBLANES, :] = carry_s[...]
    xp_s[SUBLANES:SUBLANES + T, :] = x
    carry_s[...] = x[T - SUBLANES:T, :]
    out = cb_ref[...] + cw_ref[CONV_WIDTH - 1:CONV_WIDTH, :] * x
    for k in range(CONV_WIDTH - 1):
        out = out + cw_ref[k:k + 1, :] * xp_s[pl.ds(SUBLANES - (CONV_WIDTH - 1) + k, T), :]
    return out


def _conv_step(x, prevs, cw_ref, cb_ref):
    out = cb_ref[...] + cw_ref[CONV_WIDTH - 1:CONV_WIDTH, :] * x
    for k in range(CONV_WIDTH - 1):
        out = out + cw_ref[k:k + 1, :] * prevs[k]
    return out


def _ssd_gate_norm(y, z, ng_ref):
    y = y * _silu(z)
    gw = y.shape[1] // SSD_GROUPS
    parts = []
    for g in range(SSD_GROUPS):
        seg = y[:, g * gw:(g + 1) * gw]
        parts.append(seg * lax.rsqrt(jnp.mean(seg * seg, axis=-1, keepdims=True) + EPS))
    return jnp.concatenate(parts, axis=1) * ng_ref[...]


def _lru_gates(xr, wax_ref, ba_ref, bx_ref, lam_ref):
    bw = xr.shape[1] // LRU_BLOCKS
    ra, ix = [], []
    for k in range(LRU_BLOCKS):
        g = _dot(xr[:, k * bw:(k + 1) * bw].astype(BF16), wax_ref[k])
        ra.append(g[:, :bw])
        ix.append(g[:, bw:])
    r = jax.nn.sigmoid(jnp.concatenate(ra, axis=1) + ba_ref[...])
    i = jax.nn.sigmoid(jnp.concatenate(ix, axis=1) + bx_ref[...])
    log_a = (-LRU_C) * r * _softplus(-lam_ref[...])
    a = jnp.exp(log_a)
    mult = jnp.sqrt(-jnp.tanh(log_a) * (a * a + 1.0))
    return a, mult * i * xr


def _norm_matmul_kernel(*refs, has_bias, has_small):
    it = iter(refs)
    x_ref, g_ref, w_ref = next(it), next(it), next(it)
    b_ref = next(it) if has_bias else None
    ws_ref = next(it) if has_small else None
    o_ref = next(it)
    os_ref = next(it) if has_small else None
    xn_s = next(it)

    @pl.when(pl.program_id(1) == 0)
    def _():
        xn = _rms(x_ref[...], g_ref[...]).astype(BF16)
        xn_s[...] = xn
        if has_small:
            os_ref[...] = _dot(xn, ws_ref[...])

    acc = _dot(xn_s[...], w_ref[...])
    if has_bias:
        acc = acc + b_ref[...]
    o_ref[...] = acc


def _norm_matmul(x, g, w, bias=None, w_small=None, *, tm, tn):
    M, D = x.shape
    N = w.shape[1]
    tm = min(tm, M)
    assert M % tm == 0 and N % tn == 0
    in_specs = [pl.BlockSpec((tm, D), lambda i, j: (i, 0)),
                pl.BlockSpec((1, D), lambda i, j: (0, 0)),
                pl.BlockSpec((D, tn), lambda i, j: (0, j))]
    args = [x, g, w]
    out_shape = [jax.ShapeDtypeStruct((M, N), F32)]
    out_specs = [pl.BlockSpec((tm, tn), lambda i, j: (i, j))]
    if bias is not None:
        in_specs.append(pl.BlockSpec((1, tn), lambda i, j: (0, j)))
        args.append(bias)
    if w_small is not None:
        ns = w_small.shape[1]
        in_specs.append(pl.BlockSpec((D, ns), lambda i, j: (0, 0)))
        args.append(w_small)
        out_shape.append(jax.ShapeDtypeStruct((M, ns), F32))
        out_specs.append(pl.BlockSpec((tm, ns), lambda i, j: (i, 0)))
    res = pl.pallas_call(
        functools.partial(_norm_matmul_kernel, has_bias=bias is not None, has_small=w_small is not None),
        grid=(M // tm, N // tn),
        in_specs=in_specs, out_specs=out_specs, out_shape=out_shape,
        scratch_shapes=[pltpu.VMEM((tm, D), BF16)],
        compiler_params=_cparams(("parallel", "arbitrary")),
        name="norm_matmul",
    )(*args)
    return res if w_small is not None else res[0]


def _out_ffn_kernel(y_ref, x_ref, wo_ref, bo_ref, gpost_ref, gpre_ref, w1_ref, w2_ref, gfpost_ref, o_ref, *, fc):
    m = _dot(y_ref[...], wo_ref[...]) + bo_ref[...]
    x1 = x_ref[...] + _rms(m, gpost_ref[...])
    hn = _rms(x1, gpre_ref[...]).astype(BF16)
    dff = w1_ref.shape[1]
    acc = None
    for c in range(dff // fc):
        h = jnp.maximum(_dot(hn, w1_ref[:, c * fc:(c + 1) * fc]), 0.0)
        part = _dot((h * h).astype(BF16), w2_ref[c * fc:(c + 1) * fc, :])
        acc = part if acc is None else acc + part
    o_ref[...] = x1 + _rms(acc, gfpost_ref[...])


def _out_ffn(y, x, wo, bo, gpost, gpre, w1, w2, gfpost, *, tm, fc=1024):
    M, D = x.shape
    K = y.shape[1]
    dff = w1.shape[1]
    tm = min(tm, M)
    assert M % tm == 0 and dff % fc == 0
    const = lambda i: (0, 0)
    once = dict(pipeline_mode=pl.Buffered(1))
    return pl.pallas_call(
        functools.partial(_out_ffn_kernel, fc=fc),
        grid=(M // tm,),
        in_specs=[pl.BlockSpec((tm, K), lambda i: (i, 0)),
                  pl.BlockSpec((tm, D), lambda i: (i, 0)),
                  pl.BlockSpec((K, D), const, **once),
                  pl.BlockSpec((1, D), const),
                  pl.BlockSpec((1, D), const),
                  pl.BlockSpec((1, D), const),
                  pl.BlockSpec((D, dff), const, **once),
                  pl.BlockSpec((dff, D), const, **once),
                  pl.BlockSpec((1, D), const)],
        out_specs=pl.BlockSpec((tm, D), lambda i: (i, 0)),
        out_shape=jax.ShapeDtypeStruct((M, D), F32),
        compiler_params=_cparams(("parallel",)),
        name="out_ffn",
    )(y, x, wo, bo, gpost, gpre, w1, w2, gfpost)


def _ssd_chunk_kernel(xbc_ref, z_ref, dt_ref, conv0_ref, h0_ref, cw_ref, cb_ref, dtb_ref, alog_ref, dexp_ref,
                      ng_ref, y_ref, hout_ref, xp_s, carry_s, hT_s, *, T, n_valid):
    c = pl.program_id(1)
    d_inner = z_ref.shape[1]
    gs = SSD_GROUPS * SSD_STATE
    gw = d_inner // SSD_GROUPS

    @pl.when(c == 0)
    def _():
        carry_s[...] = conv0_ref[0]
        for g in range(SSD_GROUPS):
            hT_s[g] = h0_ref[0, g].T

    xbc = _silu(_conv_seq(xbc_ref[...], xp_s, carry_s, cw_ref, cb_ref, T))
    xs = xbc[:, :d_inner]
    Bm = xbc[:, d_inner:d_inner + gs]
    Cm = xbc[:, d_inner + gs:]

    row = lax.broadcasted_iota(jnp.int32, (T, T), 0)
    col = lax.broadcasted_iota(jnp.int32, (T, T), 1)
    causal = row >= col
    lane = lax.broadcasted_iota(jnp.int32, (T, LANES), 1)

    dtv = _softplus(dt_ref[...] + dtb_ref[...])
    if n_valid < T:
        dtv = jnp.where(lax.broadcasted_iota(jnp.int32, (T, LANES), 0) < n_valid, dtv, 0.0)
    a = dtv * (-jnp.exp(alog_ref[...]))
    cs = jnp.dot(causal.astype(F32), a, precision=lax.Precision.HIGHEST, preferred_element_type=F32)
    csT = cs.T
    dtT = dtv.T
    total = cs[T - 1:T, :]
    ecs = jnp.exp(cs)
    wd = dtv * jnp.exp(total - cs)
    dec_tot = jnp.exp(total)

    ys = []
    for g in range(SSD_GROUPS):
        Bg = Bm[:, g * SSD_STATE:(g + 1) * SSD_STATE]
        Cg = Cm[:, g * SSD_STATE:(g + 1) * SSD_STATE].astype(BF16)
        BgT = Bg.T.astype(BF16)
        cb = _dot(Cg, BgT)
        hTg = hT_s[g]
        yoff = _dot(Cg, hTg.astype(BF16))
        xw_parts, dect_parts = [], []
        for jj in range(gw // LANES):
            j = g * (gw // LANES) + jj
            xpair = xs[:, j * LANES:(j + 1) * LANES]
            s_parts = []
            for hh in (2 * j, 2 * j + 1):
                L = jnp.where(causal, jnp.exp(cs[:, hh:hh + 1] - csT[hh:hh + 1, :]), 0.0)
                s_parts.append((cb * L * dtT[hh:hh + 1, :]).astype(BF16))
            s2 = jnp.concatenate(s_parts, axis=1)
            x2 = jnp.concatenate([jnp.where(lane < SSD_HEAD_DIM, xpair, 0.0).astype(BF16),
                                  jnp.where(lane >= SSD_HEAD_DIM, xpair, 0.0).astype(BF16)], axis=0)
            ydiag = _dot(s2, x2)
            ys.append(ydiag + yoff[:, jj * LANES:(jj + 1) * LANES] * _pair_expand(ecs, j, lane)
                      + xpair * dexp_ref[:, j * LANES:(j + 1) * LANES])
            xw_parts.append((xpair * _pair_expand(wd, j, lane)).astype(BF16))
            dect_parts.append(_pair_expand(dec_tot, j, lane[0:1, :]))
        xw = jnp.concatenate(xw_parts, axis=1)
        dect = jnp.concatenate(dect_parts, axis=1)
        hT_s[g] = hTg * dect + _dot(BgT, xw)

    y = jnp.concatenate(ys, axis=1)
    y_ref[...] = _ssd_gate_norm(y, z_ref[...], ng_ref).astype(y_ref.dtype)

    @pl.when(c == pl.num_programs(1) - 1)
    def _():
        for g in range(SSD_GROUPS):
            hout_ref[0, g] = hT_s[g].T


def _ssd_chunk(proj, dt, conv0, h0, p, *, nseq, nchunk, row_block0, n_valid):
    T = CHUNK
    d_inner = p["dexp"].shape[1]
    cdim = p["conv_w"].shape[1]
    gw = d_inner // SSD_GROUPS
    nz = cdim // d_inner
    assert cdim % d_inner == 0
    rows = lambda b, c: row_block0 + b * nchunk + c
    shared = (lambda b, c: (0, 0, 0)) if conv0.shape[0] == 1 else (lambda b, c: (b, 0, 0))
    shared4 = (lambda b, c: (0, 0, 0, 0)) if h0.shape[0] == 1 else (lambda b, c: (b, 0, 0, 0))
    const = lambda b, c: (0, 0)
    return pl.pallas_call(
        functools.partial(_ssd_chunk_kernel, T=T, n_valid=n_valid),
        grid=(nseq, nchunk),
        in_specs=[pl.BlockSpec((T, cdim), lambda b, c: (rows(b, c), 0)),
                  pl.BlockSpec((T, d_inner), lambda b, c: (rows(b, c), nz)),
                  pl.BlockSpec((T, LANES), lambda b, c: (rows(b, c), 0)),
                  pl.BlockSpec((1, SUBLANES, cdim), shared),
                  pl.BlockSpec((1, SSD_GROUPS, gw, SSD_STATE), shared4),
                  pl.BlockSpec((CONV_WIDTH, cdim), const),
                  pl.BlockSpec((1, cdim), const),
                  pl.BlockSpec((1, LANES), const),
                  pl.BlockSpec((1, LANES), const),
                  pl.BlockSpec((1, d_inner), const),
                  pl.BlockSpec((1, d_inner), const)],
        out_specs=[pl.BlockSpec((T, d_inner), lambda b, c: (b * nchunk + c, 0)),
                   pl.BlockSpec((1, SSD_GROUPS, gw, SSD_STATE), lambda b, c: (b, 0, 0, 0))],
        out_shape=[jax.ShapeDtypeStruct((nseq * nchunk * T, d_inner), BF16),
                   jax.ShapeDtypeStruct((nseq, SSD_GROUPS, gw, SSD_STATE), F32)],
        scratch_shapes=[pltpu.VMEM((T + SUBLANES, cdim), F32),
                        pltpu.VMEM((SUBLANES, cdim), F32),
                        pltpu.VMEM((SSD_GROUPS, SSD_STATE, gw), F32)],
        compiler_params=_cparams(("parallel", "arbitrary")),
        name="ssd_chunk",
    )(proj, proj, dt, conv0, h0, p["conv_w"], p["conv_b"], p["dt_bias"], p["a_log"], p["dexp"], p["norm"])


def _ssd_step_kernel(xbc_ref, z_ref, dt_ref, c0_ref, c1_ref, c2_ref, h_ref, cw_ref, cb_ref, dtb_ref, alog_ref,
                     dexp_ref, ng_ref, y_ref, hout_ref, xdtT_s, decT_s, b_s, c_s, xs_s, yrow_s):
    b = pl.program_id(0)
    nb, d_inner = xs_s.shape
    gs = SSD_GROUPS * SSD_STATE
    gw = d_inner // SSD_GROUPS

    @pl.when(b == 0)
    def _():
        xbc = _silu(_conv_step(xbc_ref[...], (c0_ref[...], c1_ref[...], c2_ref[...]), cw_ref, cb_ref))
        xs = xbc[:, :d_inner]
        xs_s[...] = xs
        b_s[...] = xbc[:, d_inner:d_inner + gs]
        c_s[...] = xbc[:, d_inner + gs:]
        dtv = _softplus(dt_ref[...] + dtb_ref[...])
        decT_s[...] = jnp.exp(dtv * (-jnp.exp(alog_ref[...]))).T
        dtT = dtv.T
        for j in range(d_inner // LANES):
            xT = xs[:, j * LANES:(j + 1) * LANES].T
            for k in range(2):
                hh = 2 * j + k
                xdtT_s[hh * SSD_HEAD_DIM:(hh + 1) * SSD_HEAD_DIM, :] = (
                    xT[k * SSD_HEAD_DIM:(k + 1) * SSD_HEAD_DIM, :] * dtT[hh:hh + 1, :])

    sel = lax.broadcasted_iota(jnp.int32, (1, nb), 1) == b
    dec_col = jnp.sum(jnp.where(sel, decT_s[...], 0.0), axis=1, keepdims=True)
    brow = b_s[pl.ds(b, 1), :]
    crow = c_s[pl.ds(b, 1), :]
    for g in range(SSD_GROUPS):
        xcol = jnp.sum(jnp.where(sel, xdtT_s[g * gw:(g + 1) * gw, :], 0.0), axis=1, keepdims=True)
        dcol = jnp.concatenate(
            [jnp.broadcast_to(dec_col[hh:hh + 1, :], (SSD_HEAD_DIM, 1))
             for hh in range(g * (gw // SSD_HEAD_DIM), (g + 1) * (gw // SSD_HEAD_DIM))], axis=0)
        hn = h_ref[0, g] * dcol + xcol * brow[:, g * SSD_STATE:(g + 1) * SSD_STATE]
        hout_ref[0, g] = hn
        c8 = jnp.broadcast_to(crow[:, g * SSD_STATE:(g + 1) * SSD_STATE], (SUBLANES, SSD_STATE)).astype(BF16)
        yg = _dot_nt(c8, hn.astype(BF16))
        yrow_s[pl.ds(b, 1), g * gw:(g + 1) * gw] = yg[0:1, :]

    @pl.when(b == nb - 1)
    def _():
        y = yrow_s[...] + xs_s[...] * dexp_ref[...]
        y_ref[...] = _ssd_gate_norm(y, z_ref[...], ng_ref).astype(y_ref.dtype)


def _ssd_step(proj, dt, conv_state, h, p, *, nb):
    d_inner = p["dexp"].shape[1]
    cdim = p["conv_w"].shape[1]
    gw = d_inner // SSD_GROUPS
    gs = SSD_GROUPS * SSD_STATE
    nz = cdim // d_inner
    const = lambda b: (0, 0)
    prev = lambda k: pl.BlockSpec((nb, cdim), lambda b: (0, k))
    conv_state = conv_state.reshape(nb, -1)
    return pl.pallas_call(
        _ssd_step_kernel,
        grid=(nb,),
        in_specs=[pl.BlockSpec((nb, cdim), const),
                  pl.BlockSpec((nb, d_inner), lambda b: (0, nz)),
                  pl.BlockSpec((nb, LANES), const),
                  prev(0), prev(1), prev(2),
                  pl.BlockSpec((1, SSD_GROUPS, gw, SSD_STATE), lambda b: (b, 0, 0, 0)),
                  pl.BlockSpec((CONV_WIDTH, cdim), const),
                  pl.BlockSpec((1, cdim), const),
                  pl.BlockSpec((1, LANES), const),
                  pl.BlockSpec((1, LANES), const),
                  pl.BlockSpec((1, d_inner), const),
                  pl.BlockSpec((1, d_inner), const)],
        out_specs=[pl.BlockSpec((nb, d_inner), const),
                   pl.BlockSpec((1, SSD_GROUPS, gw, SSD_STATE), lambda b: (b, 0, 0, 0))],
        out_shape=[jax.ShapeDtypeStruct((nb, d_inner), BF16),
                   jax.ShapeDtypeStruct(h.shape, F32)],
        scratch_shapes=[pltpu.VMEM((d_inner, nb), F32),
                        pltpu.VMEM((LANES, nb), F32),
                        pltpu.VMEM((nb, gs), F32),
                        pltpu.VMEM((nb, gs), F32),
                        pltpu.VMEM((nb, d_inner), F32),
                        pltpu.VMEM((nb, d_inner), F32)],
        compiler_params=_cparams(("arbitrary",)),
        name="ssd_step",
    )(proj, proj, dt, conv_state, conv_state, conv_state, h, p["conv_w"], p["conv_b"], p["dt_bias"],
      p["a_log"], p["dexp"], p["norm"])


def _shift_rows(x, s, fill, row):
    if s % SUBLANES == 0:
        return jnp.concatenate([jnp.full((s, x.shape[1]), fill, x.dtype), x[:x.shape[0] - s, :]], axis=0)
    return jnp.where(row >= s, pltpu.roll(x, s, 0), fill)


def _lru_chunk_kernel(gate_ref, xr_ref, conv0_ref, h0_ref, cw_ref, cb_ref, wax_ref, ba_ref, bx_ref, lam_ref,
                      y_ref, hout_ref, xp_s, carry_s, hc_s, *, T, n_valid):
    c = pl.program_id(1)

    @pl.when(c == 0)
    def _():
        carry_s[...] = conv0_ref[0]
        hc_s[...] = h0_ref[0]

    xr = _conv_seq(xr_ref[...], xp_s, carry_s, cw_ref, cb_ref, T)
    a, bt = _lru_gates(xr, wax_ref, ba_ref, bx_ref, lam_ref)
    row = lax.broadcasted_iota(jnp.int32, a.shape, 0)
    if n_valid < T:
        a = jnp.where(row < n_valid, a, 1.0)
        bt = jnp.where(row < n_valid, bt, 0.0)
    s = 1
    while s < T:
        bt = a * _shift_rows(bt, s, 0.0, row) + bt
        a = a * _shift_rows(a, s, 1.0, row)
        s *= 2
    h = a * hc_s[...] + bt
    hc_s[...] = h[T - 1:T, :]
    y_ref[...] = (h * _gelu_tanh(gate_ref[...])).astype(y_ref.dtype)

    @pl.when(c == pl.num_programs(1) - 1)
    def _():
        hout_ref[0] = h[T - 1:T, :]


def _lru_chunk(proj, conv0, h0, p, *, nseq, nchunk, row_block0, n_valid):
    T = CHUNK
    d = p["conv_w"].shape[1]
    bw = d // LRU_BLOCKS
    rows = lambda b, c: row_block0 + b * nchunk + c
    shared = (lambda b, c: (0, 0, 0)) if conv0.shape[0] == 1 else (lambda b, c: (b, 0, 0))
    shared_h = (lambda b, c: (0, 0, 0)) if h0.shape[0] == 1 else (lambda b, c: (b, 0, 0))
    const = lambda b, c: (0, 0)
    return pl.pallas_call(
        functools.partial(_lru_chunk_kernel, T=T, n_valid=n_valid),
        grid=(nseq, nchunk),
        in_specs=[pl.BlockSpec((T, d), lambda b, c: (rows(b, c), 0)),
                  pl.BlockSpec((T, d), lambda b, c: (rows(b, c), 1)),
                  pl.BlockSpec((1, SUBLANES, d), shared),
                  pl.BlockSpec((1, 1, d), shared_h),
                  pl.BlockSpec((CONV_WIDTH, d), const),
                  pl.BlockSpec((1, d), const),
                  pl.BlockSpec((LRU_BLOCKS, bw, 2 * bw), lambda b, c: (0, 0, 0)),
                  pl.BlockSpec((1, d), const),
                  pl.BlockSpec((1, d), const),
                  pl.BlockSpec((1, d), const)],
        out_specs=[pl.BlockSpec((T, d), lambda b, c: (b * nchunk + c, 0)),
                   pl.BlockSpec((1, 1, d), lambda b, c: (b, 0, 0))],
        out_shape=[jax.ShapeDtypeStruct((nseq * nchunk * T, d), BF16),
                   jax.ShapeDtypeStruct((nseq, 1, d), F32)],
        scratch_shapes=[pltpu.VMEM((T + SUBLANES, d), F32),
                        pltpu.VMEM((SUBLANES, d), F32),
                        pltpu.VMEM((1, d), F32)],
        compiler_params=_cparams(("parallel", "arbitrary")),
        name="lru_chunk",
    )(proj, proj, conv0, h0, p["conv_w"], p["conv_b"], p["wax"], p["b_a"], p["b_x"], p["lam"])


def _lru_step_kernel(gate_ref, xr_ref, c0_ref, c1_ref, c2_ref, h0_ref, cw_ref, cb_ref, wax_ref, ba_ref, bx_ref,
                     lam_ref, y_ref, hout_ref):
    xr = _conv_step(xr_ref[...], (c0_ref[...], c1_ref[...], c2_ref[...]), cw_ref, cb_ref)
    a, bt = _lru_gates(xr, wax_ref, ba_ref, bx_ref, lam_ref)
    h = a * h0_ref[...] + bt
    hout_ref[...] = h
    y_ref[...] = (h * _gelu_tanh(gate_ref[...])).astype(y_ref.dtype)


def _lru_step(proj, conv_state, h0, p, *, nb):
    d = p["conv_w"].shape[1]
    bw = d // LRU_BLOCKS
    const = lambda i: (0, 0)
    prev = lambda k: pl.BlockSpec((nb, d), lambda i: (0, k))
    conv_state = conv_state.reshape(nb, -1)
    return pl.pallas_call(
        _lru_step_kernel,
        grid=(1,),
        in_specs=[pl.BlockSpec((nb, d), const),
                  pl.BlockSpec((nb, d), lambda i: (0, 1)),
                  prev(0), prev(1), prev(2),
                  pl.BlockSpec((nb, d), const),
                  pl.BlockSpec((CONV_WIDTH, d), const),
                  pl.BlockSpec((1, d), const),
                  pl.BlockSpec((LRU_BLOCKS, bw, 2 * bw), lambda i: (0, 0, 0)),
                  pl.BlockSpec((1, d), const),
                  pl.BlockSpec((1, d), const),
                  pl.BlockSpec((1, d), const)],
        out_specs=[pl.BlockSpec((nb, d), const), pl.BlockSpec((nb, d), const)],
        out_shape=[jax.ShapeDtypeStruct((nb, d), BF16), jax.ShapeDtypeStruct((nb, d), F32)],
        compiler_params=_cparams(("arbitrary",)),
        name="lru_step",
    )(proj, proj, conv_state, conv_state, conv_state, h0, p["conv_w"], p["conv_b"], p["wax"], p["b_a"],
      p["b_x"], p["lam"])


def _row(v):
    return v.reshape(1, -1).astype(F32)


def _pad_lanes(v):
    return jnp.pad(v.astype(F32), (0, LANES - v.shape[0])).reshape(1, LANES)


def _carry_rows(rows):
    return jnp.pad(rows, ((SUBLANES - (CONV_WIDTH - 1), 0), (0, 0)))[None]


def kernel(x_prompt, x_sample, state_ssd_conv, state_ssd_h, state_lru_conv, state_lru_h, meta_tokens, norm_mix_pre, norm_mix_post, norm_ffn_pre, norm_ffn_post, ssd_w_in, ssd_conv_w, ssd_conv_b, ssd_dt_bias, ssd_a_log, ssd_d, ssd_norm, ssd_w_out, lru_w_in, lru_b_in, lru_conv_w, lru_conv_b, lru_w_a, lru_b_a, lru_w_x, lru_b_x, lru_lambda, lru_w_out, lru_b_out, ffn_w1, ffn_w2):
    B, S, D = x_prompt.shape
    nb = x_sample.shape[0]
    n_meta = meta_tokens.shape[0]
    depth = norm_mix_pre.shape[0]
    T = CHUNK
    n_heads = ssd_dt_bias.shape[1]
    d_inner = n_heads * SSD_HEAD_DIM
    cdim = ssd_conv_w.shape[2]
    d_rnn = lru_conv_w.shape[2]
    gw = d_inner // SSD_GROUPS
    assert x_sample.shape[1] == 1 and nb == T and S % T == 0 and CONV_WIDTH - 1 <= n_meta <= T
    nchunk = S // T
    meta_blk = nb // T

    xs = jnp.concatenate([x_sample[:, 0, :], meta_tokens, jnp.zeros((T - n_meta, D), F32)], axis=0)
    xp = x_prompt.reshape(B * S, D)

    p_ssd_conv, p_ssd_h, p_lru_conv, p_lru_h = [], [], [], []
    s_ssd_conv, s_ssd_h, s_lru_conv, s_lru_h = [], [], [], []
    for i in range(depth):
        j = i // 2
        g_pre = _row(norm_mix_pre[i])
        if i % 2 == 0:
            w = ssd_w_in[j]
            w_main = jnp.concatenate([w[:, d_inner:d_inner + cdim], w[:, :d_inner]], axis=1).astype(BF16)
            w_dt = jnp.pad(w[:, d_inner + cdim:], ((0, 0), (0, LANES - n_heads))).astype(BF16)
            p = dict(conv_w=ssd_conv_w[j], conv_b=_row(ssd_conv_b[j]), dt_bias=_pad_lanes(ssd_dt_bias[j]),
                     a_log=_pad_lanes(ssd_a_log[j]), dexp=_row(jnp.repeat(ssd_d[j], SSD_HEAD_DIM)),
                     norm=_row(ssd_norm[j]))
            w_out, b_out = ssd_w_out[j].astype(BF16), jnp.zeros((1, D), F32)

            proj_s, dt_s = _norm_matmul(xs, g_pre, w_main, w_small=w_dt, tm=1024, tn=1024)
            y_samp, h_samp = _ssd_step(proj_s, dt_s, state_ssd_conv[j],
                                       state_ssd_h[j].reshape(nb, SSD_GROUPS, gw, SSD_STATE), p, nb=nb)
            y_meta, h_meta = _ssd_chunk(proj_s, dt_s, jnp.zeros((1, SUBLANES, cdim), F32),
                                        jnp.zeros((1, SSD_GROUPS, gw, SSD_STATE), F32), p,
                                        nseq=1, nchunk=1, row_block0=meta_blk, n_valid=n_meta)
            conv_meta = _carry_rows(proj_s[nb + n_meta - (CONV_WIDTH - 1):nb + n_meta, :cdim])
            s_ssd_conv.append(jnp.concatenate([state_ssd_conv[j][:, 1:], proj_s[:nb, None, :cdim]], axis=1))
            s_ssd_h.append(h_samp.reshape(state_ssd_h[j].shape))
            y_s = jnp.concatenate([y_samp, y_meta], axis=0)

            proj_p, dt_p = _norm_matmul(xp, g_pre, w_main, w_small=w_dt, tm=1024, tn=1024)
            y_p, h_p = _ssd_chunk(proj_p, dt_p, conv_meta, h_meta, p,
                                  nseq=B, nchunk=nchunk, row_block0=0, n_valid=T)
            p_ssd_conv.append(proj_p.reshape(B, S, -1)[:, S - (CONV_WIDTH - 1):, :cdim])
            p_ssd_h.append(h_p.reshape(B, n_heads, SSD_HEAD_DIM, SSD_STATE))
        else:
            p = dict(conv_w=lru_conv_w[j], conv_b=_row(lru_conv_b[j]),
                     wax=jnp.concatenate([lru_w_a[j], lru_w_x[j]], axis=-1).astype(BF16),
                     b_a=_row(lru_b_a[j]), b_x=_row(lru_b_x[j]), lam=_row(lru_lambda[j]))
            w_in, b_in = lru_w_in[j].astype(BF16), _row(lru_b_in[j])
            w_out, b_out = lru_w_out[j].astype(BF16), _row(lru_b_out[j])

            proj_s = _norm_matmul(xs, g_pre, w_in, bias=b_in, tm=1024, tn=1024)
            y_samp, h_samp = _lru_step(proj_s, state_lru_conv[j], state_lru_h[j], p, nb=nb)
            y_meta, h_meta = _lru_chunk(proj_s, jnp.zeros((1, SUBLANES, d_rnn), F32),
                                        jnp.zeros((1, 1, d_rnn), F32), p,
                                        nseq=1, nchunk=1, row_block0=meta_blk, n_valid=n_meta)
            conv_meta = _carry_rows(proj_s[nb + n_meta - (CONV_WIDTH - 1):nb + n_meta, d_rnn:])
            s_lru_conv.append(jnp.concatenate([state_lru_conv[j][:, 1:], proj_s[:nb, None, d_rnn:]], axis=1))
            s_lru_h.append(h_samp)
            y_s = jnp.concatenate([y_samp, y_meta], axis=0)

            proj_p = _norm_matmul(xp, g_pre, w_in, bias=b_in, tm=1024, tn=1024)
            y_p, h_p = _lru_chunk(proj_p, conv_meta, h_meta, p,
                                  nseq=B, nchunk=nchunk, row_block0=0, n_valid=T)
            p_lru_conv.append(proj_p.reshape(B, S, -1)[:, S - (CONV_WIDTH - 1):, d_rnn:])
            p_lru_h.append(h_p.reshape(B, d_rnn))

        ffn = (w_out, b_out, _row(norm_mix_post[i]), _row(norm_ffn_pre[i]), ffn_w1[i].astype(BF16),
               ffn_w2[i].astype(BF16), _row(norm_ffn_post[i]))
        xs = _out_ffn(y_s, xs, *ffn, tm=512)
        xp = _out_ffn(y_p, xp, *ffn, tm=512)

    return (xp.reshape(B, S, D), xs[:nb].reshape(nb, 1, D),
            jnp.stack(p_ssd_conv), jnp.stack(p_ssd_h), jnp.stack(p_lru_conv), jnp.stack(p_lru_h),
            jnp.stack(s_ssd_conv), jnp.stack(s_ssd_h), jnp.stack(s_lru_conv), jnp.stack(s_lru_h))
```

```python
import functools
import math

import jax
import jax.numpy as jnp
from jax import lax
from jax.experimental import pallas as pl
from jax.experimental.pallas import tpu as pltpu

F32 = jnp.float32
BF16 = jnp.bfloat16

EPS = 1e-6
CONV_WIDTH = 4
SSD_HEAD_DIM = 64
SSD_GROUPS = 8
SSD_STATE = 128
LRU_BLOCKS = 8
LRU_C = 8.0
CHUNK = 128
LANES = 128
SUBLANES = 8
VMEM_LIMIT = 56 * 1024 * 1024


def _cparams(sem):
    return pltpu.CompilerParams(dimension_semantics=sem, vmem_limit_bytes=VMEM_LIMIT)


def _rms(x, g):
    return x * lax.rsqrt(jnp.mean(x * x, axis=-1, keepdims=True) + EPS) * g


def _softplus(x):
    return jnp.maximum(x, 0.0) + jnp.log1p(jnp.exp(-jnp.abs(x)))


def _silu(x):
    return x * jax.nn.sigmoid(x)


def _gelu_tanh(x):
    return 0.5 * x * (1.0 + jnp.tanh(math.sqrt(2.0 / math.pi) * (x + 0.044715 * (x * x * x))))


def _dot(a, b):
    return jnp.dot(a, b, preferred_element_type=F32)


def _dot_nt(a, b):
    return lax.dot_general(a, b, (((1,), (1,)), ((), ())), preferred_element_type=F32)


def _pair_expand(q, j, lane):
    return jnp.where(lane < SSD_HEAD_DIM, q[:, 2 * j:2 * j + 1], q[:, 2 * j + 1:2 * j + 2])


def _conv_seq(x, xp_s, carry_s, cw_ref, cb_ref, T):
    xp_s[0:SUBLANES, :] = carry_s[...]
    xp_s[SUBLANES:SUBLANES + T, :] = x
    carry_s[...] = x[T - SUBLANES:T, :]
    out = cb_ref[...] + cw_ref[CONV_WIDTH - 1:CONV_WIDTH, :] * x
    for k in range(CONV_WIDTH - 1):
        out = out + cw_ref[k:k + 1, :] * xp_s[pl.ds(SUBLANES - (CONV_WIDTH - 1) + k, T), :]
    return out


def _conv_step(x, prevs, cw_ref, cb_ref):
    out = cb_ref[...] + cw_ref[CONV_WIDTH - 1:CONV_WIDTH, :] * x
    for k in range(CONV_WIDTH - 1):
        out = out + cw_ref[k:k + 1, :] * prevs[k]
    return out


def _ssd_gate_norm(y, z, ng_ref):
    y = y * _silu(z)
    gw = y.shape[1] // SSD_GROUPS
    parts = []
    for g in range(SSD_GROUPS):
        seg = y[:, g * gw:(g + 1) * gw]
        parts.append(seg * lax.rsqrt(jnp.mean(seg * seg, axis=-1, keepdims=True) + EPS))
    return jnp.concatenate(parts, axis=1) * ng_ref[...]


def _lru_gates(xr, wax_ref, ba_ref, bx_ref, lam_ref):
    bw = xr.shape[1] // LRU_BLOCKS
    ra, ix = [], []
    for k in range(LRU_BLOCKS):
        g = _dot(xr[:, k * bw:(k + 1) * bw].astype(BF16), wax_ref[k])
        ra.append(g[:, :bw])
        ix.append(g[:, bw:])
    r = jax.nn.sigmoid(jnp.concatenate(ra, axis=1) + ba_ref[...])
    i = jax.nn.sigmoid(jnp.concatenate(ix, axis=1) + bx_ref[...])
    log_a = (-LRU_C) * r * _softplus(-lam_ref[...])
    a = jnp.exp(log_a)
    v = -jnp.tanh(log_a) * (a * a + 1.0)
    mult = jnp.where(v > 0.0, v * lax.rsqrt(v), 0.0)
    return a, mult * i * xr


def _norm_matmul_kernel(*refs, has_bias, has_small):
    it = iter(refs)
    x_ref, g_ref, w_ref = next(it), next(it), next(it)
    b_ref = next(it) if has_bias else None
    ws_ref = next(it) if has_small else None
    o_ref = next(it)
    os_ref = next(it) if has_small else None
    xn_s = next(it)

    @pl.when(pl.program_id(1) == 0)
    def _():
        xn = _rms(x_ref[...], g_ref[...]).astype(BF16)
        xn_s[...] = xn
        if has_small:
            os_ref[...] = _dot(xn, ws_ref[...])

    acc = _dot(xn_s[...], w_ref[...])
    if has_bias:
        acc = acc + b_ref[...]
    o_ref[...] = acc


def _norm_matmul(x, g, w, bias=None, w_small=None, *, tm, tn):
    M, D = x.shape
    N = w.shape[1]
    tm = min(tm, M)
    assert M % tm == 0 and N % tn == 0
    in_specs = [pl.BlockSpec((tm, D), lambda i, j: (i, 0)),
                pl.BlockSpec((1, D), lambda i, j: (0, 0)),
                pl.BlockSpec((D, tn), lambda i, j: (0, j))]
    args = [x, g, w]
    out_shape = [jax.ShapeDtypeStruct((M, N), F32)]
    out_specs = [pl.BlockSpec((tm, tn), lambda i, j: (i, j))]
    if bias is not None:
        in_specs.append(pl.BlockSpec((1, tn), lambda i, j: (0, j)))
        args.append(bias)
    if w_small is not None:
        ns = w_small.shape[1]
        in_specs.append(pl.BlockSpec((D, ns), lambda i, j: (0, 0)))
        args.append(w_small)
        out_shape.append(jax.ShapeDtypeStruct((M, ns), F32))
        out_specs.append(pl.BlockSpec((tm, ns), lambda i, j: (i, 0)))
    res = pl.pallas_call(
        functools.partial(_norm_matmul_kernel, has_bias=bias is not None, has_small=w_small is not None),
        grid=(M // tm, N // tn),
        in_specs=in_specs, out_specs=out_specs, out_shape=out_shape,
        scratch_shapes=[pltpu.VMEM((tm, D), BF16)],
        compiler_params=_cparams(("parallel", "arbitrary")),
        name="norm_matmul",
    )(*args)
    return res if w_small is not None else res[0]


def _out_ffn_kernel(y_ref, x_ref, wo_ref, bo_ref, gpost_ref, gpre_ref, w1_ref, w2_ref, gfpost_ref, o_ref, *, fc):
    m = _dot(y_ref[...], wo_ref[...]) + bo_ref[...]
    x1 = x_ref[...] + _rms(m, gpost_ref[...])
    hn = _rms(x1, gpre_ref[...]).astype(BF16)
    dff = w1_ref.shape[1]
    acc = None
    for c in range(dff // fc):
        h = jnp.maximum(_dot(hn, w1_ref[:, c * fc:(c + 1) * fc]), 0.0)
        part = _dot((h * h).astype(BF16), w2_ref[c * fc:(c + 1) * fc, :])
        acc = part if acc is None else acc + part
    o_ref[...] = x1 + _rms(acc, gfpost_ref[...])


def _out_ffn(y, x, wo, bo, gpost, gpre, w1, w2, gfpost, *, tm, fc=1024):
    M, D = x.shape
    K = y.shape[1]
    dff = w1.shape[1]
    tm = min(tm, M)
    assert M % tm == 0 and dff % fc == 0
    const = lambda i: (0, 0)
    once = dict(pipeline_mode=pl.Buffered(1))
    return pl.pallas_call(
        functools.partial(_out_ffn_kernel, fc=fc),
        grid=(M // tm,),
        in_specs=[pl.BlockSpec((tm, K), lambda i: (i, 0)),
                  pl.BlockSpec((tm, D), lambda i: (i, 0)),
                  pl.BlockSpec((K, D), const, **once),
                  pl.BlockSpec((1, D), const),
                  pl.BlockSpec((1, D), const),
                  pl.BlockSpec((1, D), const),
                  pl.BlockSpec((D, dff), const, **once),
                  pl.BlockSpec((dff, D), const, **once),
                  pl.BlockSpec((1, D), const)],
        out_specs=pl.BlockSpec((tm, D), lambda i: (i, 0)),
        out_shape=jax.ShapeDtypeStruct((M, D), F32),
        compiler_params=_cparams(("parallel",)),
        name="out_ffn",
    )(y, x, wo, bo, gpost, gpre, w1, w2, gfpost)


def _ssd_chunk_kernel(xbc_ref, z_ref, dt_ref, conv0_ref, h0_ref, cw_ref, cb_ref, dtb_ref, alog_ref, dexp_ref,
                      ng_ref, y_ref, hout_ref, xp_s, carry_s, hT_s, *, T, n_valid):
    c = pl.program_id(1)
    d_inner = z_ref.shape[1]
    gs = SSD_GROUPS * SSD_STATE
    gw = d_inner // SSD_GROUPS

    @pl.when(c == 0)
    def _():
        carry_s[...] = conv0_ref[0]
        for g in range(SSD_GROUPS):
            hT_s[g] = h0_ref[0, g].T

    xbc = _silu(_conv_seq(xbc_ref[...], xp_s, carry_s, cw_ref, cb_ref, T))
    xs = xbc[:, :d_inner]
    Bm = xbc[:, d_inner:d_inner + gs]
    Cm = xbc[:, d_inner + gs:]

    row = lax.broadcasted_iota(jnp.int32, (T, T), 0)
    col = lax.broadcasted_iota(jnp.int32, (T, T), 1)
    causal = row >= col
    lane = lax.broadcasted_iota(jnp.int32, (T, LANES), 1)

    dtv = _softplus(dt_ref[...] + dtb_ref[...])
    if n_valid < T:
        dtv = jnp.where(lax.broadcasted_iota(jnp.int32, (T, LANES), 0) < n_valid, dtv, 0.0)
    a = dtv * (-jnp.exp(alog_ref[...]))
    cs = jnp.dot(causal.astype(F32), a, precision=lax.Precision.HIGHEST, preferred_element_type=F32)
    csT = cs.T
    dtT = dtv.T
    total = cs[T - 1:T, :]
    ecs = jnp.exp(cs)
    wd = dtv * jnp.exp(total - cs)
    dec_tot = jnp.exp(total)

    ys = []
    for g in range(SSD_GROUPS):
        Bg = Bm[:, g * SSD_STATE:(g + 1) * SSD_STATE]
        Cg = Cm[:, g * SSD_STATE:(g + 1) * SSD_STATE].astype(BF16)
        BgT = Bg.T.astype(BF16)
        cb = _dot(Cg, BgT)
        hTg = hT_s[g]
        yoff = _dot(Cg, hTg.astype(BF16))
        xw_parts, dect_parts = [], []
        for jj in range(gw // LANES):
            j = g * (gw // LANES) + jj
            xpair = xs[:, j * LANES:(j + 1) * LANES]
            s_parts = []
            for hh in (2 * j, 2 * j + 1):
                L = jnp.where(causal, jnp.exp(cs[:, hh:hh + 1] - csT[hh:hh + 1, :]), 0.0)
                s_parts.append((cb * L * dtT[hh:hh + 1, :]).astype(BF16))
            s2 = jnp.concatenate(s_parts, axis=1)
            x2 = jnp.concatenate([jnp.where(lane < SSD_HEAD_DIM, xpair, 0.0).astype(BF16),
                                  jnp.where(lane >= SSD_HEAD_DIM, xpair, 0.0).astype(BF16)], axis=0)
            ydiag = _dot(s2, x2)
            ys.append(ydiag + yoff[:, jj * LANES:(jj + 1) * LANES] * _pair_expand(ecs, j, lane)
                      + xpair * dexp_ref[:, j * LANES:(j + 1) * LANES])
            xw_parts.append((xpair * _pair_expand(wd, j, lane)).astype(BF16))
            dect_parts.append(_pair_expand(dec_tot, j, lane[0:1, :]))
        xw = jnp.concatenate(xw_parts, axis=1)
        dect = jnp.concatenate(dect_parts, axis=1)
        hT_s[g] = hTg * dect + _dot(BgT, xw)

    y = jnp.concatenate(ys, axis=1)
    y_ref[...] = _ssd_gate_norm(y, z_ref[...], ng_ref).astype(y_ref.dtype)

    @pl.when(c == pl.num_programs(1) - 1)
    def _():
        for g in range(SSD_GROUPS):
            hout_ref[0, g] = hT_s[g].T


def _ssd_chunk(proj, dt, conv0, h0, p, *, nseq, nchunk, row_block0, n_valid):
    T = CHUNK
    d_inner = p["dexp"].shape[1]
    cdim = p["conv_w"].shape[1]
    gw = d_inner // SSD_GROUPS
    nz = cdim // d_inner
    assert cdim % d_inner == 0
    rows = lambda b, c: row_block0 + b * nchunk + c
    shared = (lambda b, c: (0, 0, 0)) if conv0.shape[0] == 1 else (lambda b, c: (b, 0, 0))
    shared4 = (lambda b, c: (0, 0, 0, 0)) if h0.shape[0] == 1 else (lambda b, c: (b, 0, 0, 0))
    const = lambda b, c: (0, 0)
    return pl.pallas_call(
        functools.partial(_ssd_chunk_kernel, T=T, n_valid=n_valid),
        grid=(nseq, nchunk),
        in_specs=[pl.BlockSpec((T, cdim), lambda b, c: (rows(b, c), 0)),
                  pl.BlockSpec((T, d_inner), lambda b, c: (rows(b, c), nz)),
                  pl.BlockSpec((T, LANES), lambda b, c: (rows(b, c), 0)),
                  pl.BlockSpec((1, SUBLANES, cdim), shared),
                  pl.BlockSpec((1, SSD_GROUPS, gw, SSD_STATE), shared4),
                  pl.BlockSpec((CONV_WIDTH, cdim), const),
                  pl.BlockSpec((1, cdim), const),
                  pl.BlockSpec((1, LANES), const),
                  pl.BlockSpec((1, LANES), const),
                  pl.BlockSpec((1, d_inner), const),
                  pl.BlockSpec((1, d_inner), const)],
        out_specs=[pl.BlockSpec((T, d_inner), lambda b, c: (b * nchunk + c, 0)),
                   pl.BlockSpec((1, SSD_GROUPS, gw, SSD_STATE), lambda b, c: (b, 0, 0, 0))],
        out_shape=[jax.ShapeDtypeStruct((nseq * nchunk * T, d_inner), BF16),
                   jax.ShapeDtypeStruct((nseq, SSD_GROUPS, gw, SSD_STATE), F32)],
        scratch_shapes=[pltpu.VMEM((T + SUBLANES, cdim), F32),
                        pltpu.VMEM((SUBLANES, cdim), F32),
                        pltpu.VMEM((SSD_GROUPS, SSD_STATE, gw), F32)],
        compiler_params=_cparams(("parallel", "arbitrary")),
        name="ssd_chunk",
    )(proj, proj, dt, conv0, h0, p["conv_w"], p["conv_b"], p["dt_bias"], p["a_log"], p["dexp"], p["norm"])


def _ssd_step_kernel(*refs, seq_per_step, has_prev):
    (xbc_ref, z_ref, dt_ref, c0_ref, c1_ref, c2_ref, h_ref, cw_ref, cb_ref, dtb_ref, alog_ref, dexp_ref,
     ng_ref) = refs[:13]
    y_ref, hout_ref, xdtT_s, decT_s, b_s, c_s, xs_s, yrow_s = refs[13 + has_prev:]
    s = pl.program_id(0)
    nb, d_inner = xs_s.shape
    gs = SSD_GROUPS * SSD_STATE
    gw = d_inner // SSD_GROUPS

    @pl.when(s == 0)
    def _():
        xbc = _silu(_conv_step(xbc_ref[...], (c0_ref[...], c1_ref[...], c2_ref[...]), cw_ref, cb_ref))
        xs = xbc[:, :d_inner]
        xs_s[...] = xs
        b_s[...] = xbc[:, d_inner:d_inner + gs]
        c_s[...] = xbc[:, d_inner + gs:]
        dtv = _softplus(dt_ref[...] + dtb_ref[...])
        decT_s[...] = jnp.exp(dtv * (-jnp.exp(alog_ref[...]))).T
        dtT = dtv.T
        for j in range(d_inner // LANES):
            xT = xs[:, j * LANES:(j + 1) * LANES].T
            for k in range(2):
                hh = 2 * j + k
                xdtT_s[hh * SSD_HEAD_DIM:(hh + 1) * SSD_HEAD_DIM, :] = (
                    xT[k * SSD_HEAD_DIM:(k + 1) * SSD_HEAD_DIM, :] * dtT[hh:hh + 1, :])

    for q in range(seq_per_step):
        b = s * seq_per_step + q
        sel = lax.broadcasted_iota(jnp.int32, (1, nb), 1) == b
        dec_col = jnp.sum(jnp.where(sel, decT_s[...], 0.0), axis=1, keepdims=True)
        brow = b_s[pl.ds(b, 1), :]
        crow = c_s[pl.ds(b, 1), :]
        for g in range(SSD_GROUPS):
            xcol = jnp.sum(jnp.where(sel, xdtT_s[g * gw:(g + 1) * gw, :], 0.0), axis=1, keepdims=True)
            dcol = jnp.concatenate(
                [jnp.broadcast_to(dec_col[hh:hh + 1, :], (SSD_HEAD_DIM, 1))
                 for hh in range(g * (gw // SSD_HEAD_DIM), (g + 1) * (gw // SSD_HEAD_DIM))], axis=0)
            hn = h_ref[q, g] * dcol + xcol * brow[:, g * SSD_STATE:(g + 1) * SSD_STATE]
            hout_ref[q, g] = hn
            c8 = jnp.broadcast_to(crow[:, g * SSD_STATE:(g + 1) * SSD_STATE], (SUBLANES, SSD_STATE)).astype(BF16)
            yg = _dot_nt(c8, hn.astype(BF16))
            yrow_s[pl.ds(b, 1), g * gw:(g + 1) * gw] = yg[0:1, :]

    @pl.when(s == pl.num_programs(0) - 1)
    def _():
        y = yrow_s[...] + xs_s[...] * dexp_ref[...]
        y_ref[...] = _ssd_gate_norm(y, z_ref[...], ng_ref).astype(y_ref.dtype)


def _ssd_step(proj, dt, conv_state, h_all, h_out_prev, p, *, nb, layer, seq_per_step=4):
    d_inner = p["dexp"].shape[1]
    cdim = p["conv_w"].shape[1]
    gw = d_inner // SSD_GROUPS
    gs = SSD_GROUPS * SSD_STATE
    nz = cdim // d_inner
    assert nb % seq_per_step == 0
    blk0 = layer * (nb // seq_per_step)
    const = lambda s: (0, 0)
    prev = lambda k: pl.BlockSpec((nb, cdim), lambda s: (0, k))
    conv_state = conv_state.reshape(nb, -1)
    hspec = pl.BlockSpec((seq_per_step, SSD_GROUPS, gw, SSD_STATE), lambda s: (blk0 + s, 0, 0, 0))
    has_prev = h_out_prev is not None
    in_specs = [pl.BlockSpec((nb, cdim), const),
                pl.BlockSpec((nb, d_inner), lambda s: (0, nz)),
                pl.BlockSpec((nb, LANES), const),
                prev(0), prev(1), prev(2),
                hspec,
                pl.BlockSpec((CONV_WIDTH, cdim), const),
                pl.BlockSpec((1, cdim), const),
                pl.BlockSpec((1, LANES), const),
                pl.BlockSpec((1, LANES), const),
                pl.BlockSpec((1, d_inner), const),
                pl.BlockSpec((1, d_inner), const)]
    args = [proj, proj, dt, conv_state, conv_state, conv_state, h_all, p["conv_w"], p["conv_b"], p["dt_bias"],
            p["a_log"], p["dexp"], p["norm"]]
    if has_prev:
        in_specs.append(pl.BlockSpec(memory_space=pl.ANY))
        args.append(h_out_prev)
    return pl.pallas_call(
        functools.partial(_ssd_step_kernel, seq_per_step=seq_per_step, has_prev=has_prev),
        grid=(nb // seq_per_step,),
        in_specs=in_specs,
        out_specs=[pl.BlockSpec((nb, d_inner), const), hspec],
        out_shape=[jax.ShapeDtypeStruct((nb, d_inner), BF16),
                   jax.ShapeDtypeStruct(h_all.shape, F32)],
        scratch_shapes=[pltpu.VMEM((d_inner, nb), F32),
                        pltpu.VMEM((LANES, nb), F32),
                        pltpu.VMEM((nb, gs), F32),
                        pltpu.VMEM((nb, gs), F32),
                        pltpu.VMEM((nb, d_inner), F32),
                        pltpu.VMEM((nb, d_inner), F32)],
        input_output_aliases={len(args) - 1: 1} if has_prev else {},
        compiler_params=_cparams(("arbitrary",)),
        name="ssd_step",
    )(*args)


def _shift_rows(x, s, fill, row):
    if s % SUBLANES == 0:
        return jnp.concatenate([jnp.full((s, x.shape[1]), fill, x.dtype), x[:x.shape[0] - s, :]], axis=0)
    return jnp.where(row >= s, pltpu.roll(x, s, 0), fill)


def _lru_chunk_kernel(gate_ref, xr_ref, conv0_ref, h0_ref, cw_ref, cb_ref, wax_ref, ba_ref, bx_ref, lam_ref,
                      y_ref, hout_ref, xp_s, carry_s, hc_s, *, T, n_valid):
    c = pl.program_id(1)

    @pl.when(c == 0)
    def _():
        carry_s[...] = conv0_ref[0]
        hc_s[...] = h0_ref[0]

    xr = _conv_seq(xr_ref[...], xp_s, carry_s, cw_ref, cb_ref, T)
    a, bt = _lru_gates(xr, wax_ref, ba_ref, bx_ref, lam_ref)
    row = lax.broadcasted_iota(jnp.int32, a.shape, 0)
    if n_valid < T:
        a = jnp.where(row < n_valid, a, 1.0)
        bt = jnp.where(row < n_valid, bt, 0.0)
    s = 1
    while s < T:
        bt = a * _shift_rows(bt, s, 0.0, row) + bt
        a = a * _shift_rows(a, s, 1.0, row)
        s *= 2
    h = a * hc_s[...] + bt
    hc_s[...] = h[T - 1:T, :]
    y_ref[...] = (h * _gelu_tanh(gate_ref[...])).astype(y_ref.dtype)

    @pl.when(c == pl.num_programs(1) - 1)
    def _():
        hout_ref[0] = h[T - 1:T, :]


def _lru_chunk(proj, conv0, h0, p, *, nseq, nchunk, row_block0, n_valid):
    T = CHUNK
    d = p["conv_w"].shape[1]
    bw = d // LRU_BLOCKS
    rows = lambda b, c: row_block0 + b * nchunk + c
    shared = (lambda b, c: (0, 0, 0)) if conv0.shape[0] == 1 else (lambda b, c: (b, 0, 0))
    shared_h = (lambda b, c: (0, 0, 0)) if h0.shape[0] == 1 else (lambda b, c: (b, 0, 0))
    const = lambda b, c: (0, 0)
    return pl.pallas_call(
        functools.partial(_lru_chunk_kernel, T=T, n_valid=n_valid),
        grid=(nseq, nchunk),
        in_specs=[pl.BlockSpec((T, d), lambda b, c: (rows(b, c), 0)),
                  pl.BlockSpec((T, d), lambda b, c: (rows(b, c), 1)),
                  pl.BlockSpec((1, SUBLANES, d), shared),
                  pl.BlockSpec((1, 1, d), shared_h),
                  pl.BlockSpec((CONV_WIDTH, d), const),
                  pl.BlockSpec((1, d), const),
                  pl.BlockSpec((LRU_BLOCKS, bw, 2 * bw), lambda b, c: (0, 0, 0)),
                  pl.BlockSpec((1, d), const),
                  pl.BlockSpec((1, d), const),
                  pl.BlockSpec((1, d), const)],
        out_specs=[pl.BlockSpec((T, d), lambda b, c: (b * nchunk + c, 0)),
                   pl.BlockSpec((1, 1, d), lambda b, c: (b, 0, 0))],
        out_shape=[jax.ShapeDtypeStruct((nseq * nchunk * T, d), BF16),
                   jax.ShapeDtypeStruct((nseq, 1, d), F32)],
        scratch_shapes=[pltpu.VMEM((T + SUBLANES, d), F32),
                        pltpu.VMEM((SUBLANES, d), F32),
                        pltpu.VMEM((1, d), F32)],
        compiler_params=_cparams(("parallel", "arbitrary")),
        name="lru_chunk",
    )(proj, proj, conv0, h0, p["conv_w"], p["conv_b"], p["wax"], p["b_a"], p["b_x"], p["lam"])


def _lru_step_kernel(gate_ref, xr_ref, c0_ref, c1_ref, c2_ref, h0_ref, cw_ref, cb_ref, wax_ref, ba_ref, bx_ref,
                     lam_ref, y_ref, hout_ref):
    xr = _conv_step(xr_ref[...], (c0_ref[...], c1_ref[...], c2_ref[...]), cw_ref, cb_ref)
    a, bt = _lru_gates(xr, wax_ref, ba_ref, bx_ref, lam_ref)
    h = a * h0_ref[...] + bt
    hout_ref[...] = h
    y_ref[...] = (h * _gelu_tanh(gate_ref[...])).astype(y_ref.dtype)


def _lru_step(proj, conv_state, h0, p, *, nb):
    d = p["conv_w"].shape[1]
    bw = d // LRU_BLOCKS
    const = lambda i: (0, 0)
    prev = lambda k: pl.BlockSpec((nb, d), lambda i: (0, k))
    conv_state = conv_state.reshape(nb, -1)
    return pl.pallas_call(
        _lru_step_kernel,
        grid=(1,),
        in_specs=[pl.BlockSpec((nb, d), const),
                  pl.BlockSpec((nb, d), lambda i: (0, 1)),
                  prev(0), prev(1), prev(2),
                  pl.BlockSpec((nb, d), const),
                  pl.BlockSpec((CONV_WIDTH, d), const),
                  pl.BlockSpec((1, d), const),
                  pl.BlockSpec((LRU_BLOCKS, bw, 2 * bw), lambda i: (0, 0, 0)),
                  pl.BlockSpec((1, d), const),
                  pl.BlockSpec((1, d), const),
                  pl.BlockSpec((1, d), const)],
        out_specs=[pl.BlockSpec((nb, d), const), pl.BlockSpec((nb, d), const)],
        out_shape=[jax.ShapeDtypeStruct((nb, d), BF16), jax.ShapeDtypeStruct((nb, d), F32)],
        compiler_params=_cparams(("arbitrary",)),
        name="lru_step",
    )(proj, proj, conv_state, conv_state, conv_state, h0, p["conv_w"], p["conv_b"], p["wax"], p["b_a"],
      p["b_x"], p["lam"])


def _row(v):
    return v.reshape(1, -1).astype(F32)


def _pad_lanes(v):
    return jnp.pad(v.astype(F32), (0, LANES - v.shape[0])).reshape(1, LANES)


def _carry_rows(rows):
    return jnp.pad(rows, ((SUBLANES - (CONV_WIDTH - 1), 0), (0, 0)))[None]


def kernel(x_prompt, x_sample, state_ssd_conv, state_ssd_h, state_lru_conv, state_lru_h, meta_tokens, norm_mix_pre, norm_mix_post, norm_ffn_pre, norm_ffn_post, ssd_w_in, ssd_conv_w, ssd_conv_b, ssd_dt_bias, ssd_a_log, ssd_d, ssd_norm, ssd_w_out, lru_w_in, lru_b_in, lru_conv_w, lru_conv_b, lru_w_a, lru_b_a, lru_w_x, lru_b_x, lru_lambda, lru_w_out, lru_b_out, ffn_w1, ffn_w2):
    B, S, D = x_prompt.shape
    nb = x_sample.shape[0]
    n_meta = meta_tokens.shape[0]
    depth = norm_mix_pre.shape[0]
    T = CHUNK
    n_heads = ssd_dt_bias.shape[1]
    d_inner = n_heads * SSD_HEAD_DIM
    cdim = ssd_conv_w.shape[2]
    d_rnn = lru_conv_w.shape[2]
    gw = d_inner // SSD_GROUPS
    assert x_sample.shape[1] == 1 and nb == T and S % T == 0 and CONV_WIDTH - 1 <= n_meta <= T
    nchunk = S // T
    meta_blk = nb // T

    xs = jnp.concatenate([x_sample[:, 0, :], meta_tokens, jnp.zeros((T - n_meta, D), F32)], axis=0)
    xp = x_prompt.reshape(B * S, D)

    p_ssd_conv, p_ssd_h, p_lru_conv, p_lru_h = [], [], [], []
    s_ssd_conv, s_lru_conv, s_lru_h = [], [], []
    s_ssd_h = None
    for i in range(depth):
        j = i // 2
        g_pre = _row(norm_mix_pre[i])
        if i % 2 == 0:
            w = ssd_w_in[j]
            w_main = jnp.concatenate([w[:, d_inner:d_inner + cdim], w[:, :d_inner]], axis=1).astype(BF16)
            w_dt = jnp.pad(w[:, d_inner + cdim:], ((0, 0), (0, LANES - n_heads))).astype(BF16)
            p = dict(conv_w=ssd_conv_w[j], conv_b=_row(ssd_conv_b[j]), dt_bias=_pad_lanes(ssd_dt_bias[j]),
                     a_log=_pad_lanes(ssd_a_log[j]), dexp=_row(jnp.repeat(ssd_d[j], SSD_HEAD_DIM)),
                     norm=_row(ssd_norm[j]))
            w_out, b_out = ssd_w_out[j].astype(BF16), jnp.zeros((1, D), F32)

            proj_s, dt_s = _norm_matmul(xs, g_pre, w_main, w_small=w_dt, tm=1024, tn=1024)
            y_samp, s_ssd_h = _ssd_step(proj_s, dt_s, state_ssd_conv[j],
                                        state_ssd_h.reshape(-1, SSD_GROUPS, gw, SSD_STATE), s_ssd_h, p,
                                        nb=nb, layer=j)
            y_meta, h_meta = _ssd_chunk(proj_s, dt_s, jnp.zeros((1, SUBLANES, cdim), F32),
                                        jnp.zeros((1, SSD_GROUPS, gw, SSD_STATE), F32), p,
                                        nseq=1, nchunk=1, row_block0=meta_blk, n_valid=n_meta)
            conv_meta = _carry_rows(proj_s[nb + n_meta - (CONV_WIDTH - 1):nb + n_meta, :cdim])
            s_ssd_conv.append(jnp.concatenate([state_ssd_conv[j][:, 1:], proj_s[:nb, None, :cdim]], axis=1))
            y_s = jnp.concatenate([y_samp, y_meta], axis=0)

            proj_p, dt_p = _norm_matmul(xp, g_pre, w_main, w_small=w_dt, tm=1024, tn=1024)
            y_p, h_p = _ssd_chunk(proj_p, dt_p, conv_meta, h_meta, p,
                                  nseq=B, nchunk=nchunk, row_block0=0, n_valid=T)
            p_ssd_conv.append(proj_p.reshape(B, S, -1)[:, S - (CONV_WIDTH - 1):, :cdim])
            p_ssd_h.append(h_p.reshape(B, n_heads, SSD_HEAD_DIM, SSD_STATE))
        else:
            p = dict(conv_w=lru_conv_w[j], conv_b=_row(lru_conv_b[j]),
                     wax=jnp.concatenate([lru_w_a[j], lru_w_x[j]], axis=-1).astype(BF16),
                     b_a=_row(lru_b_a[j]), b_x=_row(lru_b_x[j]), lam=_row(lru_lambda[j]))
            w_in, b_in = lru_w_in[j].astype(BF16), _row(lru_b_in[j])
            w_out, b_out = lru_w_out[j].astype(BF16), _row(lru_b_out[j])

            proj_s = _norm_matmul(xs, g_pre, w_in, bias=b_in, tm=1024, tn=1024)
            y_samp, h_samp = _lru_step(proj_s, state_lru_conv[j], state_lru_h[j], p, nb=nb)
            y_meta, h_meta = _lru_chunk(proj_s, jnp.zeros((1, SUBLANES, d_rnn), F32),
                                        jnp.zeros((1, 1, d_rnn), F32), p,
                                        nseq=1, nchunk=1, row_block0=meta_blk, n_valid=n_meta)
            conv_meta = _carry_rows(proj_s[nb + n_meta - (CONV_WIDTH - 1):nb + n_meta, d_rnn:])
            s_lru_conv.append(jnp.concatenate([state_lru_conv[j][:, 1:], proj_s[:nb, None, d_rnn:]], axis=1))
            s_lru_h.append(h_samp)
            y_s = jnp.concatenate([y_samp, y_meta], axis=0)

            proj_p = _norm_matmul(xp, g_pre, w_in, bias=b_in, tm=1024, tn=1024)
            y_p, h_p = _lru_chunk(proj_p, conv_meta, h_meta, p,
                                  nseq=B, nchunk=nchunk, row_block0=0, n_valid=T)
            p_lru_conv.append(proj_p.reshape(B, S, -1)[:, S - (CONV_WIDTH - 1):, d_rnn:])
            p_lru_h.append(h_p.reshape(B, d_rnn))

        ffn = (w_out, b_out, _row(norm_mix_post[i]), _row(norm_ffn_pre[i]), ffn_w1[i].astype(BF16),
               ffn_w2[i].astype(BF16), _row(norm_ffn_post[i]))
        xs = _out_ffn(y_s, xs, *ffn, tm=512)
        xp = _out_ffn(y_p, xp, *ffn, tm=512)

    return (xp.reshape(B, S, D), xs[:nb].reshape(nb, 1, D),
            jnp.stack(p_ssd_conv), jnp.stack(p_ssd_h), jnp.stack(p_lru_conv), jnp.stack(p_lru_h),
            jnp.stack(s_ssd_conv), s_ssd_h.reshape(state_ssd_h.shape), jnp.stack(s_lru_conv), jnp.stack(s_lru_h))
```

```python
import functools
import math

import jax
import jax.numpy as jnp
from jax import lax
from jax.experimental import pallas as pl
from jax.experimental.pallas import tpu as pltpu

F32 = jnp.float32
BF16 = jnp.bfloat16

EPS = 1e-6
CONV_WIDTH = 4
SSD_HEAD_DIM = 64
SSD_GROUPS = 8
SSD_STATE = 128
LRU_BLOCKS = 8
LRU_C = 8.0
CHUNK = 128
LANES = 128
SUBLANES = 8
MXU_COLS = 256
VMEM_LIMIT = 56 * 1024 * 1024
PROJ_TM = 1024
PROJ_TN = 1024
FFN_TM = 512


def _cparams(sem):
    return pltpu.CompilerParams(dimension_semantics=sem, vmem_limit_bytes=VMEM_LIMIT)


def _rms(x, g):
    return x * lax.rsqrt(jnp.mean(x * x, axis=-1, keepdims=True) + EPS) * g


def _softplus(x):
    return jnp.maximum(x, 0.0) + jnp.log1p(jnp.exp(-jnp.abs(x)))


def _silu(x):
    return x * jax.nn.sigmoid(x)


def _gelu_tanh(x):
    return 0.5 * x * (1.0 + jnp.tanh(math.sqrt(2.0 / math.pi) * (x + 0.044715 * (x * x * x))))


def _dot(a, b):
    return jnp.dot(a, b, preferred_element_type=F32)


def _dot_nt(a, b):
    return lax.dot_general(a, b, (((1,), (1,)), ((), ())), preferred_element_type=F32)


def _pair_expand(q, j, lane):
    return jnp.where(lane < SSD_HEAD_DIM, q[:, 2 * j:2 * j + 1], q[:, 2 * j + 1:2 * j + 2])


def _conv_rows(x, prev8, cw, cb):
    row8 = lax.broadcasted_iota(jnp.int32, prev8.shape, 0)
    out = cb + cw[CONV_WIDTH - 1:CONV_WIDTH, :] * x
    for k in range(1, CONV_WIDTH):
        r = pltpu.roll(x, k, 0)
        head = jnp.where(row8 < k, pltpu.roll(prev8, k, 0), r[0:SUBLANES, :])
        shifted = jnp.concatenate([head, r[SUBLANES:, :]], axis=0)
        out = out + cw[CONV_WIDTH - 1 - k:CONV_WIDTH - k, :] * shifted
    return out


def _conv_step(x, prevs, cw_ref, cb_ref):
    out = cb_ref[...] + cw_ref[CONV_WIDTH - 1:CONV_WIDTH, :] * x
    for k in range(CONV_WIDTH - 1):
        out = out + cw_ref[k:k + 1, :] * prevs[k]
    return out


def _ssd_gate_norm(y, zs, ng_ref):
    y = y * zs
    gw = y.shape[1] // SSD_GROUPS
    parts = []
    for g in range(SSD_GROUPS):
        seg = y[:, g * gw:(g + 1) * gw]
        parts.append(seg * lax.rsqrt(jnp.mean(seg * seg, axis=-1, keepdims=True) + EPS))
    return jnp.concatenate(parts, axis=1) * ng_ref[...]


def _lru_gates(xr, wax_ref, ba_ref, bx_ref, lam_ref):
    bw = xr.shape[1] // LRU_BLOCKS
    ra, ix = [], []
    for k in range(LRU_BLOCKS):
        g = _dot(xr[:, k * bw:(k + 1) * bw].astype(BF16), wax_ref[k])
        ra.append(g[:, :bw])
        ix.append(g[:, bw:])
    r = jax.nn.sigmoid(jnp.concatenate(ra, axis=1) + ba_ref[...])
    i = jax.nn.sigmoid(jnp.concatenate(ix, axis=1) + bx_ref[...])
    log_a = (-LRU_C) * r * _softplus(-lam_ref[...])
    a = jnp.exp(log_a)
    v = -jnp.tanh(log_a) * (a * a + 1.0)
    mult = jnp.where(v > 0.0, v * lax.rsqrt(v), 0.0)
    return a, mult * i * xr


def _norm_kernel(x_ref, g_ref, o_ref):
    o_ref[...] = _rms(x_ref[...], g_ref[...]).astype(o_ref.dtype)


def _norm_bf16(x, g, *, tm):
    M, D = x.shape
    tm = min(tm, M)
    assert M % tm == 0
    return pl.pallas_call(
        _norm_kernel,
        grid=(M // tm,),
        in_specs=[pl.BlockSpec((tm, D), lambda i: (i, 0)), pl.BlockSpec((1, D), lambda i: (0, 0))],
        out_specs=pl.BlockSpec((tm, D), lambda i: (i, 0)),
        out_shape=jax.ShapeDtypeStruct((M, D), BF16),
        compiler_params=_cparams(("parallel",)),
        name="norm",
    )(x, g)


_ACT = {"silu": _silu, "gelu": _gelu_tanh}


def _proj_kernel(*refs, mode, has_bias, n_raw, tiles_per_seq, tail_lo):
    has_conv = mode in ("conv", "conv_silu")
    it = iter(refs)
    xn_ref, w_ref = next(it), next(it)
    b_ref = next(it) if has_bias else None
    cw_ref, cb_ref, conv0_ref = (next(it), next(it), next(it)) if has_conv else (None, None, None)
    o_ref = next(it)
    tail_ref = next(it) if has_conv else None
    wb_s = next(it)
    carry_s = next(it) if has_conv else None
    i = pl.program_id(1)

    @pl.when(i == 0)
    def _():
        wb_s[...] = w_ref[...].astype(BF16)

    tm, tn = o_ref.shape
    for n in range(tn // MXU_COLS):
        sl = pl.ds(n * MXU_COLS, MXU_COLS)
        acc = _dot(xn_ref[...], wb_s[:, sl])
        if has_bias:
            acc = acc + b_ref[:, sl]
        if not has_conv:
            o_ref[:, sl] = _ACT[mode](acc)
            continue
        tail_ref[0, :, sl] = acc[tail_lo:tail_lo + SUBLANES, :]

        def conv_tile(acc=acc, sl=sl):
            start = lax.rem(i - n_raw, tiles_per_seq) == 0
            prev8 = jnp.where(start, conv0_ref[0, :, sl], carry_s[:, sl])
            out = _conv_rows(acc, prev8, cw_ref[:, sl], cb_ref[:, sl])
            carry_s[:, sl] = acc[tm - SUBLANES:tm, :]
            o_ref[:, sl] = _silu(out) if mode == "conv_silu" else out

        if n_raw == 0:
            conv_tile()
        else:
            @pl.when(i < n_raw)
            def _(acc=acc, sl=sl):
                o_ref[:, sl] = acc

            pl.when(i >= n_raw)(conv_tile)


def _proj(xn, w, layer, col0, n_out, mode, *, tm, tn, bias=None, conv=None, n_raw=0, tiles_per_seq=1,
          tail_lo=0):
    M, D = xn.shape
    tm = min(tm, M)
    assert M % tm == 0 and n_out % tn == 0 and col0 % tn == 0 and tn % MXU_COLS == 0
    n_i, n_j, jb = M // tm, n_out // tn, col0 // tn
    in_specs = [pl.BlockSpec((tm, D), lambda j, i: (i, 0)),
                pl.BlockSpec((None, D, tn), lambda j, i: (layer, 0, jb + j))]
    args = [xn, w]
    if bias is not None:
        in_specs.append(pl.BlockSpec((1, tn), lambda j, i: (0, jb + j)))
        args.append(bias)
    out_shape = [jax.ShapeDtypeStruct((M, n_out), F32)]
    out_specs = [pl.BlockSpec((tm, tn), lambda j, i: (i, j))]
    scratch = [pltpu.VMEM((D, tn), BF16)]
    if conv is not None:
        assert mode in ("conv", "conv_silu")
        cw, cb, conv0 = conv
        in_specs += [pl.BlockSpec((CONV_WIDTH, tn), lambda j, i: (0, j)),
                     pl.BlockSpec((1, tn), lambda j, i: (0, j)),
                     pl.BlockSpec((1, SUBLANES, tn), lambda j, i: (0, 0, j))]
        args += [cw, cb, conv0]
        out_shape.append(jax.ShapeDtypeStruct((n_i, SUBLANES, n_out), F32))
        out_specs.append(pl.BlockSpec((1, SUBLANES, tn), lambda j, i: (i, 0, j)))
        scratch.append(pltpu.VMEM((SUBLANES, tn), F32))
    res = pl.pallas_call(
        functools.partial(_proj_kernel, mode=mode, has_bias=bias is not None, n_raw=n_raw,
                          tiles_per_seq=tiles_per_seq, tail_lo=tail_lo),
        grid=(n_j, n_i),
        in_specs=in_specs, out_specs=out_specs, out_shape=out_shape, scratch_shapes=scratch,
        compiler_params=_cparams(("arbitrary", "arbitrary")),
        name="proj_" + mode,
    )(*args)
    return res if conv is not None else res[0]


def _dt_kernel(xn_ref, w_ref, dtb_ref, alog_ref, dtv_ref, cs_ref, csT_ref, dtT_ref, *, n_raw, n_valid):
    i = pl.program_id(0)
    tm = xn_ref.shape[0]
    T = CHUNK
    dtv = _softplus(_dot(xn_ref[...], w_ref[...].astype(BF16)) + dtb_ref[...])
    if n_valid < tm:
        keep = jnp.logical_or(lax.broadcasted_iota(jnp.int32, dtv.shape, 0) < n_valid, i < n_raw)
        dtv = jnp.where(keep, dtv, 0.0)
    dtv_ref[...] = dtv
    a = dtv * (-jnp.exp(alog_ref[...]))
    tri = (lax.broadcasted_iota(jnp.int32, (T, T), 0) >= lax.broadcasted_iota(jnp.int32, (T, T), 1)).astype(F32)
    for r in range(tm // T):
        cs = jnp.dot(tri, a[r * T:(r + 1) * T, :], precision=lax.Precision.HIGHEST, preferred_element_type=F32)
        cs_ref[r * T:(r + 1) * T, :] = cs
        csT_ref[r] = cs.T
        dtT_ref[r] = dtv[r * T:(r + 1) * T, :].T


def _dt_proj(xn, w_dt, dtb, alog, *, tm, n_raw, n_valid):
    M, D = xn.shape
    T = CHUNK
    tm = min(tm, M)
    assert M % tm == 0 and tm % T == 0 and T == LANES and (n_valid >= tm or tm == T)
    const = lambda i: (0, 0)
    rows = pl.BlockSpec((tm, LANES), lambda i: (i, 0))
    rowsT = pl.BlockSpec((tm // T, LANES, T), lambda i: (i, 0, 0))
    return pl.pallas_call(
        functools.partial(_dt_kernel, n_raw=n_raw, n_valid=n_valid),
        grid=(M // tm,),
        in_specs=[pl.BlockSpec((tm, D), lambda i: (i, 0)),
                  pl.BlockSpec((D, LANES), const),
                  pl.BlockSpec((1, LANES), const),
                  pl.BlockSpec((1, LANES), const)],
        out_specs=[rows, rows, rowsT, rowsT],
        out_shape=[jax.ShapeDtypeStruct((M, LANES), F32), jax.ShapeDtypeStruct((M, LANES), F32),
                   jax.ShapeDtypeStruct((M // T, LANES, T), F32), jax.ShapeDtypeStruct((M // T, LANES, T), F32)],
        compiler_params=_cparams(("parallel",)),
        name="dt_proj",
    )(xn, w_dt, dtb, alog)


def _out_ffn_kernel(*refs, fc, has_next):
    y_ref, x_ref, wo_ref, bo_ref, gpost_ref, gpre_ref, w1_ref, w2_ref, gfpost_ref = refs[:9]
    gnext_ref = refs[9] if has_next else None
    o_ref = refs[9 + has_next]
    xn_ref = refs[10 + has_next] if has_next else None
    m = _dot(y_ref[...], wo_ref[...]) + bo_ref[...]
    x1 = x_ref[...] + _rms(m, gpost_ref[...])
    hn = _rms(x1, gpre_ref[...]).astype(BF16)
    dff = w1_ref.shape[1]
    acc = None
    for c in range(dff // fc):
        h = jnp.maximum(_dot(hn, w1_ref[:, c * fc:(c + 1) * fc]), 0.0)
        part = _dot((h * h).astype(BF16), w2_ref[c * fc:(c + 1) * fc, :])
        acc = part if acc is None else acc + part
    x2 = x1 + _rms(acc, gfpost_ref[...])
    o_ref[...] = x2
    if has_next:
        xn_ref[...] = _rms(x2, gnext_ref[...]).astype(xn_ref.dtype)


def _out_ffn(y, x, wo, bo, gpost, gpre, w1, w2, gfpost, gnext, *, tm, fc=1024):
    M, D = x.shape
    K = y.shape[1]
    dff = w1.shape[1]
    tm = min(tm, M)
    assert M % tm == 0 and dff % fc == 0
    const = lambda i: (0, 0)
    once = dict(pipeline_mode=pl.Buffered(1))
    rows = lambda w: pl.BlockSpec((tm, w), lambda i: (i, 0))
    has_next = gnext is not None
    in_specs = [rows(K), rows(D),
                pl.BlockSpec((K, D), const, **once),
                pl.BlockSpec((1, D), const),
                pl.BlockSpec((1, D), const),
                pl.BlockSpec((1, D), const),
                pl.BlockSpec((D, dff), const, **once),
                pl.BlockSpec((dff, D), const, **once),
                pl.BlockSpec((1, D), const)]
    args = [y, x, wo, bo, gpost, gpre, w1, w2, gfpost]
    out_specs = [rows(D)]
    out_shape = [jax.ShapeDtypeStruct((M, D), F32)]
    if has_next:
        in_specs.append(pl.BlockSpec((1, D), const))
        args.append(gnext)
        out_specs.append(rows(D))
        out_shape.append(jax.ShapeDtypeStruct((M, D), BF16))
    res = pl.pallas_call(
        functools.partial(_out_ffn_kernel, fc=fc, has_next=has_next),
        grid=(M // tm,),
        in_specs=in_specs, out_specs=out_specs, out_shape=out_shape,
        compiler_params=_cparams(("parallel",)),
        name="out_ffn",
    )(*args)
    return (res[0], res[1]) if has_next else (res[0], None)


def _ssd_chunk_kernel(xbc_ref, zs_ref, cs_ref, csT_ref, dtT_ref, h0_ref, dexp_ref, ng_ref, y_ref, hout_ref,
                      hT_s, *, T):
    c = pl.program_id(1)
    d_inner = zs_ref.shape[1]
    gs = SSD_GROUPS * SSD_STATE
    gw = d_inner // SSD_GROUPS

    @pl.when(c == 0)
    def _():
        for g in range(SSD_GROUPS):
            hT_s[g] = h0_ref[0, g].T

    row = lax.broadcasted_iota(jnp.int32, (T, T), 0)
    col = lax.broadcasted_iota(jnp.int32, (T, T), 1)
    causal = row >= col
    lane = lax.broadcasted_iota(jnp.int32, (T, LANES), 1)
    lo_half = lane < SSD_HEAD_DIM

    cs = cs_ref[...]
    csT = csT_ref[0]
    dtT = dtT_ref[0]
    wdT = dtT * jnp.exp(csT[:, T - 1:T] - csT)
    dec_tot = jnp.exp(cs[T - 1:T, :])

    ys = []
    for g in range(SSD_GROUPS):
        Bg = xbc_ref[:, d_inner + g * SSD_STATE:d_inner + (g + 1) * SSD_STATE]
        Cg = xbc_ref[:, d_inner + gs + g * SSD_STATE:d_inner + gs + (g + 1) * SSD_STATE].astype(BF16)
        BgT = Bg.T
        cb = _dot(Cg, BgT.astype(BF16))
        hTg = hT_s[g]
        yoff = _dot(Cg, hTg.astype(BF16))
        for jj in range(gw // LANES):
            j = g * (gw // LANES) + jj
            xpair = xbc_ref[:, j * LANES:(j + 1) * LANES]
            s_parts, bw_parts, ecs_parts = [], [], []
            for hh in (2 * j, 2 * j + 1):
                csl = jnp.broadcast_to(cs[:, hh:hh + 1], (T, T))
                L = jnp.where(causal, jnp.exp(csl - csT[hh:hh + 1, :]), 0.0)
                s_parts.append((cb * L * dtT[hh:hh + 1, :]).astype(BF16))
                bw_parts.append((BgT * wdT[hh:hh + 1, :]).astype(BF16))
                ecs_parts.append(jnp.exp(csl))
            x2 = jnp.concatenate([jnp.where(lo_half, xpair, 0.0).astype(BF16),
                                  jnp.where(lo_half, 0.0, xpair).astype(BF16)], axis=0)
            ydiag = _dot(jnp.concatenate(s_parts, axis=1), x2)
            upd = _dot(jnp.concatenate(bw_parts, axis=1), x2)
            ecs = jnp.where(lo_half, ecs_parts[0], ecs_parts[1])
            ys.append(ydiag + yoff[:, jj * LANES:(jj + 1) * LANES] * ecs
                      + xpair * dexp_ref[:, j * LANES:(j + 1) * LANES])
            dect = _pair_expand(dec_tot, j, lane[0:1, :])
            hT_s[g, :, jj * LANES:(jj + 1) * LANES] = hTg[:, jj * LANES:(jj + 1) * LANES] * dect + upd

    y = jnp.concatenate(ys, axis=1)
    y_ref[...] = _ssd_gate_norm(y, zs_ref[...], ng_ref).astype(y_ref.dtype)

    @pl.when(c == pl.num_programs(1) - 1)
    def _():
        for g in range(SSD_GROUPS):
            hout_ref[0, g] = hT_s[g].T


def _ssd_chunk(xbc, zs, dtq, h0, p, *, nseq, nchunk, row_block0):
    T = CHUNK
    assert T == LANES
    d_inner = zs.shape[1]
    cdim = xbc.shape[1]
    gw = d_inner // SSD_GROUPS
    _, cs, csT, dtT = dtq
    rows = lambda b, c: (row_block0 + b * nchunk + c, 0)
    rows3 = lambda b, c: (row_block0 + b * nchunk + c, 0, 0)
    shared4 = (lambda b, c: (0, 0, 0, 0)) if h0.shape[0] == 1 else (lambda b, c: (b, 0, 0, 0))
    const = lambda b, c: (0, 0)
    return pl.pallas_call(
        functools.partial(_ssd_chunk_kernel, T=T),
        grid=(nseq, nchunk),
        in_specs=[pl.BlockSpec((T, cdim), rows),
                  pl.BlockSpec((T, d_inner), rows),
                  pl.BlockSpec((T, LANES), rows),
                  pl.BlockSpec((1, LANES, T), rows3),
                  pl.BlockSpec((1, LANES, T), rows3),
                  pl.BlockSpec((1, SSD_GROUPS, gw, SSD_STATE), shared4),
                  pl.BlockSpec((1, d_inner), const),
                  pl.BlockSpec((1, d_inner), const)],
        out_specs=[pl.BlockSpec((T, d_inner), lambda b, c: (b * nchunk + c, 0)),
                   pl.BlockSpec((1, SSD_GROUPS, gw, SSD_STATE), lambda b, c: (b, 0, 0, 0))],
        out_shape=[jax.ShapeDtypeStruct((nseq * nchunk * T, d_inner), BF16),
                   jax.ShapeDtypeStruct((nseq, SSD_GROUPS, gw, SSD_STATE), F32)],
        scratch_shapes=[pltpu.VMEM((SSD_GROUPS, SSD_STATE, gw), F32)],
        compiler_params=_cparams(("parallel", "arbitrary")),
        name="ssd_chunk",
    )(xbc, zs, cs, csT, dtT, h0, p["dexp"], p["norm"])


def _ssd_step_kernel(*refs, seq_per_step, has_prev):
    (xbc_ref, zs_ref, dt_ref, c0_ref, c1_ref, c2_ref, h_ref, cw_ref, cb_ref, alog_ref, dexp_ref,
     ng_ref) = refs[:12]
    y_ref, hout_ref, xdtT_s, decT_s, b_s, c_s, xs_s, yrow_s = refs[12 + has_prev:]
    s = pl.program_id(0)
    nb, d_inner = xs_s.shape
    gs = SSD_GROUPS * SSD_STATE
    gw = d_inner // SSD_GROUPS

    @pl.when(s == 0)
    def _():
        xbc = _silu(_conv_step(xbc_ref[...], (c0_ref[...], c1_ref[...], c2_ref[...]), cw_ref, cb_ref))
        xs = xbc[:, :d_inner]
        xs_s[...] = xs
        b_s[...] = xbc[:, d_inner:d_inner + gs]
        c_s[...] = xbc[:, d_inner + gs:]
        dtv = dt_ref[...]
        decT_s[...] = jnp.exp(dtv * (-jnp.exp(alog_ref[...]))).T
        dtT = dtv.T
        for j in range(d_inner // LANES):
            xT = xs[:, j * LANES:(j + 1) * LANES].T
            for k in range(2):
                hh = 2 * j + k
                xdtT_s[hh * SSD_HEAD_DIM:(hh + 1) * SSD_HEAD_DIM, :] = (
                    xT[k * SSD_HEAD_DIM:(k + 1) * SSD_HEAD_DIM, :] * dtT[hh:hh + 1, :])

    for q in range(seq_per_step):
        b = s * seq_per_step + q
        sel = lax.broadcasted_iota(jnp.int32, (1, nb), 1) == b
        dec_col = jnp.sum(jnp.where(sel, decT_s[...], 0.0), axis=1, keepdims=True)
        brow = b_s[pl.ds(b, 1), :]
        crow = c_s[pl.ds(b, 1), :]
        for g in range(SSD_GROUPS):
            xcol = jnp.sum(jnp.where(sel, xdtT_s[g * gw:(g + 1) * gw, :], 0.0), axis=1, keepdims=True)
            dcol = jnp.concatenate(
                [jnp.broadcast_to(dec_col[hh:hh + 1, :], (SSD_HEAD_DIM, 1))
                 for hh in range(g * (gw // SSD_HEAD_DIM), (g + 1) * (gw // SSD_HEAD_DIM))], axis=0)
            hn = h_ref[q, g] * dcol + xcol * brow[:, g * SSD_STATE:(g + 1) * SSD_STATE]
            hout_ref[q, g] = hn
            c8 = jnp.broadcast_to(crow[:, g * SSD_STATE:(g + 1) * SSD_STATE], (SUBLANES, SSD_STATE)).astype(BF16)
            yg = _dot_nt(c8, hn.astype(BF16))
            yrow_s[pl.ds(b, 1), g * gw:(g + 1) * gw] = yg[0:1, :]

    @pl.when(s == pl.num_programs(0) - 1)
    def _():
        y = yrow_s[...] + xs_s[...] * dexp_ref[...]
        y_ref[...] = _ssd_gate_norm(y, zs_ref[...], ng_ref).astype(y_ref.dtype)


def _ssd_step(xbc, zs, dt, conv_state, h_all, h_out_prev, p, *, nb, layer, seq_per_step=4):
    d_inner = p["dexp"].shape[1]
    cdim = p["conv_w"].shape[1]
    gw = d_inner // SSD_GROUPS
    gs = SSD_GROUPS * SSD_STATE
    assert nb % seq_per_step == 0
    blk0 = layer * (nb // seq_per_step)
    const = lambda s: (0, 0)
    prev = lambda k: pl.BlockSpec((nb, cdim), lambda s: (0, k))
    conv_state = conv_state.reshape(nb, -1)
    hspec = pl.BlockSpec((seq_per_step, SSD_GROUPS, gw, SSD_STATE), lambda s: (blk0 + s, 0, 0, 0))
    has_prev = h_out_prev is not None
    in_specs = [pl.BlockSpec((nb, cdim), const),
                pl.BlockSpec((nb, d_inner), const),
                pl.BlockSpec((nb, LANES), const),
                prev(0), prev(1), prev(2),
                hspec,
                pl.BlockSpec((CONV_WIDTH, cdim), const),
                pl.BlockSpec((1, cdim), const),
                pl.BlockSpec((1, LANES), const),
                pl.BlockSpec((1, d_inner), const),
                pl.BlockSpec((1, d_inner), const)]
    args = [xbc, zs, dt, conv_state, conv_state, conv_state, h_all, p["conv_w"], p["conv_b"], p["a_log"],
            p["dexp"], p["norm"]]
    if has_prev:
        in_specs.append(pl.BlockSpec(memory_space=pl.ANY))
        args.append(h_out_prev)
    return pl.pallas_call(
        functools.partial(_ssd_step_kernel, seq_per_step=seq_per_step, has_prev=has_prev),
        grid=(nb // seq_per_step,),
        in_specs=in_specs,
        out_specs=[pl.BlockSpec((nb, d_inner), const), hspec],
        out_shape=[jax.ShapeDtypeStruct((nb, d_inner), BF16),
                   jax.ShapeDtypeStruct(h_all.shape, F32)],
        scratch_shapes=[pltpu.VMEM((d_inner, nb), F32),
                        pltpu.VMEM((LANES, nb), F32),
                        pltpu.VMEM((nb, gs), F32),
                        pltpu.VMEM((nb, gs), F32),
                        pltpu.VMEM((nb, d_inner), F32),
                        pltpu.VMEM((nb, d_inner), F32)],
        input_output_aliases={len(args) - 1: 1} if has_prev else {},
        compiler_params=_cparams(("arbitrary",)),
        name="ssd_step",
    )(*args)


def _shift_rows(x, s, fill, row):
    if s % SUBLANES == 0:
        return jnp.concatenate([jnp.full((s, x.shape[1]), fill, x.dtype), x[:x.shape[0] - s, :]], axis=0)
    return jnp.where(row >= s, pltpu.roll(x, s, 0), fill)


def _lru_chunk_kernel(gate_ref, xr_ref, h0_ref, wax_ref, ba_ref, bx_ref, lam_ref, y_ref, hout_ref, hc_s,
                      *, T, n_valid):
    c = pl.program_id(1)

    @pl.when(c == 0)
    def _():
        hc_s[...] = h0_ref[0]

    a, bt = _lru_gates(xr_ref[...], wax_ref, ba_ref, bx_ref, lam_ref)
    row = lax.broadcasted_iota(jnp.int32, a.shape, 0)
    if n_valid < T:
        a = jnp.where(row < n_valid, a, 1.0)
        bt = jnp.where(row < n_valid, bt, 0.0)
    s = 1
    while s < T:
        bt = a * _shift_rows(bt, s, 0.0, row) + bt
        a = a * _shift_rows(a, s, 1.0, row)
        s *= 2
    h = a * hc_s[...] + bt
    hc_s[...] = h[T - 1:T, :]
    y_ref[...] = (h * gate_ref[...]).astype(y_ref.dtype)

    @pl.when(c == pl.num_programs(1) - 1)
    def _():
        hout_ref[0] = h[T - 1:T, :]


def _lru_chunk(gate, xr, h0, p, *, nseq, nchunk, row_block0, n_valid):
    T = CHUNK
    d = xr.shape[1]
    bw = d // LRU_BLOCKS
    rows = lambda b, c: (row_block0 + b * nchunk + c, 0)
    shared_h = (lambda b, c: (0, 0, 0)) if h0.shape[0] == 1 else (lambda b, c: (b, 0, 0))
    const = lambda b, c: (0, 0)
    return pl.pallas_call(
        functools.partial(_lru_chunk_kernel, T=T, n_valid=n_valid),
        grid=(nseq, nchunk),
        in_specs=[pl.BlockSpec((T, d), rows),
                  pl.BlockSpec((T, d), rows),
                  pl.BlockSpec((1, 1, d), shared_h),
                  pl.BlockSpec((LRU_BLOCKS, bw, 2 * bw), lambda b, c: (0, 0, 0)),
                  pl.BlockSpec((1, d), const),
                  pl.BlockSpec((1, d), const),
                  pl.BlockSpec((1, d), const)],
        out_specs=[pl.BlockSpec((T, d), lambda b, c: (b * nchunk + c, 0)),
                   pl.BlockSpec((1, 1, d), lambda b, c: (b, 0, 0))],
        out_shape=[jax.ShapeDtypeStruct((nseq * nchunk * T, d), BF16),
                   jax.ShapeDtypeStruct((nseq, 1, d), F32)],
        scratch_shapes=[pltpu.VMEM((1, d), F32)],
        compiler_params=_cparams(("parallel", "arbitrary")),
        name="lru_chunk",
    )(gate, xr, h0, p["wax"], p["b_a"], p["b_x"], p["lam"])


def _lru_step_kernel(gate_ref, xr_ref, c0_ref, c1_ref, c2_ref, h0_ref, cw_ref, cb_ref, wax_ref, ba_ref, bx_ref,
                     lam_ref, y_ref, hout_ref):
    xr = _conv_step(xr_ref[...], (c0_ref[...], c1_ref[...], c2_ref[...]), cw_ref, cb_ref)
    a, bt = _lru_gates(xr, wax_ref, ba_ref, bx_ref, lam_ref)
    h = a * h0_ref[...] + bt
    hout_ref[...] = h
    y_ref[...] = (h * gate_ref[...]).astype(y_ref.dtype)


def _lru_step(gate, xr, conv_state, h0, p, *, nb):
    d = xr.shape[1]
    bw = d // LRU_BLOCKS
    const = lambda i: (0, 0)
    prev = lambda k: pl.BlockSpec((nb, d), lambda i: (0, k))
    conv_state = conv_state.reshape(nb, -1)
    return pl.pallas_call(
        _lru_step_kernel,
        grid=(1,),
        in_specs=[pl.BlockSpec((nb, d), const),
                  pl.BlockSpec((nb, d), const),
                  prev(0), prev(1), prev(2),
                  pl.BlockSpec((nb, d), const),
                  pl.BlockSpec((CONV_WIDTH, d), const),
                  pl.BlockSpec((1, d), const),
                  pl.BlockSpec((LRU_BLOCKS, bw, 2 * bw), lambda i: (0, 0, 0)),
                  pl.BlockSpec((1, d), const),
                  pl.BlockSpec((1, d), const),
                  pl.BlockSpec((1, d), const)],
        out_specs=[pl.BlockSpec((nb, d), const), pl.BlockSpec((nb, d), const)],
        out_shape=[jax.ShapeDtypeStruct((nb, d), BF16), jax.ShapeDtypeStruct((nb, d), F32)],
        compiler_params=_cparams(("arbitrary",)),
        name="lru_step",
    )(gate, xr, conv_state, conv_state, conv_state, h0, p["conv_w"], p["conv_b"], p["wax"], p["b_a"],
      p["b_x"], p["lam"])


def _row(v):
    return v.reshape(1, -1).astype(F32)


def _pad_lanes(v):
    return jnp.pad(v.astype(F32), (0, LANES - v.shape[0])).reshape(1, LANES)


def kernel(x_prompt, x_sample, state_ssd_conv, state_ssd_h, state_lru_conv, state_lru_h, meta_tokens, norm_mix_pre, norm_mix_post, norm_ffn_pre, norm_ffn_post, ssd_w_in, ssd_conv_w, ssd_conv_b, ssd_dt_bias, ssd_a_log, ssd_d, ssd_norm, ssd_w_out, lru_w_in, lru_b_in, lru_conv_w, lru_conv_b, lru_w_a, lru_b_a, lru_w_x, lru_b_x, lru_lambda, lru_w_out, lru_b_out, ffn_w1, ffn_w2):
    B, S, D = x_prompt.shape
    nb = x_sample.shape[0]
    n_meta = meta_tokens.shape[0]
    depth = norm_mix_pre.shape[0]
    T = CHUNK
    n_heads = ssd_dt_bias.shape[1]
    d_inner = n_heads * SSD_HEAD_DIM
    cdim = ssd_conv_w.shape[2]
    d_rnn = lru_conv_w.shape[2]
    gw = d_inner // SSD_GROUPS
    tm_p = min(PROJ_TM, S)
    assert x_sample.shape[1] == 1 and nb == T and S % tm_p == 0 and tm_p % T == 0
    assert SUBLANES <= n_meta <= T and n_meta % SUBLANES == 0
    nchunk = S // T
    meta_blk = nb // T

    xs = jnp.concatenate([x_sample[:, 0, :], meta_tokens, jnp.zeros((T - n_meta, D), F32)], axis=0)
    xp = x_prompt.reshape(B * S, D)
    xn_s = _norm_bf16(xs, _row(norm_mix_pre[0]), tm=PROJ_TM)
    xn_p = _norm_bf16(xp, _row(norm_mix_pre[0]), tm=PROJ_TM)

    small = dict(tm=T, n_raw=meta_blk, n_valid=n_meta,
                 conv=dict(tm=T, n_raw=meta_blk, tiles_per_seq=1, tail_lo=n_meta - SUBLANES))
    prompt = dict(tm=tm_p, n_raw=0, n_valid=tm_p,
                  conv=dict(tm=tm_p, n_raw=0, tiles_per_seq=S // tm_p, tail_lo=tm_p - SUBLANES))
    last3 = lambda tail: tail.reshape(B, S // tm_p, SUBLANES, -1)[:, -1, SUBLANES - (CONV_WIDTH - 1):, :]

    p_ssd_conv, p_ssd_h, p_lru_conv, p_lru_h = [], [], [], []
    s_ssd_conv, s_lru_conv, s_lru_h = [], [], []
    s_ssd_h = None
    for i in range(depth):
        j = i // 2
        if i % 2 == 0:
            w_dt = jnp.pad(ssd_w_in[j][:, d_inner + cdim:], ((0, 0), (0, LANES - n_heads)))
            p = dict(conv_w=ssd_conv_w[j], conv_b=_row(ssd_conv_b[j]), dt_bias=_pad_lanes(ssd_dt_bias[j]),
                     a_log=_pad_lanes(ssd_a_log[j]), dexp=_row(jnp.repeat(ssd_d[j], SSD_HEAD_DIM)),
                     norm=_row(ssd_norm[j]))
            w_out, b_out = ssd_w_out[j].astype(BF16), jnp.zeros((1, D), F32)

            def run_proj(xn, conv0, v):
                xbc, tail = _proj(xn, ssd_w_in, j, d_inner, cdim, "conv_silu", tn=PROJ_TN,
                                  conv=(p["conv_w"], p["conv_b"], conv0), **v["conv"])
                zs = _proj(xn, ssd_w_in, j, 0, d_inner, "silu", tm=v["tm"], tn=PROJ_TN)
                dtq = _dt_proj(xn, w_dt, p["dt_bias"], p["a_log"], tm=v["tm"], n_raw=v["n_raw"],
                               n_valid=v["n_valid"])
                return xbc, tail, zs, dtq

            xbc_s, tail_s, zs_s, dtq_s = run_proj(xn_s, jnp.zeros((1, SUBLANES, cdim), F32), small)
            y_samp, s_ssd_h = _ssd_step(xbc_s, zs_s, dtq_s[0], state_ssd_conv[j],
                                        state_ssd_h.reshape(-1, SSD_GROUPS, gw, SSD_STATE), s_ssd_h, p,
                                        nb=nb, layer=j)
            y_meta, h_meta = _ssd_chunk(xbc_s, zs_s, dtq_s, jnp.zeros((1, SSD_GROUPS, gw, SSD_STATE), F32), p,
                                        nseq=1, nchunk=1, row_block0=meta_blk)
            conv_meta = tail_s[meta_blk:meta_blk + 1]
            s_ssd_conv.append(jnp.concatenate([state_ssd_conv[j][:, 1:], xbc_s[:nb, None, :]], axis=1))
            y_s = jnp.concatenate([y_samp, y_meta], axis=0)

            xbc_p, tail_p, zs_p, dtq_p = run_proj(xn_p, conv_meta, prompt)
            y_p, h_p = _ssd_chunk(xbc_p, zs_p, dtq_p, h_meta, p, nseq=B, nchunk=nchunk, row_block0=0)
            p_ssd_conv.append(last3(tail_p))
            p_ssd_h.append(h_p.reshape(B, n_heads, SSD_HEAD_DIM, SSD_STATE))
        else:
            p = dict(conv_w=lru_conv_w[j], conv_b=_row(lru_conv_b[j]),
                     wax=jnp.concatenate([lru_w_a[j], lru_w_x[j]], axis=-1).astype(BF16),
                     b_a=_row(lru_b_a[j]), b_x=_row(lru_b_x[j]), lam=_row(lru_lambda[j]))
            w_out, b_out = lru_w_out[j].astype(BF16), _row(lru_b_out[j])
            b_in = _row(lru_b_in[j])

            def run_proj(xn, conv0, v):
                gate = _proj(xn, lru_w_in, j, 0, d_rnn, "gelu", tm=v["tm"], tn=PROJ_TN, bias=b_in)
                xr, tail = _proj(xn, lru_w_in, j, d_rnn, d_rnn, "conv", tn=PROJ_TN, bias=b_in,
                                 conv=(p["conv_w"], p["conv_b"], conv0), **v["conv"])
                return gate, xr, tail

            gate_s, xr_s, tail_s = run_proj(xn_s, jnp.zeros((1, SUBLANES, d_rnn), F32), small)
            y_samp, h_samp = _lru_step(gate_s, xr_s, state_lru_conv[j], state_lru_h[j], p, nb=nb)
            y_meta, h_meta = _lru_chunk(gate_s, xr_s, jnp.zeros((1, 1, d_rnn), F32), p,
                                        nseq=1, nchunk=1, row_block0=meta_blk, n_valid=n_meta)
            conv_meta = tail_s[meta_blk:meta_blk + 1]
            s_lru_conv.append(jnp.concatenate([state_lru_conv[j][:, 1:], xr_s[:nb, None, :]], axis=1))
            s_lru_h.append(h_samp)
            y_s = jnp.concatenate([y_samp, y_meta], axis=0)

            gate_p, xr_p, tail_p = run_proj(xn_p, conv_meta, prompt)
            y_p, h_p = _lru_chunk(gate_p, xr_p, h_meta, p, nseq=B, nchunk=nchunk, row_block0=0, n_valid=T)
            p_lru_conv.append(last3(tail_p))
            p_lru_h.append(h_p.reshape(B, d_rnn))

        g_next = _row(norm_mix_pre[i + 1]) if i + 1 < depth else None
        ffn = (w_out, b_out, _row(norm_mix_post[i]), _row(norm_ffn_pre[i]), ffn_w1[i].astype(BF16),
               ffn_w2[i].astype(BF16), _row(norm_ffn_post[i]), g_next)
        xs, xn_s = _out_ffn(y_s, xs, *ffn, tm=FFN_TM)
        xp, xn_p = _out_ffn(y_p, xp, *ffn, tm=FFN_TM)

    return (xp.reshape(B, S, D), xs[:nb].reshape(nb, 1, D),
            jnp.stack(p_ssd_conv), jnp.stack(p_ssd_h), jnp.stack(p_lru_conv), jnp.stack(p_lru_h),
            jnp.stack(s_ssd_conv), s_ssd_h.reshape(state_ssd_h.shape), jnp.stack(s_lru_conv), jnp.stack(s_lru_h))
```

```python
import functools
import math

import jax
import jax.numpy as jnp
from jax import lax
from jax.experimental import pallas as pl
from jax.experimental.pallas import tpu as pltpu

F32 = jnp.float32
BF16 = jnp.bfloat16

EPS = 1e-6
CONV_WIDTH = 4
SSD_HEAD_DIM = 64
SSD_GROUPS = 8
SSD_STATE = 128
LRU_BLOCKS = 8
LRU_C = 8.0
CHUNK = 128
LANES = 128
SUBLANES = 8
MXU_COLS = 256
VMEM_LIMIT = 56 * 1024 * 1024
PROJ_TM = 1024
PROJ_TN = 1024
FFN_TM = 512


def _cparams(sem):
    return pltpu.CompilerParams(dimension_semantics=sem, vmem_limit_bytes=VMEM_LIMIT)


def _rms(x, g):
    return x * lax.rsqrt(jnp.mean(x * x, axis=-1, keepdims=True) + EPS) * g


def _softplus(x):
    return jnp.maximum(x, 0.0) + jnp.log1p(jnp.exp(-jnp.abs(x)))


def _silu(x):
    return x * jax.nn.sigmoid(x)


def _gelu_tanh(x):
    return 0.5 * x * (1.0 + jnp.tanh(math.sqrt(2.0 / math.pi) * (x + 0.044715 * (x * x * x))))


def _dot(a, b):
    return jnp.dot(a, b, preferred_element_type=F32)


def _dot_nt(a, b):
    return lax.dot_general(a, b, (((1,), (1,)), ((), ())), preferred_element_type=F32)


def _pair_expand(q, j, lane):
    return jnp.where(lane < SSD_HEAD_DIM, q[:, 2 * j:2 * j + 1], q[:, 2 * j + 1:2 * j + 2])


def _conv_rows(x, prev, cw, cb, rpt):
    tm = x.shape[0]
    out = cb + cw[CONV_WIDTH - 1:CONV_WIDTH, :] * x
    for k in range(1, CONV_WIDTH):
        if rpt == 1:
            row8 = lax.broadcasted_iota(jnp.int32, prev.shape, 0)
            r = pltpu.roll(x, k, 0)
            head = jnp.where(row8 < k, pltpu.roll(prev, k, 0), r[0:SUBLANES, :])
            shifted = jnp.concatenate([head, r[SUBLANES:, :]], axis=0)
        else:
            shifted = jnp.concatenate([prev[(CONV_WIDTH - 1 - k) * rpt:, :], x[:tm - k * rpt, :]], axis=0)
        out = out + cw[CONV_WIDTH - 1 - k:CONV_WIDTH - k, :] * shifted
    return out


def _conv_step(x, prevs, cw_ref, cb_ref):
    out = cb_ref[...] + cw_ref[CONV_WIDTH - 1:CONV_WIDTH, :] * x
    for k in range(CONV_WIDTH - 1):
        out = out + cw_ref[k:k + 1, :] * prevs[k]
    return out


def _ssd_gate_norm(y, zs, ng_ref):
    y = y * zs
    gw = y.shape[1] // SSD_GROUPS
    parts = []
    for g in range(SSD_GROUPS):
        seg = y[:, g * gw:(g + 1) * gw]
        parts.append(seg * lax.rsqrt(jnp.mean(seg * seg, axis=-1, keepdims=True) + EPS))
    return jnp.concatenate(parts, axis=1) * ng_ref[...]


def _lru_gates(xr, wax_ref, ba_ref, bx_ref, lam_ref):
    bw = xr.shape[1] // LRU_BLOCKS
    ra, ix = [], []
    for k in range(LRU_BLOCKS):
        g = _dot(xr[:, k * bw:(k + 1) * bw].astype(BF16), wax_ref[k])
        ra.append(g[:, :bw])
        ix.append(g[:, bw:])
    r = jax.nn.sigmoid(jnp.concatenate(ra, axis=1) + ba_ref[...])
    i = jax.nn.sigmoid(jnp.concatenate(ix, axis=1) + bx_ref[...])
    log_a = (-LRU_C) * r * _softplus(-lam_ref[...])
    a = jnp.exp(log_a)
    v = -jnp.tanh(log_a) * (a * a + 1.0)
    mult = jnp.where(v > 0.0, v * lax.rsqrt(v), 0.0)
    return a, mult * i * xr


def _norm_kernel(x_ref, g_ref, o_ref):
    o_ref[...] = _rms(x_ref[...], g_ref[...]).astype(o_ref.dtype)


def _norm_bf16(x, g, *, tm):
    M, D = x.shape
    tm = min(tm, M)
    assert M % tm == 0
    return pl.pallas_call(
        _norm_kernel,
        grid=(M // tm,),
        in_specs=[pl.BlockSpec((tm, D), lambda i: (i, 0)), pl.BlockSpec((1, D), lambda i: (0, 0))],
        out_specs=pl.BlockSpec((tm, D), lambda i: (i, 0)),
        out_shape=jax.ShapeDtypeStruct((M, D), BF16),
        compiler_params=_cparams(("parallel",)),
        name="norm",
    )(x, g)


def _norm_tm_kernel(x_ref, g_ref, xo_ref, xn_ref):
    x = x_ref[...]
    xo_ref[...] = x
    xn_ref[...] = _rms(x, g_ref[...]).astype(xn_ref.dtype)


def _norm_tm(x, g, *, nseq, tm):
    M, D = x.shape
    S = M // nseq
    tm = min(tm, S)
    assert S % tm == 0
    n_i = S // tm
    tmaj = pl.BlockSpec((tm, D), lambda b, i: (i, b))
    xo, xn = pl.pallas_call(
        _norm_tm_kernel,
        grid=(nseq, n_i),
        in_specs=[pl.BlockSpec((tm, D), lambda b, i: (b * n_i + i, 0)), pl.BlockSpec((1, D), lambda b, i: (0, 0))],
        out_specs=[tmaj, tmaj],
        out_shape=[jax.ShapeDtypeStruct((S, nseq * D), F32), jax.ShapeDtypeStruct((S, nseq * D), BF16)],
        compiler_params=_cparams(("parallel", "parallel")),
        name="norm_tm",
    )(x, g)
    return xo.reshape(M, D), xn.reshape(M, D)


_ACT = {"silu": _silu, "gelu": _gelu_tanh}


def _proj_kernel(*refs, mode, has_bias, n_raw, tiles_per_seq, tail_lo, rpt):
    has_conv = mode in ("conv", "conv_silu")
    it = iter(refs)
    xn_ref, w_ref = next(it), next(it)
    b_ref = next(it) if has_bias else None
    cw_ref, cb_ref, conv0_ref = (next(it), next(it), next(it)) if has_conv else (None, None, None)
    o_ref = next(it)
    tail_ref = next(it) if has_conv else None
    wb_s = next(it)
    carry_s = next(it) if has_conv else None
    i = pl.program_id(1)

    @pl.when(i == 0)
    def _():
        wb_s[...] = w_ref[...].astype(BF16)

    tm, tn = o_ref.shape
    for n in range(tn // MXU_COLS):
        sl = pl.ds(n * MXU_COLS, MXU_COLS)
        acc = _dot(xn_ref[...], wb_s[:, sl])
        if has_bias:
            acc = acc + b_ref[:, sl]
        if not has_conv:
            o_ref[:, sl] = _ACT[mode](acc)
            continue
        n_tail, n_carry = tail_ref.shape[1], carry_s.shape[0]
        tail_ref[0, :, sl] = acc[tail_lo:tail_lo + n_tail, :]

        def conv_tile(acc=acc, sl=sl):
            c0 = conv0_ref[0, :, sl]
            if rpt == 1:
                start = lax.rem(i - n_raw, tiles_per_seq) == 0
            else:
                start = i == 0
                c0 = jnp.concatenate([jnp.broadcast_to(c0[SUBLANES - CONV_WIDTH + 1 + k:SUBLANES - CONV_WIDTH + 2 + k, :],
                                                       (rpt, MXU_COLS)) for k in range(CONV_WIDTH - 1)], axis=0)
            prev = jnp.where(start, c0, carry_s[:, sl])
            out = _conv_rows(acc, prev, cw_ref[:, sl], cb_ref[:, sl], rpt)
            carry_s[:, sl] = acc[tm - n_carry:tm, :]
            o_ref[:, sl] = _silu(out) if mode == "conv_silu" else out

        if n_raw == 0:
            conv_tile()
        else:
            @pl.when(i < n_raw)
            def _(acc=acc, sl=sl):
                o_ref[:, sl] = acc

            pl.when(i >= n_raw)(conv_tile)


def _proj(xn, w, layer, col0, n_out, mode, *, tm, tn, bias=None, conv=None, n_raw=0, tiles_per_seq=1,
          tail_lo=0, rpt=1):
    M, D = xn.shape
    tm = min(tm, M)
    assert M % tm == 0 and n_out % tn == 0 and col0 % tn == 0 and tn % MXU_COLS == 0
    n_i, n_j, jb = M // tm, n_out // tn, col0 // tn
    in_specs = [pl.BlockSpec((tm, D), lambda j, i: (i, 0)),
                pl.BlockSpec((None, D, tn), lambda j, i: (layer, 0, jb + j))]
    args = [xn, w]
    if bias is not None:
        in_specs.append(pl.BlockSpec((1, tn), lambda j, i: (0, jb + j)))
        args.append(bias)
    out_shape = [jax.ShapeDtypeStruct((M, n_out), F32)]
    out_specs = [pl.BlockSpec((tm, tn), lambda j, i: (i, j))]
    scratch = [pltpu.VMEM((D, tn), BF16)]
    if conv is not None:
        assert mode in ("conv", "conv_silu")
        cw, cb, conv0 = conv
        in_specs += [pl.BlockSpec((CONV_WIDTH, tn), lambda j, i: (0, j)),
                     pl.BlockSpec((1, tn), lambda j, i: (0, j)),
                     pl.BlockSpec((1, SUBLANES, tn), lambda j, i: (0, 0, j))]
        args += [cw, cb, conv0]
        assert rpt == 1 or (rpt % SUBLANES == 0 and n_raw == 0)
        n_tail = SUBLANES if rpt == 1 else (CONV_WIDTH - 1) * rpt
        if rpt > 1:
            tail_lo = tm - n_tail
        out_shape.append(jax.ShapeDtypeStruct((n_i, n_tail, n_out), F32))
        out_specs.append(pl.BlockSpec((1, n_tail, tn), lambda j, i: (i, 0, j)))
        scratch.append(pltpu.VMEM((n_tail, tn), F32))
    res = pl.pallas_call(
        functools.partial(_proj_kernel, mode=mode, has_bias=bias is not None, n_raw=n_raw,
                          tiles_per_seq=tiles_per_seq, tail_lo=tail_lo, rpt=rpt),
        grid=(n_j, n_i),
        in_specs=in_specs, out_specs=out_specs, out_shape=out_shape, scratch_shapes=scratch,
        compiler_params=_cparams(("arbitrary", "arbitrary")),
        name="proj_" + mode,
    )(*args)
    return res if conv is not None else res[0]


def _dt_kernel(xn_ref, w_ref, dtb_ref, alog_ref, dtv_ref, cs_ref, csT_ref, dtT_ref, *, n_raw, n_valid, i_axis):
    i = pl.program_id(i_axis)
    tm = xn_ref.shape[0]
    T = CHUNK
    dtv = _softplus(_dot(xn_ref[...], w_ref[...].astype(BF16)) + dtb_ref[...])
    if n_valid < tm:
        keep = jnp.logical_or(lax.broadcasted_iota(jnp.int32, dtv.shape, 0) < n_valid, i < n_raw)
        dtv = jnp.where(keep, dtv, 0.0)
    dtv_ref[...] = dtv
    a = dtv * (-jnp.exp(alog_ref[...]))
    tri = (lax.broadcasted_iota(jnp.int32, (T, T), 0) >= lax.broadcasted_iota(jnp.int32, (T, T), 1)).astype(F32)
    for r in range(tm // T):
        cs = jnp.dot(tri, a[r * T:(r + 1) * T, :], precision=lax.Precision.HIGHEST, preferred_element_type=F32)
        cs_ref[r * T:(r + 1) * T, :] = cs
        csT_ref[r] = cs.T
        dtT_ref[r] = dtv[r * T:(r + 1) * T, :].T


def _dt_proj(xn, w_dt, dtb, alog, *, tm, n_raw, n_valid, nseq=1):
    T = CHUNK
    D = w_dt.shape[0]
    S = xn.shape[0]
    tm = min(tm, S)
    assert xn.shape[1] == nseq * D and S % tm == 0 and tm % T == 0 and T == LANES and (n_valid >= tm or tm == T)
    n_i = S // tm
    if nseq == 1:
        grid, rows = (n_i,), (lambda i: (i, 0))
        rows3, const = (lambda i: (i, 0, 0)), (lambda i: (0, 0))
    else:
        grid, rows = (nseq, n_i), (lambda b, i: (i, b))
        rows3, const = (lambda b, i: (b * n_i + i, 0, 0)), (lambda b, i: (0, 0))
    row_spec = pl.BlockSpec((tm, LANES), rows)
    rowT_spec = pl.BlockSpec((tm // T, LANES, T), rows3)
    return pl.pallas_call(
        functools.partial(_dt_kernel, n_raw=n_raw, n_valid=n_valid, i_axis=len(grid) - 1),
        grid=grid,
        in_specs=[pl.BlockSpec((tm, D), rows),
                  pl.BlockSpec((D, LANES), const),
                  pl.BlockSpec((1, LANES), const),
                  pl.BlockSpec((1, LANES), const)],
        out_specs=[row_spec, row_spec, rowT_spec, rowT_spec],
        out_shape=[jax.ShapeDtypeStruct((S, nseq * LANES), F32), jax.ShapeDtypeStruct((S, nseq * LANES), F32),
                   jax.ShapeDtypeStruct((nseq * S // T, LANES, T), F32),
                   jax.ShapeDtypeStruct((nseq * S // T, LANES, T), F32)],
        compiler_params=_cparams(("parallel",) * len(grid)),
        name="dt_proj",
    )(xn, w_dt, dtb, alog)


def _out_ffn_kernel(*refs, fc, has_next):
    y_ref, x_ref, wo_ref, bo_ref, gpost_ref, gpre_ref, w1_ref, w2_ref, gfpost_ref = refs[:9]
    gnext_ref = refs[9] if has_next else None
    o_ref = refs[9 + has_next]
    xn_ref = refs[10 + has_next] if has_next else None
    m = _dot(y_ref[...], wo_ref[...]) + bo_ref[...]
    x1 = x_ref[...] + _rms(m, gpost_ref[...])
    hn = _rms(x1, gpre_ref[...]).astype(BF16)
    dff = w1_ref.shape[1]
    acc = None
    for c in range(dff // fc):
        h = jnp.maximum(_dot(hn, w1_ref[:, c * fc:(c + 1) * fc]), 0.0)
        part = _dot((h * h).astype(BF16), w2_ref[c * fc:(c + 1) * fc, :])
        acc = part if acc is None else acc + part
    x2 = x1 + _rms(acc, gfpost_ref[...])
    o_ref[...] = x2
    if has_next:
        xn_ref[...] = _rms(x2, gnext_ref[...]).astype(xn_ref.dtype)


def _out_ffn(y, x, wo, bo, gpost, gpre, w1, w2, gfpost, gnext, *, tm, fc=1024, nseq=1):
    D = wo.shape[1]
    K = wo.shape[0]
    dff = w1.shape[1]
    S = x.shape[0]
    tm = min(tm, S)
    assert S % tm == 0 and dff % fc == 0 and x.shape[1] == nseq * D and y.shape[1] == nseq * K
    n_i = S // tm
    once = dict(pipeline_mode=pl.Buffered(1))
    has_next = gnext is not None
    if nseq == 1:
        grid, const = (n_i,), (lambda i: (0, 0))
        rows_in = rows_out = lambda i: (i, 0)
    else:
        assert not has_next
        grid, const = (nseq, n_i), (lambda b, i: (0, 0))
        rows_in, rows_out = (lambda b, i: (i, b)), (lambda b, i: (b * n_i + i, 0))
    in_specs = [pl.BlockSpec((tm, K), rows_in), pl.BlockSpec((tm, D), rows_in),
                pl.BlockSpec((K, D), const, **once),
                pl.BlockSpec((1, D), const),
                pl.BlockSpec((1, D), const),
                pl.BlockSpec((1, D), const),
                pl.BlockSpec((D, dff), const, **once),
                pl.BlockSpec((dff, D), const, **once),
                pl.BlockSpec((1, D), const)]
    args = [y, x, wo, bo, gpost, gpre, w1, w2, gfpost]
    out_specs = [pl.BlockSpec((tm, D), rows_out)]
    out_shape = [jax.ShapeDtypeStruct((nseq * S, D), F32)]
    if has_next:
        in_specs.append(pl.BlockSpec((1, D), const))
        args.append(gnext)
        out_specs.append(pl.BlockSpec((tm, D), rows_out))
        out_shape.append(jax.ShapeDtypeStruct((nseq * S, D), BF16))
    res = pl.pallas_call(
        functools.partial(_out_ffn_kernel, fc=fc, has_next=has_next),
        grid=grid,
        in_specs=in_specs, out_specs=out_specs, out_shape=out_shape,
        compiler_params=_cparams(("parallel",) * len(grid)),
        name="out_ffn",
    )(*args)
    return (res[0], res[1]) if has_next else (res[0], None)


def _ssd_chunk_kernel(xbc_ref, zs_ref, cs_ref, csT_ref, dtT_ref, h0_ref, dexp_ref, ng_ref, y_ref, hout_ref,
                      hT_s, *, T):
    c = pl.program_id(1)
    d_inner = zs_ref.shape[1]
    gs = SSD_GROUPS * SSD_STATE
    gw = d_inner // SSD_GROUPS

    @pl.when(c == 0)
    def _():
        for g in range(SSD_GROUPS):
            hT_s[g] = h0_ref[0, g].T

    row = lax.broadcasted_iota(jnp.int32, (T, T), 0)
    col = lax.broadcasted_iota(jnp.int32, (T, T), 1)
    causal = row >= col
    lane = lax.broadcasted_iota(jnp.int32, (T, LANES), 1)
    lo_half = lane < SSD_HEAD_DIM

    cs = cs_ref[...]
    csT = csT_ref[0]
    dtT = dtT_ref[0]
    wdT = dtT * jnp.exp(csT[:, T - 1:T] - csT)
    dec_tot = jnp.exp(cs[T - 1:T, :])

    ys = []
    for g in range(SSD_GROUPS):
        Bg = xbc_ref[:, d_inner + g * SSD_STATE:d_inner + (g + 1) * SSD_STATE]
        Cg = xbc_ref[:, d_inner + gs + g * SSD_STATE:d_inner + gs + (g + 1) * SSD_STATE].astype(BF16)
        BgT = Bg.T
        cb = _dot(Cg, BgT.astype(BF16))
        hTg = hT_s[g]
        yoff = _dot(Cg, hTg.astype(BF16))
        for jj in range(gw // LANES):
            j = g * (gw // LANES) + jj
            xpair = xbc_ref[:, j * LANES:(j + 1) * LANES]
            s_parts, bw_parts, ecs_parts = [], [], []
            for hh in (2 * j, 2 * j + 1):
                csl = jnp.broadcast_to(cs[:, hh:hh + 1], (T, T))
                L = jnp.where(causal, jnp.exp(csl - csT[hh:hh + 1, :]), 0.0)
                s_parts.append((cb * L * dtT[hh:hh + 1, :]).astype(BF16))
                bw_parts.append((BgT * wdT[hh:hh + 1, :]).astype(BF16))
                ecs_parts.append(jnp.exp(csl))
            x2 = jnp.concatenate([jnp.where(lo_half, xpair, 0.0).astype(BF16),
                                  jnp.where(lo_half, 0.0, xpair).astype(BF16)], axis=0)
            ydiag = _dot(jnp.concatenate(s_parts, axis=1), x2)
            upd = _dot(jnp.concatenate(bw_parts, axis=1), x2)
            ecs = jnp.where(lo_half, ecs_parts[0], ecs_parts[1])
            ys.append(ydiag + yoff[:, jj * LANES:(jj + 1) * LANES] * ecs
                      + xpair * dexp_ref[:, j * LANES:(j + 1) * LANES])
            dect = _pair_expand(dec_tot, j, lane[0:1, :])
            hT_s[g, :, jj * LANES:(jj + 1) * LANES] = hTg[:, jj * LANES:(jj + 1) * LANES] * dect + upd

    y = jnp.concatenate(ys, axis=1)
    y_ref[...] = _ssd_gate_norm(y, zs_ref[...], ng_ref).astype(y_ref.dtype)

    @pl.when(c == pl.num_programs(1) - 1)
    def _():
        for g in range(SSD_GROUPS):
            hout_ref[0, g] = hT_s[g].T


def _ssd_chunk(xbc, zs, dtq, h0, p, *, nseq, nchunk, row_block0=0, tmaj=False):
    T = CHUNK
    assert T == LANES
    d_inner = p["dexp"].shape[1]
    cdim = p["conv_w"].shape[1]
    gw = d_inner // SSD_GROUPS
    _, cs, csT, dtT = dtq
    if tmaj:
        rows = lambda b, c: (c, b)
        y_shape = (nchunk * T, nseq * d_inner)
    else:
        rows = lambda b, c: (row_block0 + b * nchunk + c, 0)
        y_shape = (nseq * nchunk * T, d_inner)
    rows_y = rows if tmaj else (lambda b, c: (b * nchunk + c, 0))
    rows3 = lambda b, c: ((0 if tmaj else row_block0) + b * nchunk + c, 0, 0)
    shared4 = (lambda b, c: (0, 0, 0, 0)) if h0.shape[0] == 1 else (lambda b, c: (b, 0, 0, 0))
    const = lambda b, c: (0, 0)
    return pl.pallas_call(
        functools.partial(_ssd_chunk_kernel, T=T),
        grid=(nseq, nchunk),
        in_specs=[pl.BlockSpec((T, cdim), rows),
                  pl.BlockSpec((T, d_inner), rows),
                  pl.BlockSpec((T, LANES), rows),
                  pl.BlockSpec((1, LANES, T), rows3),
                  pl.BlockSpec((1, LANES, T), rows3),
                  pl.BlockSpec((1, SSD_GROUPS, gw, SSD_STATE), shared4),
                  pl.BlockSpec((1, d_inner), const),
                  pl.BlockSpec((1, d_inner), const)],
        out_specs=[pl.BlockSpec((T, d_inner), rows_y),
                   pl.BlockSpec((1, SSD_GROUPS, gw, SSD_STATE), lambda b, c: (b, 0, 0, 0))],
        out_shape=[jax.ShapeDtypeStruct(y_shape, BF16),
                   jax.ShapeDtypeStruct((nseq, SSD_GROUPS, gw, SSD_STATE), F32)],
        scratch_shapes=[pltpu.VMEM((SSD_GROUPS, SSD_STATE, gw), F32)],
        compiler_params=_cparams(("parallel", "arbitrary")),
        name="ssd_chunk",
    )(xbc, zs, cs, csT, dtT, h0, p["dexp"], p["norm"])


def _ssd_step_kernel(*refs, seq_per_step, has_prev):
    (xbc_ref, zs_ref, dt_ref, c0_ref, c1_ref, c2_ref, h_ref, cw_ref, cb_ref, alog_ref, dexp_ref,
     ng_ref) = refs[:12]
    y_ref, hout_ref, xdtT_s, decT_s, b_s, c_s, xs_s, yrow_s = refs[12 + has_prev:]
    s = pl.program_id(0)
    nb, d_inner = xs_s.shape
    gs = SSD_GROUPS * SSD_STATE
    gw = d_inner // SSD_GROUPS

    @pl.when(s == 0)
    def _():
        xbc = _silu(_conv_step(xbc_ref[...], (c0_ref[...], c1_ref[...], c2_ref[...]), cw_ref, cb_ref))
        xs = xbc[:, :d_inner]
        xs_s[...] = xs
        b_s[...] = xbc[:, d_inner:d_inner + gs]
        c_s[...] = xbc[:, d_inner + gs:]
        dtv = dt_ref[...]
        decT_s[...] = jnp.exp(dtv * (-jnp.exp(alog_ref[...]))).T
        dtT = dtv.T
        for j in range(d_inner // LANES):
            xT = xs[:, j * LANES:(j + 1) * LANES].T
            for k in range(2):
                hh = 2 * j + k
                xdtT_s[hh * SSD_HEAD_DIM:(hh + 1) * SSD_HEAD_DIM, :] = (
                    xT[k * SSD_HEAD_DIM:(k + 1) * SSD_HEAD_DIM, :] * dtT[hh:hh + 1, :])

    for q in range(seq_per_step):
        b = s * seq_per_step + q
        sel = lax.broadcasted_iota(jnp.int32, (1, nb), 1) == b
        dec_col = jnp.sum(jnp.where(sel, decT_s[...], 0.0), axis=1, keepdims=True)
        brow = b_s[pl.ds(b, 1), :]
        crow = c_s[pl.ds(b, 1), :]
        for g in range(SSD_GROUPS):
            xcol = jnp.sum(jnp.where(sel, xdtT_s[g * gw:(g + 1) * gw, :], 0.0), axis=1, keepdims=True)
            dcol = jnp.concatenate(
                [jnp.broadcast_to(dec_col[hh:hh + 1, :], (SSD_HEAD_DIM, 1))
                 for hh in range(g * (gw // SSD_HEAD_DIM), (g + 1) * (gw // SSD_HEAD_DIM))], axis=0)
            hn = h_ref[q, g] * dcol + xcol * brow[:, g * SSD_STATE:(g + 1) * SSD_STATE]
            hout_ref[q, g] = hn
            c8 = jnp.broadcast_to(crow[:, g * SSD_STATE:(g + 1) * SSD_STATE], (SUBLANES, SSD_STATE)).astype(BF16)
            yg = _dot_nt(c8, hn.astype(BF16))
            yrow_s[pl.ds(b, 1), g * gw:(g + 1) * gw] = yg[0:1, :]

    @pl.when(s == pl.num_programs(0) - 1)
    def _():
        y = yrow_s[...] + xs_s[...] * dexp_ref[...]
        y_ref[...] = _ssd_gate_norm(y, zs_ref[...], ng_ref).astype(y_ref.dtype)


def _ssd_step(xbc, zs, dt, conv_state, h_all, h_out_prev, p, *, nb, layer, seq_per_step=4):
    d_inner = p["dexp"].shape[1]
    cdim = p["conv_w"].shape[1]
    gw = d_inner // SSD_GROUPS
    gs = SSD_GROUPS * SSD_STATE
    assert nb % seq_per_step == 0
    blk0 = layer * (nb // seq_per_step)
    const = lambda s: (0, 0)
    prev = lambda k: pl.BlockSpec((nb, cdim), lambda s: (0, k))
    conv_state = conv_state.reshape(nb, -1)
    hspec = pl.BlockSpec((seq_per_step, SSD_GROUPS, gw, SSD_STATE), lambda s: (blk0 + s, 0, 0, 0))
    has_prev = h_out_prev is not None
    in_specs = [pl.BlockSpec((nb, cdim), const),
                pl.BlockSpec((nb, d_inner), const),
                pl.BlockSpec((nb, LANES), const),
                prev(0), prev(1), prev(2),
                hspec,
                pl.BlockSpec((CONV_WIDTH, cdim), const),
                pl.BlockSpec((1, cdim), const),
                pl.BlockSpec((1, LANES), const),
                pl.BlockSpec((1, d_inner), const),
                pl.BlockSpec((1, d_inner), const)]
    args = [xbc, zs, dt, conv_state, conv_state, conv_state, h_all, p["conv_w"], p["conv_b"], p["a_log"],
            p["dexp"], p["norm"]]
    if has_prev:
        in_specs.append(pl.BlockSpec(memory_space=pl.ANY))
        args.append(h_out_prev)
    return pl.pallas_call(
        functools.partial(_ssd_step_kernel, seq_per_step=seq_per_step, has_prev=has_prev),
        grid=(nb // seq_per_step,),
        in_specs=in_specs,
        out_specs=[pl.BlockSpec((nb, d_inner), const), hspec],
        out_shape=[jax.ShapeDtypeStruct((nb, d_inner), BF16),
                   jax.ShapeDtypeStruct(h_all.shape, F32)],
        scratch_shapes=[pltpu.VMEM((d_inner, nb), F32),
                        pltpu.VMEM((LANES, nb), F32),
                        pltpu.VMEM((nb, gs), F32),
                        pltpu.VMEM((nb, gs), F32),
                        pltpu.VMEM((nb, d_inner), F32),
                        pltpu.VMEM((nb, d_inner), F32)],
        input_output_aliases={len(args) - 1: 1} if has_prev else {},
        compiler_params=_cparams(("arbitrary",)),
        name="ssd_step",
    )(*args)


def _shift_rows(x, s, fill, row):
    if s % SUBLANES == 0:
        return jnp.concatenate([jnp.full((s, x.shape[1]), fill, x.dtype), x[:x.shape[0] - s, :]], axis=0)
    return jnp.where(row >= s, pltpu.roll(x, s, 0), fill)


def _lru_chunk_kernel(gate_ref, xr_ref, h0_ref, wax_ref, ba_ref, bx_ref, lam_ref, y_ref, hout_ref, hc_s,
                      *, T, n_valid):
    c = pl.program_id(1)

    @pl.when(c == 0)
    def _():
        hc_s[...] = h0_ref[0]

    a, bt = _lru_gates(xr_ref[...], wax_ref, ba_ref, bx_ref, lam_ref)
    row = lax.broadcasted_iota(jnp.int32, a.shape, 0)
    if n_valid < T:
        a = jnp.where(row < n_valid, a, 1.0)
        bt = jnp.where(row < n_valid, bt, 0.0)
    s = 1
    while s < T:
        bt = a * _shift_rows(bt, s, 0.0, row) + bt
        a = a * _shift_rows(a, s, 1.0, row)
        s *= 2
    h = a * hc_s[...] + bt
    hc_s[...] = h[T - 1:T, :]
    y_ref[...] = (h * gate_ref[...]).astype(y_ref.dtype)

    @pl.when(c == pl.num_programs(1) - 1)
    def _():
        hout_ref[0] = h[T - 1:T, :]


def _lru_chunk(gate, xr, h0, p, *, nseq, nchunk, row_block0, n_valid):
    T = CHUNK
    d = xr.shape[1]
    bw = d // LRU_BLOCKS
    rows = lambda b, c: (row_block0 + b * nchunk + c, 0)
    shared_h = (lambda b, c: (0, 0, 0)) if h0.shape[0] == 1 else (lambda b, c: (b, 0, 0))
    const = lambda b, c: (0, 0)
    return pl.pallas_call(
        functools.partial(_lru_chunk_kernel, T=T, n_valid=n_valid),
        grid=(nseq, nchunk),
        in_specs=[pl.BlockSpec((T, d), rows),
                  pl.BlockSpec((T, d), rows),
                  pl.BlockSpec((1, 1, d), shared_h),
                  pl.BlockSpec((LRU_BLOCKS, bw, 2 * bw), lambda b, c: (0, 0, 0)),
                  pl.BlockSpec((1, d), const),
                  pl.BlockSpec((1, d), const),
                  pl.BlockSpec((1, d), const)],
        out_specs=[pl.BlockSpec((T, d), lambda b, c: (b * nchunk + c, 0)),
                   pl.BlockSpec((1, 1, d), lambda b, c: (b, 0, 0))],
        out_shape=[jax.ShapeDtypeStruct((nseq * nchunk * T, d), BF16),
                   jax.ShapeDtypeStruct((nseq, 1, d), F32)],
        scratch_shapes=[pltpu.VMEM((1, d), F32)],
        compiler_params=_cparams(("parallel", "arbitrary")),
        name="lru_chunk",
    )(gate, xr, h0, p["wax"], p["b_a"], p["b_x"], p["lam"])


def _lru_tm_kernel(gate_ref, xr_ref, h0_ref, wax_ref, ba_ref, bx_ref, lam_ref, y_ref, hout_ref, h_s, *, nseq):
    i = pl.program_id(0)
    rows = xr_ref.shape[0]

    @pl.when(i == 0)
    def _():
        h_s[...] = jnp.broadcast_to(h0_ref[0], h_s.shape)

    a, bt = _lru_gates(xr_ref[...], wax_ref, ba_ref, bx_ref, lam_ref)
    h = h_s[...]
    hs = []
    for r in range(rows // nseq):
        h = a[r * nseq:(r + 1) * nseq, :] * h + bt[r * nseq:(r + 1) * nseq, :]
        hs.append(h)
    h_s[...] = h
    y_ref[...] = (jnp.concatenate(hs, axis=0) * gate_ref[...]).astype(y_ref.dtype)

    @pl.when(i == pl.num_programs(0) - 1)
    def _():
        hout_ref[...] = h


def _lru_tm(gate, xr, h0, p, *, nseq, tl):
    M, d = xr.shape
    bw = d // LRU_BLOCKS
    tl = min(tl, M)
    assert M % tl == 0 and tl % nseq == 0 and nseq % SUBLANES == 0
    const = lambda i: (0, 0)
    rows = pl.BlockSpec((tl, d), lambda i: (i, 0))
    return pl.pallas_call(
        functools.partial(_lru_tm_kernel, nseq=nseq),
        grid=(M // tl,),
        in_specs=[rows, rows,
                  pl.BlockSpec((1, 1, d), lambda i: (0, 0, 0)),
                  pl.BlockSpec((LRU_BLOCKS, bw, 2 * bw), lambda i: (0, 0, 0)),
                  pl.BlockSpec((1, d), const),
                  pl.BlockSpec((1, d), const),
                  pl.BlockSpec((1, d), const)],
        out_specs=[rows, pl.BlockSpec((nseq, d), const)],
        out_shape=[jax.ShapeDtypeStruct((M, d), BF16), jax.ShapeDtypeStruct((nseq, d), F32)],
        scratch_shapes=[pltpu.VMEM((nseq, d), F32)],
        compiler_params=_cparams(("arbitrary",)),
        name="lru_tm",
    )(gate, xr, h0, p["wax"], p["b_a"], p["b_x"], p["lam"])


def _lru_step_kernel(gate_ref, xr_ref, c0_ref, c1_ref, c2_ref, h0_ref, cw_ref, cb_ref, wax_ref, ba_ref, bx_ref,
                     lam_ref, y_ref, hout_ref):
    xr = _conv_step(xr_ref[...], (c0_ref[...], c1_ref[...], c2_ref[...]), cw_ref, cb_ref)
    a, bt = _lru_gates(xr, wax_ref, ba_ref, bx_ref, lam_ref)
    h = a * h0_ref[...] + bt
    hout_ref[...] = h
    y_ref[...] = (h * gate_ref[...]).astype(y_ref.dtype)


def _lru_step(gate, xr, conv_state, h0, p, *, nb):
    d = xr.shape[1]
    bw = d // LRU_BLOCKS
    const = lambda i: (0, 0)
    prev = lambda k: pl.BlockSpec((nb, d), lambda i: (0, k))
    conv_state = conv_state.reshape(nb, -1)
    return pl.pallas_call(
        _lru_step_kernel,
        grid=(1,),
        in_specs=[pl.BlockSpec((nb, d), const),
                  pl.BlockSpec((nb, d), const),
                  prev(0), prev(1), prev(2),
                  pl.BlockSpec((nb, d), const),
                  pl.BlockSpec((CONV_WIDTH, d), const),
                  pl.BlockSpec((1, d), const),
                  pl.BlockSpec((LRU_BLOCKS, bw, 2 * bw), lambda i: (0, 0, 0)),
                  pl.BlockSpec((1, d), const),
                  pl.BlockSpec((1, d), const),
                  pl.BlockSpec((1, d), const)],
        out_specs=[pl.BlockSpec((nb, d), const), pl.BlockSpec((nb, d), const)],
        out_shape=[jax.ShapeDtypeStruct((nb, d), BF16), jax.ShapeDtypeStruct((nb, d), F32)],
        compiler_params=_cparams(("arbitrary",)),
        name="lru_step",
    )(gate, xr, conv_state, conv_state, conv_state, h0, p["conv_w"], p["conv_b"], p["wax"], p["b_a"],
      p["b_x"], p["lam"])


def _row(v):
    return v.reshape(1, -1).astype(F32)


def _pad_lanes(v):
    return jnp.pad(v.astype(F32), (0, LANES - v.shape[0])).reshape(1, LANES)


def kernel(x_prompt, x_sample, state_ssd_conv, state_ssd_h, state_lru_conv, state_lru_h, meta_tokens, norm_mix_pre, norm_mix_post, norm_ffn_pre, norm_ffn_post, ssd_w_in, ssd_conv_w, ssd_conv_b, ssd_dt_bias, ssd_a_log, ssd_d, ssd_norm, ssd_w_out, lru_w_in, lru_b_in, lru_conv_w, lru_conv_b, lru_w_a, lru_b_a, lru_w_x, lru_b_x, lru_lambda, lru_w_out, lru_b_out, ffn_w1, ffn_w2):
    B, S, D = x_prompt.shape
    nb = x_sample.shape[0]
    n_meta = meta_tokens.shape[0]
    depth = norm_mix_pre.shape[0]
    T = CHUNK
    n_heads = ssd_dt_bias.shape[1]
    d_inner = n_heads * SSD_HEAD_DIM
    cdim = ssd_conv_w.shape[2]
    d_rnn = lru_conv_w.shape[2]
    gw = d_inner // SSD_GROUPS
    M = B * S
    tm_p = min(PROJ_TM, S)
    assert x_sample.shape[1] == 1 and nb == T and S % tm_p == 0 and tm_p % T == 0 and B % SUBLANES == 0
    assert SUBLANES <= n_meta <= T and n_meta % SUBLANES == 0
    nchunk = S // T
    meta_blk = nb // T

    xs = jnp.concatenate([x_sample[:, 0, :], meta_tokens, jnp.zeros((T - n_meta, D), F32)], axis=0)
    xn_s = _norm_bf16(xs, _row(norm_mix_pre[0]), tm=PROJ_TM)
    xp, xn_p = _norm_tm(x_prompt.reshape(M, D), _row(norm_mix_pre[0]), nseq=B, tm=PROJ_TM)

    small = dict(tm=T, dt_tm=T, n_raw=meta_blk, n_valid=n_meta, nseq=1,
                 conv=dict(tm=T, n_raw=meta_blk, tiles_per_seq=1, tail_lo=n_meta - SUBLANES))
    prompt = dict(tm=PROJ_TM, dt_tm=tm_p, n_raw=0, n_valid=tm_p, nseq=B, conv=dict(tm=PROJ_TM, rpt=B))
    last3 = lambda tail: tail[-1].reshape(CONV_WIDTH - 1, B, -1).transpose(1, 0, 2)
    view = lambda a: a.reshape(S, -1)

    p_ssd_conv, p_ssd_h, p_lru_conv, p_lru_h = [], [], [], []
    s_ssd_conv, s_lru_conv, s_lru_h = [], [], []
    s_ssd_h = None
    for i in range(depth):
        j = i // 2
        if i % 2 == 0:
            w_dt = jnp.pad(ssd_w_in[j][:, d_inner + cdim:], ((0, 0), (0, LANES - n_heads)))
            p = dict(conv_w=ssd_conv_w[j], conv_b=_row(ssd_conv_b[j]), dt_bias=_pad_lanes(ssd_dt_bias[j]),
                     a_log=_pad_lanes(ssd_a_log[j]), dexp=_row(jnp.repeat(ssd_d[j], SSD_HEAD_DIM)),
                     norm=_row(ssd_norm[j]))
            w_out, b_out = ssd_w_out[j].astype(BF16), jnp.zeros((1, D), F32)

            def run_proj(xn, conv0, v):
                xbc, tail = _proj(xn, ssd_w_in, j, d_inner, cdim, "conv_silu", tn=PROJ_TN,
                                  conv=(p["conv_w"], p["conv_b"], conv0), **v["conv"])
                zs = _proj(xn, ssd_w_in, j, 0, d_inner, "silu", tm=v["tm"], tn=PROJ_TN)
                dtq = _dt_proj(xn if v["nseq"] == 1 else view(xn), w_dt, p["dt_bias"], p["a_log"], tm=v["dt_tm"],
                               n_raw=v["n_raw"], n_valid=v["n_valid"], nseq=v["nseq"])
                return xbc, tail, zs, dtq

            xbc_s, tail_s, zs_s, dtq_s = run_proj(xn_s, jnp.zeros((1, SUBLANES, cdim), F32), small)
            y_samp, s_ssd_h = _ssd_step(xbc_s, zs_s, dtq_s[0], state_ssd_conv[j],
                                        state_ssd_h.reshape(-1, SSD_GROUPS, gw, SSD_STATE), s_ssd_h, p,
                                        nb=nb, layer=j)
            y_meta, h_meta = _ssd_chunk(xbc_s, zs_s, dtq_s, jnp.zeros((1, SSD_GROUPS, gw, SSD_STATE), F32), p,
                                        nseq=1, nchunk=1, row_block0=meta_blk)
            conv_meta = tail_s[meta_blk:meta_blk + 1]
            s_ssd_conv.append(jnp.concatenate([state_ssd_conv[j][:, 1:], xbc_s[:nb, None, :]], axis=1))
            y_s = jnp.concatenate([y_samp, y_meta], axis=0)

            xbc_p, tail_p, zs_p, dtq_p = run_proj(xn_p, conv_meta, prompt)
            y_p, h_p = _ssd_chunk(view(xbc_p), view(zs_p), dtq_p, h_meta, p, nseq=B, nchunk=nchunk, tmaj=True)
            y_p = y_p.reshape(M, d_inner)
            p_ssd_conv.append(last3(tail_p))
            p_ssd_h.append(h_p.reshape(B, n_heads, SSD_HEAD_DIM, SSD_STATE))
        else:
            p = dict(conv_w=lru_conv_w[j], conv_b=_row(lru_conv_b[j]),
                     wax=jnp.concatenate([lru_w_a[j], lru_w_x[j]], axis=-1).astype(BF16),
                     b_a=_row(lru_b_a[j]), b_x=_row(lru_b_x[j]), lam=_row(lru_lambda[j]))
            w_out, b_out = lru_w_out[j].astype(BF16), _row(lru_b_out[j])
            b_in = _row(lru_b_in[j])

            def run_proj(xn, conv0, v):
                gate = _proj(xn, lru_w_in, j, 0, d_rnn, "gelu", tm=v["tm"], tn=PROJ_TN, bias=b_in)
                xr, tail = _proj(xn, lru_w_in, j, d_rnn, d_rnn, "conv", tn=PROJ_TN, bias=b_in,
                                 conv=(p["conv_w"], p["conv_b"], conv0), **v["conv"])
                return gate, xr, tail

            gate_s, xr_s, tail_s = run_proj(xn_s, jnp.zeros((1, SUBLANES, d_rnn), F32), small)
            y_samp, h_samp = _lru_step(gate_s, xr_s, state_lru_conv[j], state_lru_h[j], p, nb=nb)
            y_meta, h_meta = _lru_chunk(gate_s, xr_s, jnp.zeros((1, 1, d_rnn), F32), p,
                                        nseq=1, nchunk=1, row_block0=meta_blk, n_valid=n_meta)
            conv_meta = tail_s[meta_blk:meta_blk + 1]
            s_lru_conv.append(jnp.concatenate([state_lru_conv[j][:, 1:], xr_s[:nb, None, :]], axis=1))
            s_lru_h.append(h_samp)
            y_s = jnp.concatenate([y_samp, y_meta], axis=0)

            gate_p, xr_p, tail_p = run_proj(xn_p, conv_meta, prompt)
            y_p, h_p = _lru_tm(gate_p, xr_p, h_meta, p, nseq=B, tl=PROJ_TM)
            p_lru_conv.append(last3(tail_p))
            p_lru_h.append(h_p)

        g_next = _row(norm_mix_pre[i + 1]) if i + 1 < depth else None
        ffn = (w_out, b_out, _row(norm_mix_post[i]), _row(norm_ffn_pre[i]), ffn_w1[i].astype(BF16),
               ffn_w2[i].astype(BF16), _row(norm_ffn_post[i]), g_next)
        xs, xn_s = _out_ffn(y_s, xs, *ffn, tm=FFN_TM)
        if g_next is not None:
            xp, xn_p = _out_ffn(y_p, xp, *ffn, tm=FFN_TM)
        else:
            xp, _ = _out_ffn(view(y_p), view(xp), *ffn, tm=FFN_TM, nseq=B)

    return (xp.reshape(B, S, D), xs[:nb].reshape(nb, 1, D),
            jnp.stack(p_ssd_conv), jnp.stack(p_ssd_h), jnp.stack(p_lru_conv), jnp.stack(p_lru_h),
            jnp.stack(s_ssd_conv), s_ssd_h.reshape(state_ssd_h.shape), jnp.stack(s_lru_conv), jnp.stack(s_lru_h))
```

```python
import functools
import math

import jax
import jax.numpy as jnp
from jax import lax
from jax.experimental import pallas as pl
from jax.experimental.pallas import tpu as pltpu

F32 = jnp.float32
BF16 = jnp.bfloat16

EPS = 1e-6
CONV_WIDTH = 4
SSD_HEAD_DIM = 64
SSD_GROUPS = 8
SSD_STATE = 128
LRU_BLOCKS = 8
LRU_C = 8.0
CHUNK = 128
LANES = 128
SUBLANES = 8
MXU_COLS = 256
VMEM_LIMIT = 56 * 1024 * 1024
PROJ_TM = 1024
PROJ_TN = 1024
FFN_TM = 512
LRU_TT = 64


def _cparams(sem):
    return pltpu.CompilerParams(dimension_semantics=sem, vmem_limit_bytes=VMEM_LIMIT)


def _rms(x, g):
    return x * lax.rsqrt(jnp.mean(x * x, axis=-1, keepdims=True) + EPS) * g


def _softplus(x):
    return jnp.maximum(x, 0.0) + jnp.log1p(jnp.exp(-jnp.abs(x)))


def _silu(x):
    return x * jax.nn.sigmoid(x)


def _gelu_tanh(x):
    return 0.5 * x * (1.0 + jnp.tanh(math.sqrt(2.0 / math.pi) * (x + 0.044715 * (x * x * x))))


def _dot(a, b):
    return jnp.dot(a, b, preferred_element_type=F32)


def _dot_nt(a, b):
    return lax.dot_general(a, b, (((1,), (1,)), ((), ())), preferred_element_type=F32)


def _pair_expand(q, j, lane):
    return jnp.where(lane < SSD_HEAD_DIM, q[:, 2 * j:2 * j + 1], q[:, 2 * j + 1:2 * j + 2])


def _conv_rows(x, prev, cw, cb, rpt):
    tm = x.shape[0]
    out = cb + cw[CONV_WIDTH - 1:CONV_WIDTH, :] * x
    for k in range(1, CONV_WIDTH):
        if rpt == 1:
            row8 = lax.broadcasted_iota(jnp.int32, prev.shape, 0)
            r = pltpu.roll(x, k, 0)
            head = jnp.where(row8 < k, pltpu.roll(prev, k, 0), r[0:SUBLANES, :])
            shifted = jnp.concatenate([head, r[SUBLANES:, :]], axis=0)
        else:
            shifted = jnp.concatenate([prev[(CONV_WIDTH - 1 - k) * rpt:, :], x[:tm - k * rpt, :]], axis=0)
        out = out + cw[CONV_WIDTH - 1 - k:CONV_WIDTH - k, :] * shifted
    return out


def _conv_step(x, prevs, cw_ref, cb_ref):
    out = cb_ref[...] + cw_ref[CONV_WIDTH - 1:CONV_WIDTH, :] * x
    for k in range(CONV_WIDTH - 1):
        out = out + cw_ref[k:k + 1, :] * prevs[k]
    return out


def _ssd_gate_norm(y, zs, ng_ref):
    y = y * zs
    gw = y.shape[1] // SSD_GROUPS
    parts = []
    for g in range(SSD_GROUPS):
        seg = y[:, g * gw:(g + 1) * gw]
        parts.append(seg * lax.rsqrt(jnp.mean(seg * seg, axis=-1, keepdims=True) + EPS))
    return jnp.concatenate(parts, axis=1) * ng_ref[...]


def _lru_gates(xr, wax_ref, ba_ref, bx_ref, lam_ref):
    bw = xr.shape[1] // LRU_BLOCKS
    ra, ix = [], []
    for k in range(LRU_BLOCKS):
        g = _dot(xr[:, k * bw:(k + 1) * bw].astype(BF16), wax_ref[k])
        ra.append(g[:, :bw])
        ix.append(g[:, bw:])
    r = jax.nn.sigmoid(jnp.concatenate(ra, axis=1) + ba_ref[...])
    i = jax.nn.sigmoid(jnp.concatenate(ix, axis=1) + bx_ref[...])
    log_a = (-LRU_C) * r * _softplus(-lam_ref[...])
    a = jnp.exp(log_a)
    v = -jnp.tanh(log_a) * (a * a + 1.0)
    mult = jnp.where(v > 0.0, v * lax.rsqrt(v), 0.0)
    return a, mult * i * xr


def _norm_kernel(x_ref, g_ref, o_ref):
    o_ref[...] = _rms(x_ref[...], g_ref[...]).astype(o_ref.dtype)


def _norm_bf16(x, g, *, tm):
    M, D = x.shape
    tm = min(tm, M)
    assert M % tm == 0
    return pl.pallas_call(
        _norm_kernel,
        grid=(M // tm,),
        in_specs=[pl.BlockSpec((tm, D), lambda i: (i, 0)), pl.BlockSpec((1, D), lambda i: (0, 0))],
        out_specs=pl.BlockSpec((tm, D), lambda i: (i, 0)),
        out_shape=jax.ShapeDtypeStruct((M, D), BF16),
        compiler_params=_cparams(("parallel",)),
        name="norm",
    )(x, g)


_ACT = {"none": lambda v: v, "silu": _silu, "gelu": _gelu_tanh}


def _proj_kernel(*refs, mode, has_bias, n_raw, tiles_per_seq, tail_lo, rpt):
    has_conv = mode in ("conv", "conv_silu")
    it = iter(refs)
    xn_ref, w_ref = next(it), next(it)
    b_ref = next(it) if has_bias else None
    cw_ref, cb_ref, conv0_ref = (next(it), next(it), next(it)) if has_conv else (None, None, None)
    o_ref = next(it)
    tail_ref = next(it) if has_conv else None
    wb_s = next(it)
    carry_s = next(it) if has_conv else None
    i = pl.program_id(1)

    @pl.when(i == 0)
    def _():
        wb_s[...] = w_ref[...].astype(BF16)

    tm, tn = o_ref.shape
    for n in range(tn // MXU_COLS):
        sl = pl.ds(n * MXU_COLS, MXU_COLS)
        acc = _dot(xn_ref[...], wb_s[:, sl])
        if has_bias:
            acc = acc + b_ref[:, sl]
        if not has_conv:
            o_ref[:, sl] = _ACT[mode](acc)
            continue
        n_tail, n_carry = tail_ref.shape[1], carry_s.shape[0]
        tail_ref[0, :, sl] = acc[tail_lo:tail_lo + n_tail, :]

        def conv_tile(acc=acc, sl=sl):
            c0 = conv0_ref[0, :, sl]
            if rpt == 1:
                start = lax.rem(i - n_raw, tiles_per_seq) == 0
            else:
                start = i == 0
                c0 = jnp.concatenate([jnp.broadcast_to(c0[SUBLANES - CONV_WIDTH + 1 + k:SUBLANES - CONV_WIDTH + 2 + k, :],
                                                       (rpt, MXU_COLS)) for k in range(CONV_WIDTH - 1)], axis=0)
            prev = jnp.where(start, c0, carry_s[:, sl])
            out = _conv_rows(acc, prev, cw_ref[:, sl], cb_ref[:, sl], rpt)
            carry_s[:, sl] = acc[tm - n_carry:tm, :]
            o_ref[:, sl] = _silu(out) if mode == "conv_silu" else out

        if n_raw == 0:
            conv_tile()
        else:
            @pl.when(i < n_raw)
            def _(acc=acc, sl=sl):
                o_ref[:, sl] = acc

            pl.when(i >= n_raw)(conv_tile)


def _proj(xn, w, layer, col0, n_out, mode, *, tm, tn, bias=None, conv=None, n_raw=0, tiles_per_seq=1,
          tail_lo=0, rpt=1):
    M, D = xn.shape
    tm = min(tm, M)
    assert M % tm == 0 and n_out % tn == 0 and col0 % tn == 0 and tn % MXU_COLS == 0
    n_i, n_j, jb = M // tm, n_out // tn, col0 // tn
    in_specs = [pl.BlockSpec((tm, D), lambda j, i: (i, 0)),
                pl.BlockSpec((None, D, tn), lambda j, i: (layer, 0, jb + j))]
    args = [xn, w]
    if bias is not None:
        in_specs.append(pl.BlockSpec((1, tn), lambda j, i: (0, jb + j)))
        args.append(bias)
    out_shape = [jax.ShapeDtypeStruct((M, n_out), F32)]
    out_specs = [pl.BlockSpec((tm, tn), lambda j, i: (i, j))]
    scratch = [pltpu.VMEM((D, tn), BF16)]
    if conv is not None:
        assert mode in ("conv", "conv_silu")
        cw, cb, conv0 = conv
        in_specs += [pl.BlockSpec((CONV_WIDTH, tn), lambda j, i: (0, j)),
                     pl.BlockSpec((1, tn), lambda j, i: (0, j)),
                     pl.BlockSpec((1, SUBLANES, tn), lambda j, i: (0, 0, j))]
        args += [cw, cb, conv0]
        assert rpt == 1 or (rpt % SUBLANES == 0 and n_raw == 0)
        n_tail = SUBLANES if rpt == 1 else (CONV_WIDTH - 1) * rpt
        if rpt > 1:
            tail_lo = tm - n_tail
        out_shape.append(jax.ShapeDtypeStruct((n_i, n_tail, n_out), F32))
        out_specs.append(pl.BlockSpec((1, n_tail, tn), lambda j, i: (i, 0, j)))
        scratch.append(pltpu.VMEM((n_tail, tn), F32))
    res = pl.pallas_call(
        functools.partial(_proj_kernel, mode=mode, has_bias=bias is not None, n_raw=n_raw,
                          tiles_per_seq=tiles_per_seq, tail_lo=tail_lo, rpt=rpt),
        grid=(n_j, n_i),
        in_specs=in_specs, out_specs=out_specs, out_shape=out_shape, scratch_shapes=scratch,
        compiler_params=_cparams(("arbitrary", "arbitrary")),
        name="proj_" + mode,
    )(*args)
    return res if conv is not None else res[0]


def _dt_kernel(xn_ref, w_ref, dtb_ref, alog_ref, dtv_ref, cs_ref, csT_ref, dtT_ref, *, n_raw, n_valid, i_axis):
    i = pl.program_id(i_axis)
    tm = xn_ref.shape[0]
    T = CHUNK
    dtv = _softplus(_dot(xn_ref[...], w_ref[...].astype(BF16)) + dtb_ref[...])
    if n_valid < tm:
        keep = jnp.logical_or(lax.broadcasted_iota(jnp.int32, dtv.shape, 0) < n_valid, i < n_raw)
        dtv = jnp.where(keep, dtv, 0.0)
    dtv_ref[...] = dtv
    a = dtv * (-jnp.exp(alog_ref[...]))
    tri = (lax.broadcasted_iota(jnp.int32, (T, T), 0) >= lax.broadcasted_iota(jnp.int32, (T, T), 1)).astype(F32)
    for r in range(tm // T):
        cs = jnp.dot(tri, a[r * T:(r + 1) * T, :], precision=lax.Precision.HIGHEST, preferred_element_type=F32)
        cs_ref[r * T:(r + 1) * T, :] = cs
        csT_ref[r] = cs.T
        dtT_ref[r] = dtv[r * T:(r + 1) * T, :].T


def _dt_proj(xn, w_dt, dtb, alog, *, tm, n_raw, n_valid, nseq=1):
    T = CHUNK
    D = w_dt.shape[0]
    S = xn.shape[0]
    tm = min(tm, S)
    assert xn.shape[1] == nseq * D and S % tm == 0 and tm % T == 0 and T == LANES and (n_valid >= tm or tm == T)
    n_i = S // tm
    if nseq == 1:
        grid, rows = (n_i,), (lambda i: (i, 0))
        rows3, const = (lambda i: (i, 0, 0)), (lambda i: (0, 0))
    else:
        grid, rows = (nseq, n_i), (lambda b, i: (i, b))
        rows3, const = (lambda b, i: (b * n_i + i, 0, 0)), (lambda b, i: (0, 0))
    row_spec = pl.BlockSpec((tm, LANES), rows)
    rowT_spec = pl.BlockSpec((tm // T, LANES, T), rows3)
    return pl.pallas_call(
        functools.partial(_dt_kernel, n_raw=n_raw, n_valid=n_valid, i_axis=len(grid) - 1),
        grid=grid,
        in_specs=[pl.BlockSpec((tm, D), rows),
                  pl.BlockSpec((D, LANES), const),
                  pl.BlockSpec((1, LANES), const),
                  pl.BlockSpec((1, LANES), const)],
        out_specs=[row_spec, row_spec, rowT_spec, rowT_spec],
        out_shape=[jax.ShapeDtypeStruct((S, nseq * LANES), F32), jax.ShapeDtypeStruct((S, nseq * LANES), F32),
                   jax.ShapeDtypeStruct((nseq * S // T, LANES, T), F32),
                   jax.ShapeDtypeStruct((nseq * S // T, LANES, T), F32)],
        compiler_params=_cparams(("parallel",) * len(grid)),
        name="dt_proj",
    )(xn, w_dt, dtb, alog)


def _out_ffn_kernel(*refs, fc, has_next):
    y_ref, x_ref, wo_ref, bo_ref, gpost_ref, gpre_ref, w1_ref, w2_ref, gfpost_ref = refs[:9]
    gnext_ref = refs[9] if has_next else None
    o_ref = refs[9 + has_next]
    xn_ref = refs[10 + has_next] if has_next else None
    m = _dot(y_ref[...], wo_ref[...]) + bo_ref[...]
    x1 = x_ref[...] + _rms(m, gpost_ref[...])
    hn = _rms(x1, gpre_ref[...]).astype(BF16)
    dff = w1_ref.shape[1]
    acc = None
    for c in range(dff // fc):
        h = jnp.maximum(_dot(hn, w1_ref[:, c * fc:(c + 1) * fc]), 0.0)
        part = _dot((h * h).astype(BF16), w2_ref[c * fc:(c + 1) * fc, :])
        acc = part if acc is None else acc + part
    x2 = x1 + _rms(acc, gfpost_ref[...])
    o_ref[...] = x2
    if has_next:
        xn_ref[...] = _rms(x2, gnext_ref[...]).astype(xn_ref.dtype)


def _out_ffn(y, x, wo, bo, gpost, gpre, w1, w2, gfpost, gnext, *, tm, fc=1024, nseq=1):
    D = wo.shape[1]
    K = wo.shape[0]
    dff = w1.shape[1]
    S = x.shape[0]
    tm = min(tm, S)
    assert S % tm == 0 and dff % fc == 0 and x.shape[1] == nseq * D and y.shape[1] == nseq * K
    n_i = S // tm
    once = dict(pipeline_mode=pl.Buffered(1))
    has_next = gnext is not None
    if nseq == 1:
        grid, const = (n_i,), (lambda i: (0, 0))
        rows_in = rows_out = lambda i: (i, 0)
    else:
        assert not has_next
        grid, const = (nseq, n_i), (lambda b, i: (0, 0))
        rows_in, rows_out = (lambda b, i: (i, b)), (lambda b, i: (b * n_i + i, 0))
    in_specs = [pl.BlockSpec((tm, K), rows_in), pl.BlockSpec((tm, D), rows_in),
                pl.BlockSpec((K, D), const, **once),
                pl.BlockSpec((1, D), const),
                pl.BlockSpec((1, D), const),
                pl.BlockSpec((1, D), const),
                pl.BlockSpec((D, dff), const, **once),
                pl.BlockSpec((dff, D), const, **once),
                pl.BlockSpec((1, D), const)]
    args = [y, x, wo, bo, gpost, gpre, w1, w2, gfpost]
    out_specs = [pl.BlockSpec((tm, D), rows_out)]
    out_shape = [jax.ShapeDtypeStruct((nseq * S, D), F32)]
    if has_next:
        in_specs.append(pl.BlockSpec((1, D), const))
        args.append(gnext)
        out_specs.append(pl.BlockSpec((tm, D), rows_out))
        out_shape.append(jax.ShapeDtypeStruct((nseq * S, D), BF16))
    res = pl.pallas_call(
        functools.partial(_out_ffn_kernel, fc=fc, has_next=has_next),
        grid=grid,
        in_specs=in_specs, out_specs=out_specs, out_shape=out_shape,
        compiler_params=_cparams(("parallel",) * len(grid)),
        name="out_ffn",
    )(*args)
    return (res[0], res[1]) if has_next else (res[0], None)


def _ssd_chunk_kernel(xbc_ref, zs_ref, cs_ref, csT_ref, dtT_ref, h0_ref, dexp_ref, ng_ref, y_ref, hout_ref,
                      hT_s, *, T):
    c = pl.program_id(1)
    d_inner = zs_ref.shape[1]
    gs = SSD_GROUPS * SSD_STATE
    gw = d_inner // SSD_GROUPS

    @pl.when(c == 0)
    def _():
        for g in range(SSD_GROUPS):
            hT_s[g] = h0_ref[0, g].T

    row = lax.broadcasted_iota(jnp.int32, (T, T), 0)
    col = lax.broadcasted_iota(jnp.int32, (T, T), 1)
    causal = row >= col
    lane = lax.broadcasted_iota(jnp.int32, (T, LANES), 1)
    lo_half = lane < SSD_HEAD_DIM

    cs = cs_ref[...]
    csT = csT_ref[0]
    dtT = dtT_ref[0]
    wdT = dtT * jnp.exp(csT[:, T - 1:T] - csT)
    dec_tot = jnp.exp(cs[T - 1:T, :])

    ys = []
    for g in range(SSD_GROUPS):
        Bg = xbc_ref[:, d_inner + g * SSD_STATE:d_inner + (g + 1) * SSD_STATE]
        Cg = xbc_ref[:, d_inner + gs + g * SSD_STATE:d_inner + gs + (g + 1) * SSD_STATE].astype(BF16)
        BgT = Bg.T
        cb = _dot(Cg, BgT.astype(BF16))
        hTg = hT_s[g]
        yoff = _dot(Cg, hTg.astype(BF16))
        for jj in range(gw // LANES):
            j = g * (gw // LANES) + jj
            xpair = xbc_ref[:, j * LANES:(j + 1) * LANES]
            s_parts, bw_parts, ecs_parts = [], [], []
            for hh in (2 * j, 2 * j + 1):
                csl = jnp.broadcast_to(cs[:, hh:hh + 1], (T, T))
                L = jnp.where(causal, jnp.exp(csl - csT[hh:hh + 1, :]), 0.0)
                s_parts.append((cb * L * dtT[hh:hh + 1, :]).astype(BF16))
                bw_parts.append((BgT * wdT[hh:hh + 1, :]).astype(BF16))
                ecs_parts.append(jnp.exp(csl))
            x2 = jnp.concatenate([jnp.where(lo_half, xpair, 0.0).astype(BF16),
                                  jnp.where(lo_half, 0.0, xpair).astype(BF16)], axis=0)
            ydiag = _dot(jnp.concatenate(s_parts, axis=1), x2)
            upd = _dot(jnp.concatenate(bw_parts, axis=1), x2)
            ecs = jnp.where(lo_half, ecs_parts[0], ecs_parts[1])
            ys.append(ydiag + yoff[:, jj * LANES:(jj + 1) * LANES] * ecs
                      + xpair * dexp_ref[:, j * LANES:(j + 1) * LANES])
            dect = _pair_expand(dec_tot, j, lane[0:1, :])
            hT_s[g, :, jj * LANES:(jj + 1) * LANES] = hTg[:, jj * LANES:(jj + 1) * LANES] * dect + upd

    y = jnp.concatenate(ys, axis=1)
    y_ref[...] = _ssd_gate_norm(y, zs_ref[...], ng_ref).astype(y_ref.dtype)

    @pl.when(c == pl.num_programs(1) - 1)
    def _():
        for g in range(SSD_GROUPS):
            hout_ref[0, g] = hT_s[g].T


def _ssd_chunk(xbc, zs, dtq, h0, p, *, nseq, nchunk, row_block0=0, tmaj=False):
    T = CHUNK
    assert T == LANES
    d_inner = p["dexp"].shape[1]
    cdim = p["conv_w"].shape[1]
    gw = d_inner // SSD_GROUPS
    _, cs, csT, dtT = dtq
    if tmaj:
        rows = lambda b, c: (c, b)
        y_shape = (nchunk * T, nseq * d_inner)
    else:
        rows = lambda b, c: (row_block0 + b * nchunk + c, 0)
        y_shape = (nseq * nchunk * T, d_inner)
    rows_y = rows if tmaj else (lambda b, c: (b * nchunk + c, 0))
    rows3 = lambda b, c: ((0 if tmaj else row_block0) + b * nchunk + c, 0, 0)
    shared4 = (lambda b, c: (0, 0, 0, 0)) if h0.shape[0] == 1 else (lambda b, c: (b, 0, 0, 0))
    const = lambda b, c: (0, 0)
    return pl.pallas_call(
        functools.partial(_ssd_chunk_kernel, T=T),
        grid=(nseq, nchunk),
        in_specs=[pl.BlockSpec((T, cdim), rows),
                  pl.BlockSpec((T, d_inner), rows),
                  pl.BlockSpec((T, LANES), rows),
                  pl.BlockSpec((1, LANES, T), rows3),
                  pl.BlockSpec((1, LANES, T), rows3),
                  pl.BlockSpec((1, SSD_GROUPS, gw, SSD_STATE), shared4),
                  pl.BlockSpec((1, d_inner), const),
                  pl.BlockSpec((1, d_inner), const)],
        out_specs=[pl.BlockSpec((T, d_inner), rows_y),
                   pl.BlockSpec((1, SSD_GROUPS, gw, SSD_STATE), lambda b, c: (b, 0, 0, 0))],
        out_shape=[jax.ShapeDtypeStruct(y_shape, BF16),
                   jax.ShapeDtypeStruct((nseq, SSD_GROUPS, gw, SSD_STATE), F32)],
        scratch_shapes=[pltpu.VMEM((SSD_GROUPS, SSD_STATE, gw), F32)],
        compiler_params=_cparams(("parallel", "arbitrary")),
        name="ssd_chunk",
    )(xbc, zs, cs, csT, dtT, h0, p["dexp"], p["norm"])


def _ssd_step_kernel(*refs, seq_per_step, has_prev):
    (xbc_ref, zs_ref, dt_ref, c0_ref, c1_ref, c2_ref, h_ref, cw_ref, cb_ref, alog_ref, dexp_ref,
     ng_ref) = refs[:12]
    y_ref, hout_ref, xdtT_s, decT_s, b_s, c_s, xs_s, yrow_s = refs[12 + has_prev:]
    s = pl.program_id(0)
    nb, d_inner = xs_s.shape
    gs = SSD_GROUPS * SSD_STATE
    gw = d_inner // SSD_GROUPS

    @pl.when(s == 0)
    def _():
        xbc = _silu(_conv_step(xbc_ref[...], (c0_ref[...], c1_ref[...], c2_ref[...]), cw_ref, cb_ref))
        xs = xbc[:, :d_inner]
        xs_s[...] = xs
        b_s[...] = xbc[:, d_inner:d_inner + gs]
        c_s[...] = xbc[:, d_inner + gs:]
        dtv = dt_ref[...]
        decT_s[...] = jnp.exp(dtv * (-jnp.exp(alog_ref[...]))).T
        dtT = dtv.T
        for j in range(d_inner // LANES):
            xT = xs[:, j * LANES:(j + 1) * LANES].T
            for k in range(2):
                hh = 2 * j + k
                xdtT_s[hh * SSD_HEAD_DIM:(hh + 1) * SSD_HEAD_DIM, :] = (
                    xT[k * SSD_HEAD_DIM:(k + 1) * SSD_HEAD_DIM, :] * dtT[hh:hh + 1, :])

    for q in range(seq_per_step):
        b = s * seq_per_step + q
        sel = lax.broadcasted_iota(jnp.int32, (1, nb), 1) == b
        dec_col = jnp.sum(jnp.where(sel, decT_s[...], 0.0), axis=1, keepdims=True)
        brow = b_s[pl.ds(b, 1), :]
        crow = c_s[pl.ds(b, 1), :]
        for g in range(SSD_GROUPS):
            xcol = jnp.sum(jnp.where(sel, xdtT_s[g * gw:(g + 1) * gw, :], 0.0), axis=1, keepdims=True)
            dcol = jnp.concatenate(
                [jnp.broadcast_to(dec_col[hh:hh + 1, :], (SSD_HEAD_DIM, 1))
                 for hh in range(g * (gw // SSD_HEAD_DIM), (g + 1) * (gw // SSD_HEAD_DIM))], axis=0)
            hn = h_ref[q, g] * dcol + xcol * brow[:, g * SSD_STATE:(g + 1) * SSD_STATE]
            hout_ref[q, g] = hn
            c8 = jnp.broadcast_to(crow[:, g * SSD_STATE:(g + 1) * SSD_STATE], (SUBLANES, SSD_STATE)).astype(BF16)
            yg = _dot_nt(c8, hn.astype(BF16))
            yrow_s[pl.ds(b, 1), g * gw:(g + 1) * gw] = yg[0:1, :]

    @pl.when(s == pl.num_programs(0) - 1)
    def _():
        y = yrow_s[...] + xs_s[...] * dexp_ref[...]
        y_ref[...] = _ssd_gate_norm(y, zs_ref[...], ng_ref).astype(y_ref.dtype)


def _ssd_step(xbc, zs, dt, conv_state, h_all, h_out_prev, p, *, nb, layer, seq_per_step=4):
    d_inner = p["dexp"].shape[1]
    cdim = p["conv_w"].shape[1]
    gw = d_inner // SSD_GROUPS
    gs = SSD_GROUPS * SSD_STATE
    assert nb % seq_per_step == 0
    blk0 = layer * (nb // seq_per_step)
    const = lambda s: (0, 0)
    prev = lambda k: pl.BlockSpec((nb, cdim), lambda s: (0, k))
    conv_state = conv_state.reshape(nb, -1)
    hspec = pl.BlockSpec((seq_per_step, SSD_GROUPS, gw, SSD_STATE), lambda s: (blk0 + s, 0, 0, 0))
    has_prev = h_out_prev is not None
    in_specs = [pl.BlockSpec((nb, cdim), const),
                pl.BlockSpec((nb, d_inner), const),
                pl.BlockSpec((nb, LANES), const),
                prev(0), prev(1), prev(2),
                hspec,
                pl.BlockSpec((CONV_WIDTH, cdim), const),
                pl.BlockSpec((1, cdim), const),
                pl.BlockSpec((1, LANES), const),
                pl.BlockSpec((1, d_inner), const),
                pl.BlockSpec((1, d_inner), const)]
    args = [xbc, zs, dt, conv_state, conv_state, conv_state, h_all, p["conv_w"], p["conv_b"], p["a_log"],
            p["dexp"], p["norm"]]
    if has_prev:
        in_specs.append(pl.BlockSpec(memory_space=pl.ANY))
        args.append(h_out_prev)
    return pl.pallas_call(
        functools.partial(_ssd_step_kernel, seq_per_step=seq_per_step, has_prev=has_prev),
        grid=(nb // seq_per_step,),
        in_specs=in_specs,
        out_specs=[pl.BlockSpec((nb, d_inner), const), hspec],
        out_shape=[jax.ShapeDtypeStruct((nb, d_inner), BF16),
                   jax.ShapeDtypeStruct(h_all.shape, F32)],
        scratch_shapes=[pltpu.VMEM((d_inner, nb), F32),
                        pltpu.VMEM((LANES, nb), F32),
                        pltpu.VMEM((nb, gs), F32),
                        pltpu.VMEM((nb, gs), F32),
                        pltpu.VMEM((nb, d_inner), F32),
                        pltpu.VMEM((nb, d_inner), F32)],
        input_output_aliases={len(args) - 1: 1} if has_prev else {},
        compiler_params=_cparams(("arbitrary",)),
        name="ssd_step",
    )(*args)


def _shift_rows(x, s, fill, row):
    if s % SUBLANES == 0:
        return jnp.concatenate([jnp.full((s, x.shape[1]), fill, x.dtype), x[:x.shape[0] - s, :]], axis=0)
    return jnp.where(row >= s, pltpu.roll(x, s, 0), fill)


def _lru_chunk_kernel(gate_ref, xr_ref, h0_ref, wax_ref, ba_ref, bx_ref, lam_ref, y_ref, hout_ref, hc_s,
                      *, T, n_valid):
    c = pl.program_id(1)

    @pl.when(c == 0)
    def _():
        hc_s[...] = h0_ref[0]

    a, bt = _lru_gates(xr_ref[...], wax_ref, ba_ref, bx_ref, lam_ref)
    row = lax.broadcasted_iota(jnp.int32, a.shape, 0)
    if n_valid < T:
        a = jnp.where(row < n_valid, a, 1.0)
        bt = jnp.where(row < n_valid, bt, 0.0)
    s = 1
    while s < T:
        bt = a * _shift_rows(bt, s, 0.0, row) + bt
        a = a * _shift_rows(a, s, 1.0, row)
        s *= 2
    h = a * hc_s[...] + bt
    hc_s[...] = h[T - 1:T, :]
    y_ref[...] = (h * gate_ref[...]).astype(y_ref.dtype)

    @pl.when(c == pl.num_programs(1) - 1)
    def _():
        hout_ref[0] = h[T - 1:T, :]


def _lru_chunk(gate, xr, h0, p, *, nseq, nchunk, row_block0, n_valid):
    T = CHUNK
    d = xr.shape[1]
    bw = d // LRU_BLOCKS
    rows = lambda b, c: (row_block0 + b * nchunk + c, 0)
    shared_h = (lambda b, c: (0, 0, 0)) if h0.shape[0] == 1 else (lambda b, c: (b, 0, 0))
    const = lambda b, c: (0, 0)
    return pl.pallas_call(
        functools.partial(_lru_chunk_kernel, T=T, n_valid=n_valid),
        grid=(nseq, nchunk),
        in_specs=[pl.BlockSpec((T, d), rows),
                  pl.BlockSpec((T, d), rows),
                  pl.BlockSpec((1, 1, d), shared_h),
                  pl.BlockSpec((LRU_BLOCKS, bw, 2 * bw), lambda b, c: (0, 0, 0)),
                  pl.BlockSpec((1, d), const),
                  pl.BlockSpec((1, d), const),
                  pl.BlockSpec((1, d), const)],
        out_specs=[pl.BlockSpec((T, d), lambda b, c: (b * nchunk + c, 0)),
                   pl.BlockSpec((1, 1, d), lambda b, c: (b, 0, 0))],
        out_shape=[jax.ShapeDtypeStruct((nseq * nchunk * T, d), BF16),
                   jax.ShapeDtypeStruct((nseq, 1, d), F32)],
        scratch_shapes=[pltpu.VMEM((1, d), F32)],
        compiler_params=_cparams(("parallel", "arbitrary")),
        name="lru_chunk",
    )(gate, xr, h0, p["wax"], p["b_a"], p["b_x"], p["lam"])


def _lru_seq_kernel(gate_ref, xr_ref, conv0_ref, h0_ref, cw_ref, cb_ref, wax_ref, ba_ref, bx_ref, lam_ref,
                    y_ref, hout_ref, xc_s, hy_s, prev_s, h_s):
    i = pl.program_id(0)
    nseq, Tt, d = xr_ref.shape

    @pl.when(i == 0)
    def _():
        for k in range(CONV_WIDTH - 1):
            prev_s[k] = jnp.broadcast_to(conv0_ref[0, SUBLANES - CONV_WIDTH + 1 + k:SUBLANES - CONV_WIDTH + 2 + k, :],
                                         (nseq, d))
        h_s[...] = jnp.broadcast_to(h0_ref[0], (nseq, d))

    taps = [prev_s[k] for k in range(CONV_WIDTH - 1)]
    for t in range(Tt):
        x_t = xr_ref[:, t, :]
        c_t = cb_ref[...] + cw_ref[CONV_WIDTH - 1:CONV_WIDTH, :] * x_t
        for k in range(CONV_WIDTH - 1):
            c_t = c_t + cw_ref[k:k + 1, :] * taps[k]
        xc_s[t * nseq:(t + 1) * nseq, :] = c_t
        taps = taps[1:] + [x_t]
    for k in range(CONV_WIDTH - 1):
        prev_s[k] = taps[k]

    a, bt = _lru_gates(xc_s[...], wax_ref, ba_ref, bx_ref, lam_ref)
    h = h_s[...]
    for t in range(Tt):
        h = a[t * nseq:(t + 1) * nseq, :] * h + bt[t * nseq:(t + 1) * nseq, :]
        hy_s[:, t, :] = h
    h_s[...] = h
    y_ref[...] = (hy_s[...] * gate_ref[...]).astype(y_ref.dtype)

    @pl.when(i == pl.num_programs(0) - 1)
    def _():
        hout_ref[...] = h


def _lru_seq(gate, xr, conv0, h0, p, *, nseq, tt):
    M, d = xr.shape
    S = M // nseq
    bw = d // LRU_BLOCKS
    tt = min(tt, S)
    assert S % tt == 0 and nseq == SUBLANES and tt % SUBLANES == 0
    const = lambda i: (0, 0)
    blk = pl.BlockSpec((nseq, tt, d), lambda i: (0, i, 0))
    y, hout = pl.pallas_call(
        _lru_seq_kernel,
        grid=(S // tt,),
        in_specs=[blk, blk,
                  pl.BlockSpec((1, SUBLANES, d), lambda i: (0, 0, 0)),
                  pl.BlockSpec((1, 1, d), lambda i: (0, 0, 0)),
                  pl.BlockSpec((CONV_WIDTH, d), const),
                  pl.BlockSpec((1, d), const),
                  pl.BlockSpec((LRU_BLOCKS, bw, 2 * bw), lambda i: (0, 0, 0)),
                  pl.BlockSpec((1, d), const),
                  pl.BlockSpec((1, d), const),
                  pl.BlockSpec((1, d), const)],
        out_specs=[blk, pl.BlockSpec((nseq, d), const)],
        out_shape=[jax.ShapeDtypeStruct((nseq, S, d), BF16), jax.ShapeDtypeStruct((nseq, d), F32)],
        scratch_shapes=[pltpu.VMEM((tt * nseq, d), F32),
                        pltpu.VMEM((nseq, tt, d), F32),
                        pltpu.VMEM((CONV_WIDTH - 1, nseq, d), F32),
                        pltpu.VMEM((nseq, d), F32)],
        compiler_params=_cparams(("arbitrary",)),
        name="lru_seq",
    )(gate.reshape(nseq, S, d), xr.reshape(nseq, S, d), conv0, h0, p["conv_w"], p["conv_b"], p["wax"],
      p["b_a"], p["b_x"], p["lam"])
    return y.reshape(M, d), hout


def _lru_step_kernel(gate_ref, xr_ref, c0_ref, c1_ref, c2_ref, h0_ref, cw_ref, cb_ref, wax_ref, ba_ref, bx_ref,
                     lam_ref, y_ref, hout_ref):
    xr = _conv_step(xr_ref[...], (c0_ref[...], c1_ref[...], c2_ref[...]), cw_ref, cb_ref)
    a, bt = _lru_gates(xr, wax_ref, ba_ref, bx_ref, lam_ref)
    h = a * h0_ref[...] + bt
    hout_ref[...] = h
    y_ref[...] = (h * gate_ref[...]).astype(y_ref.dtype)


def _lru_step(gate, xr, conv_state, h0, p, *, nb):
    d = xr.shape[1]
    bw = d // LRU_BLOCKS
    const = lambda i: (0, 0)
    prev = lambda k: pl.BlockSpec((nb, d), lambda i: (0, k))
    conv_state = conv_state.reshape(nb, -1)
    return pl.pallas_call(
        _lru_step_kernel,
        grid=(1,),
        in_specs=[pl.BlockSpec((nb, d), const),
                  pl.BlockSpec((nb, d), const),
                  prev(0), prev(1), prev(2),
                  pl.BlockSpec((nb, d), const),
                  pl.BlockSpec((CONV_WIDTH, d), const),
                  pl.BlockSpec((1, d), const),
                  pl.BlockSpec((LRU_BLOCKS, bw, 2 * bw), lambda i: (0, 0, 0)),
                  pl.BlockSpec((1, d), const),
                  pl.BlockSpec((1, d), const),
                  pl.BlockSpec((1, d), const)],
        out_specs=[pl.BlockSpec((nb, d), const), pl.BlockSpec((nb, d), const)],
        out_shape=[jax.ShapeDtypeStruct((nb, d), BF16), jax.ShapeDtypeStruct((nb, d), F32)],
        compiler_params=_cparams(("arbitrary",)),
        name="lru_step",
    )(gate, xr, conv_state, conv_state, conv_state, h0, p["conv_w"], p["conv_b"], p["wax"], p["b_a"],
      p["b_x"], p["lam"])


def _row(v):
    return v.reshape(1, -1).astype(F32)


def _pad_lanes(v):
    return jnp.pad(v.astype(F32), (0, LANES - v.shape[0])).reshape(1, LANES)


def kernel(x_prompt, x_sample, state_ssd_conv, state_ssd_h, state_lru_conv, state_lru_h, meta_tokens, norm_mix_pre, norm_mix_post, norm_ffn_pre, norm_ffn_post, ssd_w_in, ssd_conv_w, ssd_conv_b, ssd_dt_bias, ssd_a_log, ssd_d, ssd_norm, ssd_w_out, lru_w_in, lru_b_in, lru_conv_w, lru_conv_b, lru_w_a, lru_b_a, lru_w_x, lru_b_x, lru_lambda, lru_w_out, lru_b_out, ffn_w1, ffn_w2):
    B, S, D = x_prompt.shape
    nb = x_sample.shape[0]
    n_meta = meta_tokens.shape[0]
    depth = norm_mix_pre.shape[0]
    T = CHUNK
    n_heads = ssd_dt_bias.shape[1]
    d_inner = n_heads * SSD_HEAD_DIM
    cdim = ssd_conv_w.shape[2]
    d_rnn = lru_conv_w.shape[2]
    gw = d_inner // SSD_GROUPS
    tm_p = min(PROJ_TM, S)
    assert x_sample.shape[1] == 1 and nb == T and S % tm_p == 0 and tm_p % T == 0 and B == SUBLANES
    assert SUBLANES <= n_meta <= T and n_meta % SUBLANES == 0
    nchunk = S // T
    meta_blk = nb // T

    xs = jnp.concatenate([x_sample[:, 0, :], meta_tokens, jnp.zeros((T - n_meta, D), F32)], axis=0)
    xp = x_prompt.reshape(B * S, D)
    xn_s = _norm_bf16(xs, _row(norm_mix_pre[0]), tm=PROJ_TM)
    xn_p = _norm_bf16(xp, _row(norm_mix_pre[0]), tm=PROJ_TM)

    small = dict(tm=T, n_raw=meta_blk, n_valid=n_meta,
                 conv=dict(tm=T, n_raw=meta_blk, tiles_per_seq=1, tail_lo=n_meta - SUBLANES))
    prompt = dict(tm=tm_p, n_raw=0, n_valid=tm_p,
                  conv=dict(tm=tm_p, n_raw=0, tiles_per_seq=S // tm_p, tail_lo=tm_p - SUBLANES))
    last3 = lambda tail: tail.reshape(B, S // tm_p, SUBLANES, -1)[:, -1, SUBLANES - (CONV_WIDTH - 1):, :]

    p_ssd_conv, p_ssd_h, p_lru_conv, p_lru_h = [], [], [], []
    s_ssd_conv, s_lru_conv, s_lru_h = [], [], []
    s_ssd_h = None
    for i in range(depth):
        j = i // 2
        if i % 2 == 0:
            w_dt = jnp.pad(ssd_w_in[j][:, d_inner + cdim:], ((0, 0), (0, LANES - n_heads)))
            p = dict(conv_w=ssd_conv_w[j], conv_b=_row(ssd_conv_b[j]), dt_bias=_pad_lanes(ssd_dt_bias[j]),
                     a_log=_pad_lanes(ssd_a_log[j]), dexp=_row(jnp.repeat(ssd_d[j], SSD_HEAD_DIM)),
                     norm=_row(ssd_norm[j]))
            w_out, b_out = ssd_w_out[j].astype(BF16), jnp.zeros((1, D), F32)

            def run_proj(xn, conv0, v):
                xbc, tail = _proj(xn, ssd_w_in, j, d_inner, cdim, "conv_silu", tn=PROJ_TN,
                                  conv=(p["conv_w"], p["conv_b"], conv0), **v["conv"])
                zs = _proj(xn, ssd_w_in, j, 0, d_inner, "silu", tm=v["tm"], tn=PROJ_TN)
                dtq = _dt_proj(xn, w_dt, p["dt_bias"], p["a_log"], tm=v["tm"], n_raw=v["n_raw"],
                               n_valid=v["n_valid"])
                return xbc, tail, zs, dtq

            xbc_s, tail_s, zs_s, dtq_s = run_proj(xn_s, jnp.zeros((1, SUBLANES, cdim), F32), small)
            y_samp, s_ssd_h = _ssd_step(xbc_s, zs_s, dtq_s[0], state_ssd_conv[j],
                                        state_ssd_h.reshape(-1, SSD_GROUPS, gw, SSD_STATE), s_ssd_h, p,
                                        nb=nb, layer=j)
            y_meta, h_meta = _ssd_chunk(xbc_s, zs_s, dtq_s, jnp.zeros((1, SSD_GROUPS, gw, SSD_STATE), F32), p,
                                        nseq=1, nchunk=1, row_block0=meta_blk)
            conv_meta = tail_s[meta_blk:meta_blk + 1]
            s_ssd_conv.append(jnp.concatenate([state_ssd_conv[j][:, 1:], xbc_s[:nb, None, :]], axis=1))
            y_s = jnp.concatenate([y_samp, y_meta], axis=0)

            xbc_p, tail_p, zs_p, dtq_p = run_proj(xn_p, conv_meta, prompt)
            y_p, h_p = _ssd_chunk(xbc_p, zs_p, dtq_p, h_meta, p, nseq=B, nchunk=nchunk)
            p_ssd_conv.append(last3(tail_p))
            p_ssd_h.append(h_p.reshape(B, n_heads, SSD_HEAD_DIM, SSD_STATE))
        else:
            p = dict(conv_w=lru_conv_w[j], conv_b=_row(lru_conv_b[j]),
                     wax=jnp.concatenate([lru_w_a[j], lru_w_x[j]], axis=-1).astype(BF16),
                     b_a=_row(lru_b_a[j]), b_x=_row(lru_b_x[j]), lam=_row(lru_lambda[j]))
            w_out, b_out = lru_w_out[j].astype(BF16), _row(lru_b_out[j])
            b_in = _row(lru_b_in[j])

            def run_proj(xn, conv0, v):
                gate = _proj(xn, lru_w_in, j, 0, d_rnn, "gelu", tm=v["tm"], tn=PROJ_TN, bias=b_in)
                xr, tail = _proj(xn, lru_w_in, j, d_rnn, d_rnn, "conv", tn=PROJ_TN, bias=b_in,
                                 conv=(p["conv_w"], p["conv_b"], conv0), **v["conv"])
                return gate, xr, tail

            gate_s, xr_s, tail_s = run_proj(xn_s, jnp.zeros((1, SUBLANES, d_rnn), F32), small)
            y_samp, h_samp = _lru_step(gate_s, xr_s, state_lru_conv[j], state_lru_h[j], p, nb=nb)
            y_meta, h_meta = _lru_chunk(gate_s, xr_s, jnp.zeros((1, 1, d_rnn), F32), p,
                                        nseq=1, nchunk=1, row_block0=meta_blk, n_valid=n_meta)
            conv_meta = tail_s[meta_blk:meta_blk + 1]
            s_lru_conv.append(jnp.concatenate([state_lru_conv[j][:, 1:], xr_s[:nb, None, :]], axis=1))
            s_lru_h.append(h_samp)
            y_s = jnp.concatenate([y_samp, y_meta], axis=0)

            gate_p = _proj(xn_p, lru_w_in, j, 0, d_rnn, "gelu", tm=tm_p, tn=PROJ_TN, bias=b_in)
            xr_p = _proj(xn_p, lru_w_in, j, d_rnn, d_rnn, "none", tm=tm_p, tn=PROJ_TN, bias=b_in)
            y_p, h_p = _lru_seq(gate_p, xr_p, conv_meta, h_meta, p, nseq=B, tt=LRU_TT)
            p_lru_conv.append(xr_p.reshape(B, S, d_rnn)[:, S - (CONV_WIDTH - 1):, :])
            p_lru_h.append(h_p)

        g_next = _row(norm_mix_pre[i + 1]) if i + 1 < depth else None
        ffn = (w_out, b_out, _row(norm_mix_post[i]), _row(norm_ffn_pre[i]), ffn_w1[i].astype(BF16),
               ffn_w2[i].astype(BF16), _row(norm_ffn_post[i]), g_next)
        xs, xn_s = _out_ffn(y_s, xs, *ffn, tm=FFN_TM)
        xp, xn_p = _out_ffn(y_p, xp, *ffn, tm=FFN_TM)

    return (xp.reshape(B, S, D), xs[:nb].reshape(nb, 1, D),
            jnp.stack(p_ssd_conv), jnp.stack(p_ssd_h), jnp.stack(p_lru_conv), jnp.stack(p_lru_h),
            jnp.stack(s_ssd_conv), s_ssd_h.reshape(state_ssd_h.shape), jnp.stack(s_lru_conv), jnp.stack(s_lru_h))
```

```python
import functools
import math

import jax
import jax.numpy as jnp
from jax import lax
from jax.experimental import pallas as pl
from jax.experimental.pallas import tpu as pltpu

F32 = jnp.float32
BF16 = jnp.bfloat16

EPS = 1e-6
CONV_WIDTH = 4
SSD_HEAD_DIM = 64
SSD_GROUPS = 8
SSD_STATE = 128
LRU_BLOCKS = 8
LRU_C = 8.0
CHUNK = 128
LANES = 128
SUBLANES = 8
MXU_COLS = 256
VMEM_LIMIT = 56 * 1024 * 1024
PROJ_TM = 1024
PROJ_TN = 1024
FFN_TM = 512
LRU_TT = 64


def _cparams(sem):
    return pltpu.CompilerParams(dimension_semantics=sem, vmem_limit_bytes=VMEM_LIMIT)


def _rms(x, g):
    return x * lax.rsqrt(jnp.mean(x * x, axis=-1, keepdims=True) + EPS) * g


def _softplus(x):
    return jnp.maximum(x, 0.0) + jnp.log1p(jnp.exp(-jnp.abs(x)))


def _silu(x):
    return x * jax.nn.sigmoid(x)


def _gelu_tanh(x):
    return 0.5 * x * (1.0 + jnp.tanh(math.sqrt(2.0 / math.pi) * (x + 0.044715 * (x * x * x))))


def _dot(a, b):
    return jnp.dot(a, b, preferred_element_type=F32)


def _dot_nt(a, b):
    return lax.dot_general(a, b, (((1,), (1,)), ((), ())), preferred_element_type=F32)


def _pair_expand(q, j, lane):
    return jnp.where(lane < SSD_HEAD_DIM, q[:, 2 * j:2 * j + 1], q[:, 2 * j + 1:2 * j + 2])


def _conv_rows(x, prev8, cw, cb):
    row8 = lax.broadcasted_iota(jnp.int32, prev8.shape, 0)
    out = cb + cw[CONV_WIDTH - 1:CONV_WIDTH, :] * x
    for k in range(1, CONV_WIDTH):
        r = pltpu.roll(x, k, 0)
        head = jnp.where(row8 < k, pltpu.roll(prev8, k, 0), r[0:SUBLANES, :])
        shifted = jnp.concatenate([head, r[SUBLANES:, :]], axis=0)
        out = out + cw[CONV_WIDTH - 1 - k:CONV_WIDTH - k, :] * shifted
    return out


def _conv_step(x, prevs, cw_ref, cb_ref):
    out = cb_ref[...] + cw_ref[CONV_WIDTH - 1:CONV_WIDTH, :] * x
    for k in range(CONV_WIDTH - 1):
        out = out + cw_ref[k:k + 1, :] * prevs[k]
    return out


def _ssd_gate_norm(y, zs, ng_ref):
    y = y * zs
    gw = y.shape[1] // SSD_GROUPS
    parts = []
    for g in range(SSD_GROUPS):
        seg = y[:, g * gw:(g + 1) * gw]
        parts.append(seg * lax.rsqrt(jnp.mean(seg * seg, axis=-1, keepdims=True) + EPS))
    return jnp.concatenate(parts, axis=1) * ng_ref[...]


def _lru_gates(xr, wax_ref, ba_ref, bx_ref, lam_ref):
    bw = xr.shape[1] // LRU_BLOCKS
    ra, ix = [], []
    for k in range(LRU_BLOCKS):
        g = _dot(xr[:, k * bw:(k + 1) * bw].astype(BF16), wax_ref[k])
        ra.append(g[:, :bw])
        ix.append(g[:, bw:])
    r = jax.nn.sigmoid(jnp.concatenate(ra, axis=1) + ba_ref[...])
    i = jax.nn.sigmoid(jnp.concatenate(ix, axis=1) + bx_ref[...])
    log_a = (-LRU_C) * r * _softplus(-lam_ref[...])
    a = jnp.exp(log_a)
    v = -jnp.tanh(log_a) * (a * a + 1.0)
    mult = jnp.where(v > 0.0, v * lax.rsqrt(v), 0.0)
    return a, mult * i * xr


def _norm_kernel(x_ref, g_ref, o_ref):
    o_ref[...] = _rms(x_ref[...], g_ref[...]).astype(o_ref.dtype)


def _norm_bf16(x, g, *, tm):
    M, D = x.shape
    tm = min(tm, M)
    assert M % tm == 0
    return pl.pallas_call(
        _norm_kernel,
        grid=(M // tm,),
        in_specs=[pl.BlockSpec((tm, D), lambda i: (i, 0)), pl.BlockSpec((1, D), lambda i: (0, 0))],
        out_specs=pl.BlockSpec((tm, D), lambda i: (i, 0)),
        out_shape=jax.ShapeDtypeStruct((M, D), BF16),
        compiler_params=_cparams(("parallel",)),
        name="norm",
    )(x, g)


def _proj_kernel(*refs, mode, has_bias, n_raw, tiles_per_seq, tail_lo):
    has_conv = mode in ("conv", "conv_silu")
    it = iter(refs)
    xn_ref, w_ref = next(it), next(it)
    b_ref = next(it) if has_bias else None
    cw_ref, cb_ref, conv0_ref = (next(it), next(it), next(it)) if has_conv else (None, None, None)
    o_ref = next(it)
    tail_ref = next(it) if has_conv else None
    wb_s = next(it)
    carry_s = next(it) if has_conv else None
    i = pl.program_id(1)

    @pl.when(i == 0)
    def _():
        wb_s[...] = w_ref[...].astype(BF16)

    tm, tn = o_ref.shape
    for n in range(tn // MXU_COLS):
        sl = pl.ds(n * MXU_COLS, MXU_COLS)
        acc = _dot(xn_ref[...], wb_s[:, sl])
        if has_bias:
            acc = acc + b_ref[:, sl]
        if not has_conv:
            o_ref[:, sl] = acc
            continue
        tail_ref[0, :, sl] = acc[tail_lo:tail_lo + SUBLANES, :]

        def conv_tile(acc=acc, sl=sl):
            start = lax.rem(i - n_raw, tiles_per_seq) == 0
            prev8 = jnp.where(start, conv0_ref[0, :, sl], carry_s[:, sl])
            out = _conv_rows(acc, prev8, cw_ref[:, sl], cb_ref[:, sl])
            carry_s[:, sl] = acc[tm - SUBLANES:tm, :]
            o_ref[:, sl] = _silu(out) if mode == "conv_silu" else out

        if n_raw == 0:
            conv_tile()
        else:
            @pl.when(i < n_raw)
            def _(acc=acc, sl=sl):
                o_ref[:, sl] = acc

            pl.when(i >= n_raw)(conv_tile)


def _proj(xn, w, layer, col0, n_out, mode, *, tm, tn, bias=None, conv=None, n_raw=0, tiles_per_seq=1,
          tail_lo=0):
    M, D = xn.shape
    tm = min(tm, M)
    assert M % tm == 0 and n_out % tn == 0 and col0 % tn == 0 and tn % MXU_COLS == 0
    assert (conv is not None) == (mode in ("conv", "conv_silu"))
    n_i, n_j, jb = M // tm, n_out // tn, col0 // tn
    in_specs = [pl.BlockSpec((tm, D), lambda j, i: (i, 0)),
                pl.BlockSpec((D, tn), lambda j, i: (layer, jb + j))]
    args = [xn, w.reshape(-1, w.shape[-1])]
    if bias is not None:
        in_specs.append(pl.BlockSpec((1, tn), lambda j, i: (0, jb + j)))
        args.append(bias)
    out_shape = [jax.ShapeDtypeStruct((M, n_out), F32)]
    out_specs = [pl.BlockSpec((tm, tn), lambda j, i: (i, j))]
    scratch = [pltpu.VMEM((D, tn), BF16)]
    if conv is not None:
        cw, cb, conv0 = conv
        in_specs += [pl.BlockSpec((CONV_WIDTH, tn), lambda j, i: (0, j)),
                     pl.BlockSpec((1, tn), lambda j, i: (0, j)),
                     pl.BlockSpec((1, SUBLANES, tn), lambda j, i: (0, 0, j))]
        args += [cw, cb, conv0]
        out_shape.append(jax.ShapeDtypeStruct((n_i, SUBLANES, n_out), F32))
        out_specs.append(pl.BlockSpec((1, SUBLANES, tn), lambda j, i: (i, 0, j)))
        scratch.append(pltpu.VMEM((SUBLANES, tn), F32))
    res = pl.pallas_call(
        functools.partial(_proj_kernel, mode=mode, has_bias=bias is not None, n_raw=n_raw,
                          tiles_per_seq=tiles_per_seq, tail_lo=tail_lo),
        grid=(n_j, n_i),
        in_specs=in_specs, out_specs=out_specs, out_shape=out_shape, scratch_shapes=scratch,
        compiler_params=_cparams(("arbitrary", "arbitrary")),
        name="proj_" + mode,
    )(*args)
    return res if conv is not None else res[0]


def _dt_kernel(xn_ref, w_ref, dtb_ref, alog_ref, dtv_ref, cs_ref, csT_ref, dtT_ref, *, n_raw, n_valid):
    i = pl.program_id(0)
    tm = xn_ref.shape[0]
    T = CHUNK
    dtv = _softplus(_dot(xn_ref[...], w_ref[...].astype(BF16)) + dtb_ref[...])
    if n_valid < tm:
        keep = jnp.logical_or(lax.broadcasted_iota(jnp.int32, dtv.shape, 0) < n_valid, i < n_raw)
        dtv = jnp.where(keep, dtv, 0.0)
    dtv_ref[...] = dtv
    a = dtv * (-jnp.exp(alog_ref[...]))
    tri = (lax.broadcasted_iota(jnp.int32, (T, T), 0) >= lax.broadcasted_iota(jnp.int32, (T, T), 1)).astype(F32)
    for r in range(tm // T):
        cs = jnp.dot(tri, a[r * T:(r + 1) * T, :], precision=lax.Precision.HIGHEST, preferred_element_type=F32)
        cs_ref[r * T:(r + 1) * T, :] = cs
        csT_ref[r] = cs.T
        dtT_ref[r] = dtv[r * T:(r + 1) * T, :].T


def _dt_proj(xn, w_dt, dtb, alog, *, tm, n_raw, n_valid):
    M, D = xn.shape
    T = CHUNK
    tm = min(tm, M)
    assert M % tm == 0 and tm % T == 0 and T == LANES and (n_valid >= tm or tm == T)
    const = lambda i: (0, 0)
    rows = pl.BlockSpec((tm, LANES), lambda i: (i, 0))
    rowsT = pl.BlockSpec((tm // T, LANES, T), lambda i: (i, 0, 0))
    return pl.pallas_call(
        functools.partial(_dt_kernel, n_raw=n_raw, n_valid=n_valid),
        grid=(M // tm,),
        in_specs=[pl.BlockSpec((tm, D), lambda i: (i, 0)),
                  pl.BlockSpec((D, LANES), const),
                  pl.BlockSpec((1, LANES), const),
                  pl.BlockSpec((1, LANES), const)],
        out_specs=[rows, rows, rowsT, rowsT],
        out_shape=[jax.ShapeDtypeStruct((M, LANES), F32), jax.ShapeDtypeStruct((M, LANES), F32),
                   jax.ShapeDtypeStruct((M // T, LANES, T), F32), jax.ShapeDtypeStruct((M // T, LANES, T), F32)],
        compiler_params=_cparams(("parallel",)),
        name="dt_proj",
    )(xn, w_dt, dtb, alog)


def _out_ffn_kernel(*refs, fc, mix, has_next):
    it = iter(refs)
    u_ref, g_ref = next(it), next(it)
    ng_ref = next(it) if mix == "ssd" else None
    x_ref, wo_ref, bo_ref, gpost_ref, gpre_ref, w1_ref, w2_ref, gfpost_ref = (next(it) for _ in range(8))
    gnext_ref = next(it) if has_next else None
    o_ref = next(it)
    xn_ref = next(it) if has_next else None
    if mix == "ssd":
        y = _ssd_gate_norm(u_ref[...], _silu(g_ref[...]), ng_ref)
    else:
        y = u_ref[...] * _gelu_tanh(g_ref[...])
    m = _dot(y.astype(BF16), wo_ref[...]) + bo_ref[...]
    x1 = x_ref[...] + _rms(m, gpost_ref[...])
    hn = _rms(x1, gpre_ref[...]).astype(BF16)
    dff = w1_ref.shape[1]
    acc = None
    for c in range(dff // fc):
        h = jnp.maximum(_dot(hn, w1_ref[:, c * fc:(c + 1) * fc]), 0.0)
        part = _dot((h * h).astype(BF16), w2_ref[c * fc:(c + 1) * fc, :])
        acc = part if acc is None else acc + part
    x2 = x1 + _rms(acc, gfpost_ref[...])
    o_ref[...] = x2
    if has_next:
        xn_ref[...] = _rms(x2, gnext_ref[...]).astype(xn_ref.dtype)


def _out_ffn(u, g, ng, x, layer, wo, wo_layer, bo, gpost, gpre, w1, w2, gfpost, gnext, *, mix, tm, fc=1024):
    M, D = x.shape
    K = u.shape[1]
    dff = w1.shape[2]
    tm = min(tm, M)
    assert M % tm == 0 and dff % fc == 0
    const = lambda i: (0, 0)
    once = dict(pipeline_mode=pl.Buffered(1))
    rows = lambda w: pl.BlockSpec((tm, w), lambda i: (i, 0))
    has_next = gnext is not None
    in_specs = [rows(K), rows(K)]
    args = [u, g]
    if mix == "ssd":
        in_specs.append(pl.BlockSpec((1, K), const))
        args.append(ng)
    in_specs += [rows(D),
                 pl.BlockSpec((None, K, D), lambda i: (wo_layer, 0, 0), **once),
                 pl.BlockSpec((1, D), const),
                 pl.BlockSpec((1, D), const),
                 pl.BlockSpec((1, D), const),
                 pl.BlockSpec((None, D, dff), lambda i: (layer, 0, 0), **once),
                 pl.BlockSpec((None, dff, D), lambda i: (layer, 0, 0), **once),
                 pl.BlockSpec((1, D), const)]
    args += [x, wo, bo, gpost, gpre, w1, w2, gfpost]
    out_specs = [rows(D)]
    out_shape = [jax.ShapeDtypeStruct((M, D), F32)]
    if has_next:
        in_specs.append(pl.BlockSpec((1, D), const))
        args.append(gnext)
        out_specs.append(rows(D))
        out_shape.append(jax.ShapeDtypeStruct((M, D), BF16))
    res = pl.pallas_call(
        functools.partial(_out_ffn_kernel, fc=fc, mix=mix, has_next=has_next),
        grid=(M // tm,),
        in_specs=in_specs, out_specs=out_specs, out_shape=out_shape,
        compiler_params=_cparams(("parallel",)),
        name="out_ffn_" + mix,
    )(*args)
    return (res[0], res[1]) if has_next else (res[0], None)


def _ssd_chunk_kernel(xbc_ref, cs_ref, csT_ref, dtT_ref, h0_ref, dexp_ref, y_ref, hout_ref, hT_s, *, T):
    c = pl.program_id(1)
    d_inner = y_ref.shape[1]
    gs = SSD_GROUPS * SSD_STATE
    gw = d_inner // SSD_GROUPS

    @pl.when(c == 0)
    def _():
        for g in range(SSD_GROUPS):
            hT_s[g] = h0_ref[0, g].T

    row = lax.broadcasted_iota(jnp.int32, (T, T), 0)
    col = lax.broadcasted_iota(jnp.int32, (T, T), 1)
    causal = row >= col
    lane = lax.broadcasted_iota(jnp.int32, (T, LANES), 1)
    lo_half = lane < SSD_HEAD_DIM

    cs = cs_ref[...]
    csT = csT_ref[0]
    dtT = dtT_ref[0]
    wdT = dtT * jnp.exp(csT[:, T - 1:T] - csT)
    dec_tot = jnp.exp(cs[T - 1:T, :])

    ys = []
    for g in range(SSD_GROUPS):
        Bg = xbc_ref[:, d_inner + g * SSD_STATE:d_inner + (g + 1) * SSD_STATE]
        Cg = xbc_ref[:, d_inner + gs + g * SSD_STATE:d_inner + gs + (g + 1) * SSD_STATE].astype(BF16)
        BgT = Bg.T
        cb = _dot(Cg, BgT.astype(BF16))
        hTg = hT_s[g]
        yoff = _dot(Cg, hTg.astype(BF16))
        for jj in range(gw // LANES):
            j = g * (gw // LANES) + jj
            xpair = xbc_ref[:, j * LANES:(j + 1) * LANES]
            s_parts, bw_parts, ecs_parts = [], [], []
            for hh in (2 * j, 2 * j + 1):
                csl = jnp.broadcast_to(cs[:, hh:hh + 1], (T, T))
                L = jnp.where(causal, jnp.exp(csl - csT[hh:hh + 1, :]), 0.0)
                s_parts.append((cb * L * dtT[hh:hh + 1, :]).astype(BF16))
                bw_parts.append((BgT * wdT[hh:hh + 1, :]).astype(BF16))
                ecs_parts.append(jnp.exp(csl))
            x2 = jnp.concatenate([jnp.where(lo_half, xpair, 0.0).astype(BF16),
                                  jnp.where(lo_half, 0.0, xpair).astype(BF16)], axis=0)
            ydiag = _dot(jnp.concatenate(s_parts, axis=1), x2)
            upd = _dot(jnp.concatenate(bw_parts, axis=1), x2)
            ecs = jnp.where(lo_half, ecs_parts[0], ecs_parts[1])
            ys.append(ydiag + yoff[:, jj * LANES:(jj + 1) * LANES] * ecs
                      + xpair * dexp_ref[:, j * LANES:(j + 1) * LANES])
            dect = _pair_expand(dec_tot, j, lane[0:1, :])
            hT_s[g, :, jj * LANES:(jj + 1) * LANES] = hTg[:, jj * LANES:(jj + 1) * LANES] * dect + upd

    y_ref[...] = jnp.concatenate(ys, axis=1)

    @pl.when(c == pl.num_programs(1) - 1)
    def _():
        for g in range(SSD_GROUPS):
            hout_ref[0, g] = hT_s[g].T


def _ssd_chunk(xbc, dtq, h0, p, *, nseq, nchunk, row_block0=0):
    T = CHUNK
    assert T == LANES
    d_inner = p["dexp"].shape[1]
    cdim = p["conv_w"].shape[1]
    gw = d_inner // SSD_GROUPS
    _, cs, csT, dtT = dtq
    rows = lambda b, c: (row_block0 + b * nchunk + c, 0)
    rows3 = lambda b, c: (row_block0 + b * nchunk + c, 0, 0)
    shared4 = (lambda b, c: (0, 0, 0, 0)) if h0.shape[0] == 1 else (lambda b, c: (b, 0, 0, 0))
    const = lambda b, c: (0, 0)
    return pl.pallas_call(
        functools.partial(_ssd_chunk_kernel, T=T),
        grid=(nseq, nchunk),
        in_specs=[pl.BlockSpec((T, cdim), rows),
                  pl.BlockSpec((T, LANES), rows),
                  pl.BlockSpec((1, LANES, T), rows3),
                  pl.BlockSpec((1, LANES, T), rows3),
                  pl.BlockSpec((1, SSD_GROUPS, gw, SSD_STATE), shared4),
                  pl.BlockSpec((1, d_inner), const)],
        out_specs=[pl.BlockSpec((T, d_inner), lambda b, c: (b * nchunk + c, 0)),
                   pl.BlockSpec((1, SSD_GROUPS, gw, SSD_STATE), lambda b, c: (b, 0, 0, 0))],
        out_shape=[jax.ShapeDtypeStruct((nseq * nchunk * T, d_inner), F32),
                   jax.ShapeDtypeStruct((nseq, SSD_GROUPS, gw, SSD_STATE), F32)],
        scratch_shapes=[pltpu.VMEM((SSD_GROUPS, SSD_STATE, gw), F32)],
        compiler_params=_cparams(("parallel", "arbitrary")),
        name="ssd_chunk",
    )(xbc, cs, csT, dtT, h0, p["dexp"])


def _ssd_step_kernel(*refs, seq_per_step, has_prev):
    xbc_ref, dt_ref, c0_ref, c1_ref, c2_ref, h_ref, cw_ref, cb_ref, alog_ref, dexp_ref = refs[:10]
    y_ref, hout_ref, xdtT_s, decT_s, b_s, c_s, xs_s, yrow_s = refs[10 + has_prev:]
    s = pl.program_id(0)
    nb, d_inner = xs_s.shape
    gs = SSD_GROUPS * SSD_STATE
    gw = d_inner // SSD_GROUPS

    @pl.when(s == 0)
    def _():
        xbc = _silu(_conv_step(xbc_ref[...], (c0_ref[...], c1_ref[...], c2_ref[...]), cw_ref, cb_ref))
        xs = xbc[:, :d_inner]
        xs_s[...] = xs
        b_s[...] = xbc[:, d_inner:d_inner + gs]
        c_s[...] = xbc[:, d_inner + gs:]
        dtv = dt_ref[...]
        decT_s[...] = jnp.exp(dtv * (-jnp.exp(alog_ref[...]))).T
        dtT = dtv.T
        for j in range(d_inner // LANES):
            xT = xs[:, j * LANES:(j + 1) * LANES].T
            for k in range(2):
                hh = 2 * j + k
                xdtT_s[hh * SSD_HEAD_DIM:(hh + 1) * SSD_HEAD_DIM, :] = (
                    xT[k * SSD_HEAD_DIM:(k + 1) * SSD_HEAD_DIM, :] * dtT[hh:hh + 1, :])

    for q in range(seq_per_step):
        b = s * seq_per_step + q
        sel = lax.broadcasted_iota(jnp.int32, (1, nb), 1) == b
        dec_col = jnp.sum(jnp.where(sel, decT_s[...], 0.0), axis=1, keepdims=True)
        brow = b_s[pl.ds(b, 1), :]
        crow = c_s[pl.ds(b, 1), :]
        for g in range(SSD_GROUPS):
            xcol = jnp.sum(jnp.where(sel, xdtT_s[g * gw:(g + 1) * gw, :], 0.0), axis=1, keepdims=True)
            dcol = jnp.concatenate(
                [jnp.broadcast_to(dec_col[hh:hh + 1, :], (SSD_HEAD_DIM, 1))
                 for hh in range(g * (gw // SSD_HEAD_DIM), (g + 1) * (gw // SSD_HEAD_DIM))], axis=0)
            hn = h_ref[q, g] * dcol + xcol * brow[:, g * SSD_STATE:(g + 1) * SSD_STATE]
            hout_ref[q, g] = hn
            c8 = jnp.broadcast_to(crow[:, g * SSD_STATE:(g + 1) * SSD_STATE], (SUBLANES, SSD_STATE)).astype(BF16)
            yg = _dot_nt(c8, hn.astype(BF16))
            yrow_s[pl.ds(b, 1), g * gw:(g + 1) * gw] = yg[0:1, :]

    @pl.when(s == pl.num_programs(0) - 1)
    def _():
        y_ref[...] = yrow_s[...] + xs_s[...] * dexp_ref[...]


def _ssd_step(xbc, dt, conv_state, h_all, h_out_prev, p, *, nb, layer, seq_per_step=4):
    d_inner = p["dexp"].shape[1]
    cdim = p["conv_w"].shape[1]
    gw = d_inner // SSD_GROUPS
    gs = SSD_GROUPS * SSD_STATE
    assert nb % seq_per_step == 0
    blk0 = layer * (nb // seq_per_step)
    const = lambda s: (0, 0)
    prev = lambda k: pl.BlockSpec((nb, cdim), lambda s: (0, k))
    conv_state = conv_state.reshape(nb, -1)
    hspec = pl.BlockSpec((seq_per_step, SSD_GROUPS, gw, SSD_STATE), lambda s: (blk0 + s, 0, 0, 0))
    has_prev = h_out_prev is not None
    in_specs = [pl.BlockSpec((nb, cdim), const),
                pl.BlockSpec((nb, LANES), const),
                prev(0), prev(1), prev(2),
                hspec,
                pl.BlockSpec((CONV_WIDTH, cdim), const),
                pl.BlockSpec((1, cdim), const),
                pl.BlockSpec((1, LANES), const),
                pl.BlockSpec((1, d_inner), const)]
    args = [xbc, dt, conv_state, conv_state, conv_state, h_all, p["conv_w"], p["conv_b"], p["a_log"], p["dexp"]]
    if has_prev:
        in_specs.append(pl.BlockSpec(memory_space=pl.ANY))
        args.append(h_out_prev)
    return pl.pallas_call(
        functools.partial(_ssd_step_kernel, seq_per_step=seq_per_step, has_prev=has_prev),
        grid=(nb // seq_per_step,),
        in_specs=in_specs,
        out_specs=[pl.BlockSpec((nb, d_inner), const), hspec],
        out_shape=[jax.ShapeDtypeStruct((nb, d_inner), F32),
                   jax.ShapeDtypeStruct(h_all.shape, F32)],
        scratch_shapes=[pltpu.VMEM((d_inner, nb), F32),
                        pltpu.VMEM((LANES, nb), F32),
                        pltpu.VMEM((nb, gs), F32),
                        pltpu.VMEM((nb, gs), F32),
                        pltpu.VMEM((nb, d_inner), F32),
                        pltpu.VMEM((nb, d_inner), F32)],
        input_output_aliases={len(args) - 1: 1} if has_prev else {},
        compiler_params=_cparams(("arbitrary",)),
        name="ssd_step",
    )(*args)


def _shift_rows(x, s, fill, row):
    if s % SUBLANES == 0:
        return jnp.concatenate([jnp.full((s, x.shape[1]), fill, x.dtype), x[:x.shape[0] - s, :]], axis=0)
    return jnp.where(row >= s, pltpu.roll(x, s, 0), fill)


def _lru_chunk_kernel(xr_ref, h0_ref, wax_ref, ba_ref, bx_ref, lam_ref, y_ref, hout_ref, hc_s, *, T, n_valid):
    c = pl.program_id(1)

    @pl.when(c == 0)
    def _():
        hc_s[...] = h0_ref[0]

    a, bt = _lru_gates(xr_ref[...], wax_ref, ba_ref, bx_ref, lam_ref)
    row = lax.broadcasted_iota(jnp.int32, a.shape, 0)
    if n_valid < T:
        a = jnp.where(row < n_valid, a, 1.0)
        bt = jnp.where(row < n_valid, bt, 0.0)
    s = 1
    while s < T:
        bt = a * _shift_rows(bt, s, 0.0, row) + bt
        a = a * _shift_rows(a, s, 1.0, row)
        s *= 2
    h = a * hc_s[...] + bt
    hc_s[...] = h[T - 1:T, :]
    y_ref[...] = h

    @pl.when(c == pl.num_programs(1) - 1)
    def _():
        hout_ref[0] = h[T - 1:T, :]


def _lru_chunk(xr, h0, p, *, nseq, nchunk, row_block0, n_valid):
    T = CHUNK
    d = xr.shape[1]
    bw = d // LRU_BLOCKS
    rows = lambda b, c: (row_block0 + b * nchunk + c, 0)
    shared_h = (lambda b, c: (0, 0, 0)) if h0.shape[0] == 1 else (lambda b, c: (b, 0, 0))
    const = lambda b, c: (0, 0)
    return pl.pallas_call(
        functools.partial(_lru_chunk_kernel, T=T, n_valid=n_valid),
        grid=(nseq, nchunk),
        in_specs=[pl.BlockSpec((T, d), rows),
                  pl.BlockSpec((1, 1, d), shared_h),
                  pl.BlockSpec((LRU_BLOCKS, bw, 2 * bw), lambda b, c: (0, 0, 0)),
                  pl.BlockSpec((1, d), const),
                  pl.BlockSpec((1, d), const),
                  pl.BlockSpec((1, d), const)],
        out_specs=[pl.BlockSpec((T, d), lambda b, c: (b * nchunk + c, 0)),
                   pl.BlockSpec((1, 1, d), lambda b, c: (b, 0, 0))],
        out_shape=[jax.ShapeDtypeStruct((nseq * nchunk * T, d), F32),
                   jax.ShapeDtypeStruct((nseq, 1, d), F32)],
        scratch_shapes=[pltpu.VMEM((1, d), F32)],
        compiler_params=_cparams(("parallel", "arbitrary")),
        name="lru_chunk",
    )(xr, h0, p["wax"], p["b_a"], p["b_x"], p["lam"])


def _lru_seq_kernel(xr_ref, conv0_ref, h0_ref, cw_ref, cb_ref, wax_ref, ba_ref, bx_ref, lam_ref,
                    y_ref, hout_ref, xc_s, prev_s, h_s):
    i = pl.program_id(0)
    nseq, Tt, d = xr_ref.shape

    @pl.when(i == 0)
    def _():
        for k in range(CONV_WIDTH - 1):
            prev_s[k] = jnp.broadcast_to(conv0_ref[0, SUBLANES - CONV_WIDTH + 1 + k:SUBLANES - CONV_WIDTH + 2 + k, :],
                                         (nseq, d))
        h_s[...] = jnp.broadcast_to(h0_ref[0], (nseq, d))

    taps = [prev_s[k] for k in range(CONV_WIDTH - 1)]
    for t in range(Tt):
        x_t = xr_ref[:, t, :]
        c_t = cb_ref[...] + cw_ref[CONV_WIDTH - 1:CONV_WIDTH, :] * x_t
        for k in range(CONV_WIDTH - 1):
            c_t = c_t + cw_ref[k:k + 1, :] * taps[k]
        xc_s[t * nseq:(t + 1) * nseq, :] = c_t
        taps = taps[1:] + [x_t]
    for k in range(CONV_WIDTH - 1):
        prev_s[k] = taps[k]

    a, bt = _lru_gates(xc_s[...], wax_ref, ba_ref, bx_ref, lam_ref)
    h = h_s[...]
    for t in range(Tt):
        h = a[t * nseq:(t + 1) * nseq, :] * h + bt[t * nseq:(t + 1) * nseq, :]
        y_ref[:, t, :] = h
    h_s[...] = h

    @pl.when(i == pl.num_programs(0) - 1)
    def _():
        hout_ref[...] = h


def _lru_seq(gx, conv0, h0, p, *, nseq, tt):
    M, d2 = gx.shape
    d = d2 // 2
    S = M // nseq
    bw = d // LRU_BLOCKS
    tt = min(tt, S)
    assert S % tt == 0 and nseq == SUBLANES and tt % SUBLANES == 0
    const = lambda i: (0, 0)
    y, hout = pl.pallas_call(
        _lru_seq_kernel,
        grid=(S // tt,),
        in_specs=[pl.BlockSpec((nseq, tt, d), lambda i: (0, i, 1)),
                  pl.BlockSpec((1, SUBLANES, d), lambda i: (0, 0, 0)),
                  pl.BlockSpec((1, 1, d), lambda i: (0, 0, 0)),
                  pl.BlockSpec((CONV_WIDTH, d), const),
                  pl.BlockSpec((1, d), const),
                  pl.BlockSpec((LRU_BLOCKS, bw, 2 * bw), lambda i: (0, 0, 0)),
                  pl.BlockSpec((1, d), const),
                  pl.BlockSpec((1, d), const),
                  pl.BlockSpec((1, d), const)],
        out_specs=[pl.BlockSpec((nseq, tt, d), lambda i: (0, i, 0)), pl.BlockSpec((nseq, d), const)],
        out_shape=[jax.ShapeDtypeStruct((nseq, S, d), F32), jax.ShapeDtypeStruct((nseq, d), F32)],
        scratch_shapes=[pltpu.VMEM((tt * nseq, d), F32),
                        pltpu.VMEM((CONV_WIDTH - 1, nseq, d), F32),
                        pltpu.VMEM((nseq, d), F32)],
        compiler_params=_cparams(("arbitrary",)),
        name="lru_seq",
    )(gx.reshape(nseq, S, d2), conv0, h0, p["conv_w"], p["conv_b"], p["wax"], p["b_a"], p["b_x"], p["lam"])
    return y.reshape(M, d), hout


def _lru_step_kernel(xr_ref, c0_ref, c1_ref, c2_ref, h0_ref, cw_ref, cb_ref, wax_ref, ba_ref, bx_ref, lam_ref, hout_ref):
    xr = _conv_step(xr_ref[...], (c0_ref[...], c1_ref[...], c2_ref[...]), cw_ref, cb_ref)
    a, bt = _lru_gates(xr, wax_ref, ba_ref, bx_ref, lam_ref)
    hout_ref[...] = a * h0_ref[...] + bt


def _lru_step(xr, conv_state, h0, p, *, nb):
    d = xr.shape[1]
    bw = d // LRU_BLOCKS
    const = lambda i: (0, 0)
    prev = lambda k: pl.BlockSpec((nb, d), lambda i: (0, k))
    conv_state = conv_state.reshape(nb, -1)
    return pl.pallas_call(
        _lru_step_kernel,
        grid=(1,),
        in_specs=[pl.BlockSpec((nb, d), const),
                  prev(0), prev(1), prev(2),
                  pl.BlockSpec((nb, d), const),
                  pl.BlockSpec((CONV_WIDTH, d), const),
                  pl.BlockSpec((1, d), const),
                  pl.BlockSpec((LRU_BLOCKS, bw, 2 * bw), lambda i: (0, 0, 0)),
                  pl.BlockSpec((1, d), const),
                  pl.BlockSpec((1, d), const),
                  pl.BlockSpec((1, d), const)],
        out_specs=pl.BlockSpec((nb, d), const),
        out_shape=jax.ShapeDtypeStruct((nb, d), F32),
        compiler_params=_cparams(("arbitrary",)),
        name="lru_step",
    )(xr, conv_state, conv_state, conv_state, h0, p["conv_w"], p["conv_b"], p["wax"], p["b_a"], p["b_x"], p["lam"])


def _row(v):
    return v.reshape(1, -1).astype(F32)


def _pad_lanes(v):
    return jnp.pad(v.astype(F32), (0, LANES - v.shape[0])).reshape(1, LANES)


def kernel(x_prompt, x_sample, state_ssd_conv, state_ssd_h, state_lru_conv, state_lru_h, meta_tokens, norm_mix_pre, norm_mix_post, norm_ffn_pre, norm_ffn_post, ssd_w_in, ssd_conv_w, ssd_conv_b, ssd_dt_bias, ssd_a_log, ssd_d, ssd_norm, ssd_w_out, lru_w_in, lru_b_in, lru_conv_w, lru_conv_b, lru_w_a, lru_b_a, lru_w_x, lru_b_x, lru_lambda, lru_w_out, lru_b_out, ffn_w1, ffn_w2):
    B, S, D = x_prompt.shape
    nb = x_sample.shape[0]
    n_meta = meta_tokens.shape[0]
    depth = norm_mix_pre.shape[0]
    T = CHUNK
    n_heads = ssd_dt_bias.shape[1]
    d_inner = n_heads * SSD_HEAD_DIM
    cdim = ssd_conv_w.shape[2]
    d_rnn = lru_conv_w.shape[2]
    gw = d_inner // SSD_GROUPS
    tm_p = min(PROJ_TM, S)
    assert x_sample.shape[1] == 1 and nb == T and S % tm_p == 0 and tm_p % T == 0 and B == SUBLANES
    assert SUBLANES <= n_meta <= T and n_meta % SUBLANES == 0
    nchunk = S // T
    meta_blk = nb // T

    xs = jnp.concatenate([x_sample[:, 0, :], meta_tokens, jnp.zeros((T - n_meta, D), F32)], axis=0)
    xp = x_prompt.reshape(B * S, D)
    xn_s = _norm_bf16(xs, _row(norm_mix_pre[0]), tm=PROJ_TM)
    xn_p = _norm_bf16(xp, _row(norm_mix_pre[0]), tm=PROJ_TM)

    small = dict(tm=T, n_raw=meta_blk, n_valid=n_meta,
                 conv=dict(tm=T, n_raw=meta_blk, tiles_per_seq=1, tail_lo=n_meta - SUBLANES))
    prompt = dict(tm=tm_p, n_raw=0, n_valid=tm_p,
                  conv=dict(tm=tm_p, n_raw=0, tiles_per_seq=S // tm_p, tail_lo=tm_p - SUBLANES))
    last3 = lambda tail: tail.reshape(B, S // tm_p, SUBLANES, -1)[:, -1, SUBLANES - (CONV_WIDTH - 1):, :]

    w1_all, w2_all = ffn_w1.astype(BF16), ffn_w2.astype(BF16)
    ssd_wo_all, lru_wo_all = ssd_w_out.astype(BF16), lru_w_out.astype(BF16)

    p_ssd_conv, p_ssd_h, p_lru_conv, p_lru_h = [], [], [], []
    s_ssd_conv, s_lru_conv, s_lru_h = [], [], []
    s_ssd_h = None
    for i in range(depth):
        j = i // 2
        if i % 2 == 0:
            w_dt = jnp.pad(ssd_w_in[j][:, d_inner + cdim:], ((0, 0), (0, LANES - n_heads)))
            p = dict(conv_w=ssd_conv_w[j], conv_b=_row(ssd_conv_b[j]), dt_bias=_pad_lanes(ssd_dt_bias[j]),
                     a_log=_pad_lanes(ssd_a_log[j]), dexp=_row(jnp.repeat(ssd_d[j], SSD_HEAD_DIM)))
            mix, ng, wo_all, b_out = "ssd", _row(ssd_norm[j]), ssd_wo_all, jnp.zeros((1, D), F32)

            def run_proj(xn, conv0, v):
                xbc, tail = _proj(xn, ssd_w_in, j, d_inner, cdim, "conv_silu", tn=PROJ_TN,
                                  conv=(p["conv_w"], p["conv_b"], conv0), **v["conv"])
                z = _proj(xn, ssd_w_in, j, 0, d_inner, "none", tm=v["tm"], tn=PROJ_TN)
                dtq = _dt_proj(xn, w_dt, p["dt_bias"], p["a_log"], tm=v["tm"], n_raw=v["n_raw"],
                               n_valid=v["n_valid"])
                return xbc, tail, z, dtq

            xbc_s, tail_s, g_s, dtq_s = run_proj(xn_s, jnp.zeros((1, SUBLANES, cdim), F32), small)
            u_samp, s_ssd_h = _ssd_step(xbc_s, dtq_s[0], state_ssd_conv[j],
                                        state_ssd_h.reshape(-1, SSD_GROUPS, gw, SSD_STATE), s_ssd_h, p,
                                        nb=nb, layer=j)
            u_meta, h_meta = _ssd_chunk(xbc_s, dtq_s, jnp.zeros((1, SSD_GROUPS, gw, SSD_STATE), F32), p,
                                        nseq=1, nchunk=1, row_block0=meta_blk)
            conv_meta = tail_s[meta_blk:meta_blk + 1]
            s_ssd_conv.append(jnp.concatenate([state_ssd_conv[j][:, 1:], xbc_s[:nb, None, :]], axis=1))
            u_s = jnp.concatenate([u_samp, u_meta], axis=0)

            xbc_p, tail_p, g_p, dtq_p = run_proj(xn_p, conv_meta, prompt)
            u_p, h_p = _ssd_chunk(xbc_p, dtq_p, h_meta, p, nseq=B, nchunk=nchunk)
            p_ssd_conv.append(last3(tail_p))
            p_ssd_h.append(h_p.reshape(B, n_heads, SSD_HEAD_DIM, SSD_STATE))
        else:
            p = dict(conv_w=lru_conv_w[j], conv_b=_row(lru_conv_b[j]),
                     wax=jnp.concatenate([lru_w_a[j], lru_w_x[j]], axis=-1).astype(BF16),
                     b_a=_row(lru_b_a[j]), b_x=_row(lru_b_x[j]), lam=_row(lru_lambda[j]))
            mix, ng, wo_all, b_out = "lru", None, lru_wo_all, _row(lru_b_out[j])
            b_in = _row(lru_b_in[j])

            g_s = _proj(xn_s, lru_w_in, j, 0, d_rnn, "none", tm=T, tn=PROJ_TN, bias=b_in)
            xr_s, tail_s = _proj(xn_s, lru_w_in, j, d_rnn, d_rnn, "conv", tn=PROJ_TN, bias=b_in,
                                 conv=(p["conv_w"], p["conv_b"], jnp.zeros((1, SUBLANES, d_rnn), F32)),
                                 **small["conv"])
            h_samp = _lru_step(xr_s, state_lru_conv[j], state_lru_h[j], p, nb=nb)
            u_meta, h_meta = _lru_chunk(xr_s, jnp.zeros((1, 1, d_rnn), F32), p,
                                        nseq=1, nchunk=1, row_block0=meta_blk, n_valid=n_meta)
            conv_meta = tail_s[meta_blk:meta_blk + 1]
            s_lru_conv.append(jnp.concatenate([state_lru_conv[j][:, 1:], xr_s[:nb, None, :]], axis=1))
            s_lru_h.append(h_samp)
            u_s = jnp.concatenate([h_samp, u_meta], axis=0)

            g_p = _proj(xn_p, lru_w_in, j, 0, 2 * d_rnn, "none", tm=tm_p, tn=PROJ_TN, bias=b_in)
            u_p, h_p = _lru_seq(g_p, conv_meta, h_meta, p, nseq=B, tt=LRU_TT)
            p_lru_conv.append(g_p.reshape(B, S, 2 * d_rnn)[:, S - (CONV_WIDTH - 1):, d_rnn:])
            p_lru_h.append(h_p)

        g_next = _row(norm_mix_pre[i + 1]) if i + 1 < depth else None
        ffn = (i, wo_all, j, b_out, _row(norm_mix_post[i]), _row(norm_ffn_pre[i]), w1_all, w2_all,
               _row(norm_ffn_post[i]), g_next)
        xs, xn_s = _out_ffn(u_s, g_s, ng, xs, *ffn, mix=mix, tm=FFN_TM)
        xp, xn_p = _out_ffn(u_p, g_p, ng, xp, *ffn, mix=mix, tm=FFN_TM)

    return (xp.reshape(B, S, D), xs[:nb].reshape(nb, 1, D),
            jnp.stack(p_ssd_conv), jnp.stack(p_ssd_h), jnp.stack(p_lru_conv), jnp.stack(p_lru_h),
            jnp.stack(s_ssd_conv), s_ssd_h.reshape(state_ssd_h.shape), jnp.stack(s_lru_conv), jnp.stack(s_lru_h))
```

```python
import functools
import math

import jax
import jax.numpy as jnp
from jax import lax
from jax.experimental import pallas as pl
from jax.experimental.pallas import tpu as pltpu

F32 = jnp.float32
BF16 = jnp.bfloat16

EPS = 1e-6
CONV_WIDTH = 4
SSD_HEAD_DIM = 64
SSD_GROUPS = 8
SSD_STATE = 128
LRU_BLOCKS = 8
LRU_C = 8.0
CHUNK = 128
LANES = 128
SUBLANES = 8
MXU_COLS = 256
VMEM_LIMIT = 56 * 1024 * 1024
PROJ_TM = 1024
PROJ_TN = 1024
FFN_TM = 512
LRU_TT = 64
SSD_SEQ_PER_STEP = 2


def _cparams(sem):
    return pltpu.CompilerParams(dimension_semantics=sem, vmem_limit_bytes=VMEM_LIMIT)


def _rms(x, g):
    return x * lax.rsqrt(jnp.mean(x * x, axis=-1, keepdims=True) + EPS) * g


def _softplus(x):
    return jnp.maximum(x, 0.0) + jnp.log1p(jnp.exp(-jnp.abs(x)))


def _silu(x):
    return x * jax.nn.sigmoid(x)


def _gelu_tanh(x):
    return 0.5 * x * (1.0 + jnp.tanh(math.sqrt(2.0 / math.pi) * (x + 0.044715 * (x * x * x))))


def _dot(a, b):
    return jnp.dot(a, b, preferred_element_type=F32)


def _dot_nt(a, b):
    return lax.dot_general(a, b, (((1,), (1,)), ((), ())), preferred_element_type=F32)


def _pair_expand(q, j, lane):
    return jnp.where(lane < SSD_HEAD_DIM, q[:, 2 * j:2 * j + 1], q[:, 2 * j + 1:2 * j + 2])


def _conv_rows(x, prev8, cw, cb):
    row8 = lax.broadcasted_iota(jnp.int32, prev8.shape, 0)
    out = cb + cw[CONV_WIDTH - 1:CONV_WIDTH, :] * x
    for k in range(1, CONV_WIDTH):
        r = pltpu.roll(x, k, 0)
        head = jnp.where(row8 < k, pltpu.roll(prev8, k, 0), r[0:SUBLANES, :])
        shifted = jnp.concatenate([head, r[SUBLANES:, :]], axis=0)
        out = out + cw[CONV_WIDTH - 1 - k:CONV_WIDTH - k, :] * shifted
    return out


def _conv_step(x, prevs, cw_ref, cb_ref):
    out = cb_ref[...] + cw_ref[CONV_WIDTH - 1:CONV_WIDTH, :] * x
    for k in range(CONV_WIDTH - 1):
        out = out + cw_ref[k:k + 1, :] * prevs[k]
    return out


def _ssd_gate_norm(y, zs, ng_ref):
    y = y * zs
    gw = y.shape[1] // SSD_GROUPS
    parts = []
    for g in range(SSD_GROUPS):
        seg = y[:, g * gw:(g + 1) * gw]
        parts.append(seg * lax.rsqrt(jnp.mean(seg * seg, axis=-1, keepdims=True) + EPS))
    return jnp.concatenate(parts, axis=1) * ng_ref[...]


def _lru_gates(xr, wax_ref, ba_ref, bx_ref, lam_ref):
    bw = xr.shape[1] // LRU_BLOCKS
    ra, ix = [], []
    for k in range(LRU_BLOCKS):
        g = _dot(xr[:, k * bw:(k + 1) * bw].astype(BF16), wax_ref[k])
        ra.append(g[:, :bw])
        ix.append(g[:, bw:])
    r = jax.nn.sigmoid(jnp.concatenate(ra, axis=1) + ba_ref[...])
    i = jax.nn.sigmoid(jnp.concatenate(ix, axis=1) + bx_ref[...])
    log_a = (-LRU_C) * r * _softplus(-lam_ref[...])
    a = jnp.exp(log_a)
    v = -jnp.tanh(log_a) * (a * a + 1.0)
    mult = jnp.where(v > 0.0, v * lax.rsqrt(v), 0.0)
    return a, mult * i * xr


def _norm_kernel(x_ref, g_ref, o_ref):
    o_ref[...] = _rms(x_ref[...], g_ref[...]).astype(o_ref.dtype)


def _norm_bf16(x, g, *, tm):
    M, D = x.shape
    tm = min(tm, M)
    assert M % tm == 0
    return pl.pallas_call(
        _norm_kernel,
        grid=(M // tm,),
        in_specs=[pl.BlockSpec((tm, D), lambda i: (i, 0)), pl.BlockSpec((1, D), lambda i: (0, 0))],
        out_specs=pl.BlockSpec((tm, D), lambda i: (i, 0)),
        out_shape=jax.ShapeDtypeStruct((M, D), BF16),
        compiler_params=_cparams(("parallel",)),
        name="norm",
    )(x, g)


def _proj_kernel(*refs, mode, has_bias, n_raw, tiles_per_seq, tail_lo):
    has_conv = mode in ("conv", "conv_silu")
    it = iter(refs)
    xn_ref, w_ref = next(it), next(it)
    b_ref = next(it) if has_bias else None
    cw_ref, cb_ref, conv0_ref = (next(it), next(it), next(it)) if has_conv else (None, None, None)
    o_ref = next(it)
    tail_ref = next(it) if has_conv else None
    wb_s = next(it)
    carry_s = next(it) if has_conv else None
    i = pl.program_id(1)

    @pl.when(i == 0)
    def _():
        wb_s[...] = w_ref[...].astype(BF16)

    tm, tn = o_ref.shape
    for n in range(tn // MXU_COLS):
        sl = pl.ds(n * MXU_COLS, MXU_COLS)
        acc = _dot(xn_ref[...], wb_s[:, sl])
        if has_bias:
            acc = acc + b_ref[:, sl]
        if not has_conv:
            o_ref[:, sl] = acc
            continue
        tail_ref[0, :, sl] = acc[tail_lo:tail_lo + SUBLANES, :]

        def conv_tile(acc=acc, sl=sl):
            start = lax.rem(i - n_raw, tiles_per_seq) == 0
            prev8 = jnp.where(start, conv0_ref[0, :, sl], carry_s[:, sl])
            out = _conv_rows(acc, prev8, cw_ref[:, sl], cb_ref[:, sl])
            carry_s[:, sl] = acc[tm - SUBLANES:tm, :]
            o_ref[:, sl] = _silu(out) if mode == "conv_silu" else out

        if n_raw == 0:
            conv_tile()
        else:
            @pl.when(i < n_raw)
            def _(acc=acc, sl=sl):
                o_ref[:, sl] = acc

            pl.when(i >= n_raw)(conv_tile)


def _proj(xn, w, layer, col0, n_out, mode, *, tm, tn, bias=None, conv=None, n_raw=0, tiles_per_seq=1,
          tail_lo=0):
    M, D = xn.shape
    tm = min(tm, M)
    assert M % tm == 0 and n_out % tn == 0 and col0 % tn == 0 and tn % MXU_COLS == 0
    assert (conv is not None) == (mode in ("conv", "conv_silu"))
    n_i, n_j, jb = M // tm, n_out // tn, col0 // tn
    in_specs = [pl.BlockSpec((tm, D), lambda j, i: (i, 0)),
                pl.BlockSpec((D, tn), lambda j, i: (layer, jb + j))]
    args = [xn, w.reshape(-1, w.shape[-1])]
    if bias is not None:
        in_specs.append(pl.BlockSpec((1, tn), lambda j, i: (0, jb + j)))
        args.append(bias)
    out_shape = [jax.ShapeDtypeStruct((M, n_out), F32)]
    out_specs = [pl.BlockSpec((tm, tn), lambda j, i: (i, j))]
    scratch = [pltpu.VMEM((D, tn), BF16)]
    if conv is not None:
        cw, cb, conv0 = conv
        in_specs += [pl.BlockSpec((CONV_WIDTH, tn), lambda j, i: (0, j)),
                     pl.BlockSpec((1, tn), lambda j, i: (0, j)),
                     pl.BlockSpec((1, SUBLANES, tn), lambda j, i: (0, 0, j))]
        args += [cw, cb, conv0]
        out_shape.append(jax.ShapeDtypeStruct((n_i, SUBLANES, n_out), F32))
        out_specs.append(pl.BlockSpec((1, SUBLANES, tn), lambda j, i: (i, 0, j)))
        scratch.append(pltpu.VMEM((SUBLANES, tn), F32))
    res = pl.pallas_call(
        functools.partial(_proj_kernel, mode=mode, has_bias=bias is not None, n_raw=n_raw,
                          tiles_per_seq=tiles_per_seq, tail_lo=tail_lo),
        grid=(n_j, n_i),
        in_specs=in_specs, out_specs=out_specs, out_shape=out_shape, scratch_shapes=scratch,
        compiler_params=_cparams(("arbitrary", "arbitrary")),
        name="proj_" + mode,
    )(*args)
    return res if conv is not None else res[0]


def _dt_kernel(xn_ref, w_ref, dtb_ref, alog_ref, dtv_ref, cs_ref, csT_ref, dtT_ref, *, n_raw, n_valid):
    i = pl.program_id(0)
    tm = xn_ref.shape[0]
    T = CHUNK
    dtv = _softplus(_dot(xn_ref[...], w_ref[...].astype(BF16)) + dtb_ref[...])
    if n_valid < tm:
        keep = jnp.logical_or(lax.broadcasted_iota(jnp.int32, dtv.shape, 0) < n_valid, i < n_raw)
        dtv = jnp.where(keep, dtv, 0.0)
    dtv_ref[...] = dtv
    a = dtv * (-jnp.exp(alog_ref[...]))
    tri = (lax.broadcasted_iota(jnp.int32, (T, T), 0) >= lax.broadcasted_iota(jnp.int32, (T, T), 1)).astype(F32)
    for r in range(tm // T):
        cs = jnp.dot(tri, a[r * T:(r + 1) * T, :], precision=lax.Precision.HIGHEST, preferred_element_type=F32)
        cs_ref[r * T:(r + 1) * T, :] = cs
        csT_ref[r] = cs.T
        dtT_ref[r] = dtv[r * T:(r + 1) * T, :].T


def _dt_proj(xn, w_dt, dtb, alog, *, tm, n_raw, n_valid):
    M, D = xn.shape
    T = CHUNK
    tm = min(tm, M)
    assert M % tm == 0 and tm % T == 0 and T == LANES and (n_valid >= tm or tm == T)
    const = lambda i: (0, 0)
    rows = pl.BlockSpec((tm, LANES), lambda i: (i, 0))
    rowsT = pl.BlockSpec((tm // T, LANES, T), lambda i: (i, 0, 0))
    return pl.pallas_call(
        functools.partial(_dt_kernel, n_raw=n_raw, n_valid=n_valid),
        grid=(M // tm,),
        in_specs=[pl.BlockSpec((tm, D), lambda i: (i, 0)),
                  pl.BlockSpec((D, LANES), const),
                  pl.BlockSpec((1, LANES), const),
                  pl.BlockSpec((1, LANES), const)],
        out_specs=[rows, rows, rowsT, rowsT],
        out_shape=[jax.ShapeDtypeStruct((M, LANES), F32), jax.ShapeDtypeStruct((M, LANES), F32),
                   jax.ShapeDtypeStruct((M // T, LANES, T), F32), jax.ShapeDtypeStruct((M // T, LANES, T), F32)],
        compiler_params=_cparams(("parallel",)),
        name="dt_proj",
    )(xn, w_dt, dtb, alog)


def _out_ffn_kernel(*refs, fc, mix, has_next):
    it = iter(refs)
    u_ref, g_ref = next(it), next(it)
    ng_ref = next(it) if mix == "ssd" else None
    x_ref, wo_ref, bo_ref, gpost_ref, gpre_ref, w1_ref, w2_ref, gfpost_ref = (next(it) for _ in range(8))
    gnext_ref = next(it) if has_next else None
    o_ref = next(it)
    xn_ref = next(it) if has_next else None
    if mix == "ssd":
        y = _ssd_gate_norm(u_ref[...], _silu(g_ref[...]), ng_ref)
    else:
        y = u_ref[...] * _gelu_tanh(g_ref[...])
    m = _dot(y.astype(BF16), wo_ref[...]) + bo_ref[...]
    x1 = x_ref[...] + _rms(m, gpost_ref[...])
    hn = _rms(x1, gpre_ref[...]).astype(BF16)
    dff = w1_ref.shape[1]
    acc = None
    for c in range(dff // fc):
        h = jnp.maximum(_dot(hn, w1_ref[:, c * fc:(c + 1) * fc]), 0.0)
        part = _dot((h * h).astype(BF16), w2_ref[c * fc:(c + 1) * fc, :])
        acc = part if acc is None else acc + part
    x2 = x1 + _rms(acc, gfpost_ref[...])
    o_ref[...] = x2
    if has_next:
        xn_ref[...] = _rms(x2, gnext_ref[...]).astype(xn_ref.dtype)


def _out_ffn(u, g, ng, x, layer, wo, wo_layer, bo, gpost, gpre, w1, w2, gfpost, gnext, *, mix, tm, fc=1024):
    M, D = x.shape
    K = u.shape[1]
    dff = w1.shape[2]
    tm = min(tm, M)
    assert M % tm == 0 and dff % fc == 0
    const = lambda i: (0, 0)
    once = dict(pipeline_mode=pl.Buffered(1))
    rows = lambda w: pl.BlockSpec((tm, w), lambda i: (i, 0))
    has_next = gnext is not None
    in_specs = [rows(K), rows(K)]
    args = [u, g]
    if mix == "ssd":
        in_specs.append(pl.BlockSpec((1, K), const))
        args.append(ng)
    in_specs += [rows(D),
                 pl.BlockSpec((None, K, D), lambda i: (wo_layer, 0, 0), **once),
                 pl.BlockSpec((1, D), const),
                 pl.BlockSpec((1, D), const),
                 pl.BlockSpec((1, D), const),
                 pl.BlockSpec((None, D, dff), lambda i: (layer, 0, 0), **once),
                 pl.BlockSpec((None, dff, D), lambda i: (layer, 0, 0), **once),
                 pl.BlockSpec((1, D), const)]
    args += [x, wo, bo, gpost, gpre, w1, w2, gfpost]
    out_specs = [rows(D)]
    out_shape = [jax.ShapeDtypeStruct((M, D), F32)]
    if has_next:
        in_specs.append(pl.BlockSpec((1, D), const))
        args.append(gnext)
        out_specs.append(rows(D))
        out_shape.append(jax.ShapeDtypeStruct((M, D), BF16))
    res = pl.pallas_call(
        functools.partial(_out_ffn_kernel, fc=fc, mix=mix, has_next=has_next),
        grid=(M // tm,),
        in_specs=in_specs, out_specs=out_specs, out_shape=out_shape,
        compiler_params=_cparams(("parallel",)),
        name="out_ffn_" + mix,
    )(*args)
    return (res[0], res[1]) if has_next else (res[0], None)


def _ssd_chunk_kernel(xbc_ref, cs_ref, csT_ref, dtT_ref, h0_ref, dexp_ref, y_ref, hout_ref, hT_s, *, T):
    nq = y_ref.shape[0]
    c = pl.program_id(1)

    @pl.when(c == 0)
    def _():
        for q in range(nq):
            for g in range(SSD_GROUPS):
                hT_s[q, g] = h0_ref[0, g].T

    for q in range(nq):
        _ssd_chunk_body(xbc_ref.at[q], cs_ref.at[q], csT_ref.at[q], dtT_ref.at[q], dexp_ref, y_ref.at[q],
                        hT_s.at[q], T=T)

    @pl.when(c == pl.num_programs(1) - 1)
    def _():
        for q in range(nq):
            for g in range(SSD_GROUPS):
                hout_ref[q, g] = hT_s[q, g].T


def _ssd_chunk_body(xbc_ref, cs_ref, csT_ref, dtT_ref, dexp_ref, y_ref, hT_s, *, T):
    d_inner = y_ref.shape[1]
    gs = SSD_GROUPS * SSD_STATE
    gw = d_inner // SSD_GROUPS

    row = lax.broadcasted_iota(jnp.int32, (T, T), 0)
    col = lax.broadcasted_iota(jnp.int32, (T, T), 1)
    causal = row >= col
    lane = lax.broadcasted_iota(jnp.int32, (T, LANES), 1)
    lo_half = lane < SSD_HEAD_DIM

    cs = cs_ref[...]
    csT = csT_ref[...]
    dtT = dtT_ref[...]
    wdT = dtT * jnp.exp(csT[:, T - 1:T] - csT)
    dec_tot = jnp.exp(cs[T - 1:T, :])

    ys = []
    for g in range(SSD_GROUPS):
        Bg = xbc_ref[:, d_inner + g * SSD_STATE:d_inner + (g + 1) * SSD_STATE]
        Cg = xbc_ref[:, d_inner + gs + g * SSD_STATE:d_inner + gs + (g + 1) * SSD_STATE].astype(BF16)
        BgT = Bg.T
        cb = _dot(Cg, BgT.astype(BF16))
        hTg = hT_s[g]
        yoff = _dot(Cg, hTg.astype(BF16))
        for jj in range(gw // LANES):
            j = g * (gw // LANES) + jj
            xpair = xbc_ref[:, j * LANES:(j + 1) * LANES]
            s_parts, bw_parts, ecs_parts = [], [], []
            for hh in (2 * j, 2 * j + 1):
                csl = jnp.broadcast_to(cs[:, hh:hh + 1], (T, T))
                L = jnp.where(causal, jnp.exp(csl - csT[hh:hh + 1, :]), 0.0)
                s_parts.append((cb * L * dtT[hh:hh + 1, :]).astype(BF16))
                bw_parts.append((BgT * wdT[hh:hh + 1, :]).astype(BF16))
                ecs_parts.append(jnp.exp(csl))
            x2 = jnp.concatenate([jnp.where(lo_half, xpair, 0.0).astype(BF16),
                                  jnp.where(lo_half, 0.0, xpair).astype(BF16)], axis=0)
            ydiag = _dot(jnp.concatenate(s_parts, axis=1), x2)
            upd = _dot(jnp.concatenate(bw_parts, axis=1), x2)
            ecs = jnp.where(lo_half, ecs_parts[0], ecs_parts[1])
            ys.append(ydiag + yoff[:, jj * LANES:(jj + 1) * LANES] * ecs
                      + xpair * dexp_ref[:, j * LANES:(j + 1) * LANES])
            dect = _pair_expand(dec_tot, j, lane[0:1, :])
            hT_s[g, :, jj * LANES:(jj + 1) * LANES] = hTg[:, jj * LANES:(jj + 1) * LANES] * dect + upd

    y_ref[...] = jnp.concatenate(ys, axis=1)


def _ssd_chunk(xbc, dtq, h0, p, *, nseq, nchunk, row_block0=0):
    T = CHUNK
    assert T == LANES and h0.shape[0] == 1 and (nseq == 1 or row_block0 == 0)
    d_inner = p["dexp"].shape[1]
    gw = d_inner // SSD_GROUPS
    _, cs, csT, dtT = dtq
    nq = SSD_SEQ_PER_STEP if nseq % SSD_SEQ_PER_STEP == 0 else 1
    ng = nseq // nq
    v4 = lambda a: a.reshape(ng, nq, a.shape[0] // nseq, a.shape[1])
    v5 = lambda a: a.reshape(ng, nq, a.shape[0] // nseq, a.shape[1], a.shape[2])
    rows = lambda w: pl.BlockSpec((None, nq, T, w), lambda b, c: (b, 0, row_block0 + c, 0))
    rowsT = pl.BlockSpec((None, nq, None, LANES, T), lambda b, c: (b, 0, row_block0 + c, 0, 0))
    hshape = (SSD_GROUPS, gw, SSD_STATE)
    y, hout = pl.pallas_call(
        functools.partial(_ssd_chunk_kernel, T=T),
        grid=(ng, nchunk),
        in_specs=[rows(xbc.shape[1]), rows(LANES), rowsT, rowsT,
                  pl.BlockSpec((1,) + hshape, lambda b, c: (0, 0, 0, 0)),
                  pl.BlockSpec((1, d_inner), lambda b, c: (0, 0))],
        out_specs=[pl.BlockSpec((None, nq, T, d_inner), lambda b, c: (b, 0, c, 0)),
                   pl.BlockSpec((None, nq) + hshape, lambda b, c: (b, 0, 0, 0, 0))],
        out_shape=[jax.ShapeDtypeStruct((ng, nq, nchunk * T, d_inner), F32),
                   jax.ShapeDtypeStruct((ng, nq) + hshape, F32)],
        scratch_shapes=[pltpu.VMEM((nq, SSD_GROUPS, SSD_STATE, gw), F32)],
        compiler_params=_cparams(("parallel", "arbitrary")),
        name="ssd_chunk",
    )(v4(xbc), v4(cs), v5(csT), v5(dtT), h0, p["dexp"])
    return y.reshape(nseq * nchunk * T, d_inner), hout.reshape((nseq,) + hshape)


def _ssd_step_kernel(*refs, seq_per_step, has_prev):
    xbc_ref, dt_ref, c0_ref, c1_ref, c2_ref, h_ref, cw_ref, cb_ref, alog_ref, dexp_ref = refs[:10]
    y_ref, hout_ref, xdtT_s, decT_s, b_s, c_s, xs_s, yrow_s = refs[10 + has_prev:]
    s = pl.program_id(0)
    nb, d_inner = xs_s.shape
    gs = SSD_GROUPS * SSD_STATE
    gw = d_inner // SSD_GROUPS

    @pl.when(s == 0)
    def _():
        xbc = _silu(_conv_step(xbc_ref[...], (c0_ref[...], c1_ref[...], c2_ref[...]), cw_ref, cb_ref))
        xs = xbc[:, :d_inner]
        xs_s[...] = xs
        b_s[...] = xbc[:, d_inner:d_inner + gs]
        c_s[...] = xbc[:, d_inner + gs:]
        dtv = dt_ref[...]
        decT_s[...] = jnp.exp(dtv * (-jnp.exp(alog_ref[...]))).T
        dtT = dtv.T
        for j in range(d_inner // LANES):
            xT = xs[:, j * LANES:(j + 1) * LANES].T
            for k in range(2):
                hh = 2 * j + k
                xdtT_s[hh * SSD_HEAD_DIM:(hh + 1) * SSD_HEAD_DIM, :] = (
                    xT[k * SSD_HEAD_DIM:(k + 1) * SSD_HEAD_DIM, :] * dtT[hh:hh + 1, :])

    for q in range(seq_per_step):
        b = s * seq_per_step + q
        sel = lax.broadcasted_iota(jnp.int32, (1, nb), 1) == b
        dec_col = jnp.sum(jnp.where(sel, decT_s[...], 0.0), axis=1, keepdims=True)
        brow = b_s[pl.ds(b, 1), :]
        crow = c_s[pl.ds(b, 1), :]
        for g in range(SSD_GROUPS):
            xcol = jnp.sum(jnp.where(sel, xdtT_s[g * gw:(g + 1) * gw, :], 0.0), axis=1, keepdims=True)
            dcol = jnp.concatenate(
                [jnp.broadcast_to(dec_col[hh:hh + 1, :], (SSD_HEAD_DIM, 1))
                 for hh in range(g * (gw // SSD_HEAD_DIM), (g + 1) * (gw // SSD_HEAD_DIM))], axis=0)
            hn = h_ref[q, g] * dcol + xcol * brow[:, g * SSD_STATE:(g + 1) * SSD_STATE]
            hout_ref[q, g] = hn
            c8 = jnp.broadcast_to(crow[:, g * SSD_STATE:(g + 1) * SSD_STATE], (SUBLANES, SSD_STATE)).astype(BF16)
            yg = _dot_nt(c8, hn.astype(BF16))
            yrow_s[pl.ds(b, 1), g * gw:(g + 1) * gw] = yg[0:1, :]

    @pl.when(s == pl.num_programs(0) - 1)
    def _():
        y_ref[...] = yrow_s[...] + xs_s[...] * dexp_ref[...]


def _ssd_step(xbc, dt, conv_state, h_all, h_out_prev, p, *, nb, layer, seq_per_step=4):
    d_inner = p["dexp"].shape[1]
    cdim = p["conv_w"].shape[1]
    gw = d_inner // SSD_GROUPS
    gs = SSD_GROUPS * SSD_STATE
    assert nb % seq_per_step == 0
    blk0 = layer * (nb // seq_per_step)
    const = lambda s: (0, 0)
    prev = lambda k: pl.BlockSpec((nb, cdim), lambda s: (0, k))
    conv_state = conv_state.reshape(nb, -1)
    hspec = pl.BlockSpec((seq_per_step, SSD_GROUPS, gw, SSD_STATE), lambda s: (blk0 + s, 0, 0, 0))
    has_prev = h_out_prev is not None
    in_specs = [pl.BlockSpec((nb, cdim), const),
                pl.BlockSpec((nb, LANES), const),
                prev(0), prev(1), prev(2),
                hspec,
                pl.BlockSpec((CONV_WIDTH, cdim), const),
                pl.BlockSpec((1, cdim), const),
                pl.BlockSpec((1, LANES), const),
                pl.BlockSpec((1, d_inner), const)]
    args = [xbc, dt, conv_state, conv_state, conv_state, h_all, p["conv_w"], p["conv_b"], p["a_log"], p["dexp"]]
    if has_prev:
        in_specs.append(pl.BlockSpec(memory_space=pl.ANY))
        args.append(h_out_prev)
    return pl.pallas_call(
        functools.partial(_ssd_step_kernel, seq_per_step=seq_per_step, has_prev=has_prev),
        grid=(nb // seq_per_step,),
        in_specs=in_specs,
        out_specs=[pl.BlockSpec((nb, d_inner), const), hspec],
        out_shape=[jax.ShapeDtypeStruct((nb, d_inner), F32),
                   jax.ShapeDtypeStruct(h_all.shape, F32)],
        scratch_shapes=[pltpu.VMEM((d_inner, nb), F32),
                        pltpu.VMEM((LANES, nb), F32),
                        pltpu.VMEM((nb, gs), F32),
                        pltpu.VMEM((nb, gs), F32),
                        pltpu.VMEM((nb, d_inner), F32),
                        pltpu.VMEM((nb, d_inner), F32)],
        input_output_aliases={len(args) - 1: 1} if has_prev else {},
        compiler_params=_cparams(("arbitrary",)),
        name="ssd_step",
    )(*args)


def _shift_rows(x, s, fill, row):
    if s % SUBLANES == 0:
        return jnp.concatenate([jnp.full((s, x.shape[1]), fill, x.dtype), x[:x.shape[0] - s, :]], axis=0)
    return jnp.where(row >= s, pltpu.roll(x, s, 0), fill)


def _lru_chunk_kernel(xr_ref, h0_ref, wax_ref, ba_ref, bx_ref, lam_ref, y_ref, hout_ref, hc_s, *, T, n_valid):
    c = pl.program_id(1)

    @pl.when(c == 0)
    def _():
        hc_s[...] = h0_ref[0]

    a, bt = _lru_gates(xr_ref[...], wax_ref, ba_ref, bx_ref, lam_ref)
    row = lax.broadcasted_iota(jnp.int32, a.shape, 0)
    if n_valid < T:
        a = jnp.where(row < n_valid, a, 1.0)
        bt = jnp.where(row < n_valid, bt, 0.0)
    s = 1
    while s < T:
        bt = a * _shift_rows(bt, s, 0.0, row) + bt
        a = a * _shift_rows(a, s, 1.0, row)
        s *= 2
    h = a * hc_s[...] + bt
    hc_s[...] = h[T - 1:T, :]
    y_ref[...] = h

    @pl.when(c == pl.num_programs(1) - 1)
    def _():
        hout_ref[0] = h[T - 1:T, :]


def _lru_chunk(xr, h0, p, *, nseq, nchunk, row_block0, n_valid):
    T = CHUNK
    d = xr.shape[1]
    bw = d // LRU_BLOCKS
    rows = lambda b, c: (row_block0 + b * nchunk + c, 0)
    shared_h = (lambda b, c: (0, 0, 0)) if h0.shape[0] == 1 else (lambda b, c: (b, 0, 0))
    const = lambda b, c: (0, 0)
    return pl.pallas_call(
        functools.partial(_lru_chunk_kernel, T=T, n_valid=n_valid),
        grid=(nseq, nchunk),
        in_specs=[pl.BlockSpec((T, d), rows),
                  pl.BlockSpec((1, 1, d), shared_h),
                  pl.BlockSpec((LRU_BLOCKS, bw, 2 * bw), lambda b, c: (0, 0, 0)),
                  pl.BlockSpec((1, d), const),
                  pl.BlockSpec((1, d), const),
                  pl.BlockSpec((1, d), const)],
        out_specs=[pl.BlockSpec((T, d), lambda b, c: (b * nchunk + c, 0)),
                   pl.BlockSpec((1, 1, d), lambda b, c: (b, 0, 0))],
        out_shape=[jax.ShapeDtypeStruct((nseq * nchunk * T, d), F32),
                   jax.ShapeDtypeStruct((nseq, 1, d), F32)],
        scratch_shapes=[pltpu.VMEM((1, d), F32)],
        compiler_params=_cparams(("parallel", "arbitrary")),
        name="lru_chunk",
    )(xr, h0, p["wax"], p["b_a"], p["b_x"], p["lam"])


def _lru_seq_kernel(xr_ref, conv0_ref, h0_ref, cw_ref, cb_ref, wax_ref, ba_ref, bx_ref, lam_ref,
                    y_ref, hout_ref, xc_s, prev_s, h_s):
    i = pl.program_id(0)
    nseq, Tt, d = xr_ref.shape

    @pl.when(i == 0)
    def _():
        for k in range(CONV_WIDTH - 1):
            prev_s[k] = jnp.broadcast_to(conv0_ref[0, SUBLANES - CONV_WIDTH + 1 + k:SUBLANES - CONV_WIDTH + 2 + k, :],
                                         (nseq, d))
        h_s[...] = jnp.broadcast_to(h0_ref[0], (nseq, d))

    taps = [prev_s[k] for k in range(CONV_WIDTH - 1)]
    for t in range(Tt):
        x_t = xr_ref[:, t, :]
        c_t = cb_ref[...] + cw_ref[CONV_WIDTH - 1:CONV_WIDTH, :] * x_t
        for k in range(CONV_WIDTH - 1):
            c_t = c_t + cw_ref[k:k + 1, :] * taps[k]
        xc_s[t * nseq:(t + 1) * nseq, :] = c_t
        taps = taps[1:] + [x_t]
    for k in range(CONV_WIDTH - 1):
        prev_s[k] = taps[k]

    a, bt = _lru_gates(xc_s[...], wax_ref, ba_ref, bx_ref, lam_ref)
    h = h_s[...]
    for t in range(Tt):
        h = a[t * nseq:(t + 1) * nseq, :] * h + bt[t * nseq:(t + 1) * nseq, :]
        y_ref[:, t, :] = h
    h_s[...] = h

    @pl.when(i == pl.num_programs(0) - 1)
    def _():
        hout_ref[...] = h


def _lru_seq(gx, conv0, h0, p, *, nseq, tt):
    M, d2 = gx.shape
    d = d2 // 2
    S = M // nseq
    bw = d // LRU_BLOCKS
    tt = min(tt, S)
    assert S % tt == 0 and nseq == SUBLANES and tt % SUBLANES == 0
    const = lambda i: (0, 0)
    y, hout = pl.pallas_call(
        _lru_seq_kernel,
        grid=(S // tt,),
        in_specs=[pl.BlockSpec((nseq, tt, d), lambda i: (0, i, 1)),
                  pl.BlockSpec((1, SUBLANES, d), lambda i: (0, 0, 0)),
                  pl.BlockSpec((1, 1, d), lambda i: (0, 0, 0)),
                  pl.BlockSpec((CONV_WIDTH, d), const),
                  pl.BlockSpec((1, d), const),
                  pl.BlockSpec((LRU_BLOCKS, bw, 2 * bw), lambda i: (0, 0, 0)),
                  pl.BlockSpec((1, d), const),
                  pl.BlockSpec((1, d), const),
                  pl.BlockSpec((1, d), const)],
        out_specs=[pl.BlockSpec((nseq, tt, d), lambda i: (0, i, 0)), pl.BlockSpec((nseq, d), const)],
        out_shape=[jax.ShapeDtypeStruct((nseq, S, d), F32), jax.ShapeDtypeStruct((nseq, d), F32)],
        scratch_shapes=[pltpu.VMEM((tt * nseq, d), F32),
                        pltpu.VMEM((CONV_WIDTH - 1, nseq, d), F32),
                        pltpu.VMEM((nseq, d), F32)],
        compiler_params=_cparams(("arbitrary",)),
        name="lru_seq",
    )(gx.reshape(nseq, S, d2), conv0, h0, p["conv_w"], p["conv_b"], p["wax"], p["b_a"], p["b_x"], p["lam"])
    return y.reshape(M, d), hout


def _lru_step_kernel(xr_ref, c0_ref, c1_ref, c2_ref, h0_ref, cw_ref, cb_ref, wax_ref, ba_ref, bx_ref, lam_ref, hout_ref):
    xr = _conv_step(xr_ref[...], (c0_ref[...], c1_ref[...], c2_ref[...]), cw_ref, cb_ref)
    a, bt = _lru_gates(xr, wax_ref, ba_ref, bx_ref, lam_ref)
    hout_ref[...] = a * h0_ref[...] + bt


def _lru_step(xr, conv_state, h0, p, *, nb):
    d = xr.shape[1]
    bw = d // LRU_BLOCKS
    const = lambda i: (0, 0)
    prev = lambda k: pl.BlockSpec((nb, d), lambda i: (0, k))
    conv_state = conv_state.reshape(nb, -1)
    return pl.pallas_call(
        _lru_step_kernel,
        grid=(1,),
        in_specs=[pl.BlockSpec((nb, d), const),
                  prev(0), prev(1), prev(2),
                  pl.BlockSpec((nb, d), const),
                  pl.BlockSpec((CONV_WIDTH, d), const),
                  pl.BlockSpec((1, d), const),
                  pl.BlockSpec((LRU_BLOCKS, bw, 2 * bw), lambda i: (0, 0, 0)),
                  pl.BlockSpec((1, d), const),
                  pl.BlockSpec((1, d), const),
                  pl.BlockSpec((1, d), const)],
        out_specs=pl.BlockSpec((nb, d), const),
        out_shape=jax.ShapeDtypeStruct((nb, d), F32),
        compiler_params=_cparams(("arbitrary",)),
        name="lru_step",
    )(xr, conv_state, conv_state, conv_state, h0, p["conv_w"], p["conv_b"], p["wax"], p["b_a"], p["b_x"], p["lam"])


def _row(v):
    return v.reshape(1, -1).astype(F32)


def _pad_lanes(v):
    return jnp.pad(v.astype(F32), (0, LANES - v.shape[0])).reshape(1, LANES)


def kernel(x_prompt, x_sample, state_ssd_conv, state_ssd_h, state_lru_conv, state_lru_h, meta_tokens, norm_mix_pre, norm_mix_post, norm_ffn_pre, norm_ffn_post, ssd_w_in, ssd_conv_w, ssd_conv_b, ssd_dt_bias, ssd_a_log, ssd_d, ssd_norm, ssd_w_out, lru_w_in, lru_b_in, lru_conv_w, lru_conv_b, lru_w_a, lru_b_a, lru_w_x, lru_b_x, lru_lambda, lru_w_out, lru_b_out, ffn_w1, ffn_w2):
    B, S, D = x_prompt.shape
    nb = x_sample.shape[0]
    n_meta = meta_tokens.shape[0]
    depth = norm_mix_pre.shape[0]
    T = CHUNK
    n_heads = ssd_dt_bias.shape[1]
    d_inner = n_heads * SSD_HEAD_DIM
    cdim = ssd_conv_w.shape[2]
    d_rnn = lru_conv_w.shape[2]
    gw = d_inner // SSD_GROUPS
    tm_p = min(PROJ_TM, S)
    assert x_sample.shape[1] == 1 and nb == T and S % tm_p == 0 and tm_p % T == 0 and B == SUBLANES
    assert SUBLANES <= n_meta <= T and n_meta % SUBLANES == 0
    nchunk = S // T
    meta_blk = nb // T

    xs = jnp.concatenate([x_sample[:, 0, :], meta_tokens, jnp.zeros((T - n_meta, D), F32)], axis=0)
    xp = x_prompt.reshape(B * S, D)
    xn_s = _norm_bf16(xs, _row(norm_mix_pre[0]), tm=PROJ_TM)
    xn_p = _norm_bf16(xp, _row(norm_mix_pre[0]), tm=PROJ_TM)

    small = dict(tm=T, n_raw=meta_blk, n_valid=n_meta,
                 conv=dict(tm=T, n_raw=meta_blk, tiles_per_seq=1, tail_lo=n_meta - SUBLANES))
    prompt = dict(tm=tm_p, n_raw=0, n_valid=tm_p,
                  conv=dict(tm=tm_p, n_raw=0, tiles_per_seq=S // tm_p, tail_lo=tm_p - SUBLANES))
    last3 = lambda tail: tail.reshape(B, S // tm_p, SUBLANES, -1)[:, -1, SUBLANES - (CONV_WIDTH - 1):, :]

    w1_all, w2_all = ffn_w1.astype(BF16), ffn_w2.astype(BF16)
    ssd_wo_all, lru_wo_all = ssd_w_out.astype(BF16), lru_w_out.astype(BF16)

    p_ssd_conv, p_ssd_h, p_lru_conv, p_lru_h = [], [], [], []
    s_ssd_conv, s_lru_conv, s_lru_h = [], [], []
    s_ssd_h = None
    for i in range(depth):
        j = i // 2
        if i % 2 == 0:
            w_dt = jnp.pad(ssd_w_in[j][:, d_inner + cdim:], ((0, 0), (0, LANES - n_heads)))
            p = dict(conv_w=ssd_conv_w[j], conv_b=_row(ssd_conv_b[j]), dt_bias=_pad_lanes(ssd_dt_bias[j]),
                     a_log=_pad_lanes(ssd_a_log[j]), dexp=_row(jnp.repeat(ssd_d[j], SSD_HEAD_DIM)))
            mix, ng, wo_all, b_out = "ssd", _row(ssd_norm[j]), ssd_wo_all, jnp.zeros((1, D), F32)

            def run_proj(xn, conv0, v):
                xbc, tail = _proj(xn, ssd_w_in, j, d_inner, cdim, "conv_silu", tn=PROJ_TN,
                                  conv=(p["conv_w"], p["conv_b"], conv0), **v["conv"])
                z = _proj(xn, ssd_w_in, j, 0, d_inner, "none", tm=v["tm"], tn=PROJ_TN)
                dtq = _dt_proj(xn, w_dt, p["dt_bias"], p["a_log"], tm=v["tm"], n_raw=v["n_raw"],
                               n_valid=v["n_valid"])
                return xbc, tail, z, dtq

            xbc_s, tail_s, g_s, dtq_s = run_proj(xn_s, jnp.zeros((1, SUBLANES, cdim), F32), small)
            u_samp, s_ssd_h = _ssd_step(xbc_s, dtq_s[0], state_ssd_conv[j],
                                        state_ssd_h.reshape(-1, SSD_GROUPS, gw, SSD_STATE), s_ssd_h, p,
                                        nb=nb, layer=j)
            u_meta, h_meta = _ssd_chunk(xbc_s, dtq_s, jnp.zeros((1, SSD_GROUPS, gw, SSD_STATE), F32), p,
                                        nseq=1, nchunk=1, row_block0=meta_blk)
            conv_meta = tail_s[meta_blk:meta_blk + 1]
            s_ssd_conv.append(jnp.concatenate([state_ssd_conv[j][:, 1:], xbc_s[:nb, None, :]], axis=1))
            u_s = jnp.concatenate([u_samp, u_meta], axis=0)

            xbc_p, tail_p, g_p, dtq_p = run_proj(xn_p, conv_meta, prompt)
            u_p, h_p = _ssd_chunk(xbc_p, dtq_p, h_meta, p, nseq=B, nchunk=nchunk)
            p_ssd_conv.append(last3(tail_p))
            p_ssd_h.append(h_p.reshape(B, n_heads, SSD_HEAD_DIM, SSD_STATE))
        else:
            p = dict(conv_w=lru_conv_w[j], conv_b=_row(lru_conv_b[j]),
                     wax=jnp.concatenate([lru_w_a[j], lru_w_x[j]], axis=-1).astype(BF16),
                     b_a=_row(lru_b_a[j]), b_x=_row(lru_b_x[j]), lam=_row(lru_lambda[j]))
            mix, ng, wo_all, b_out = "lru", None, lru_wo_all, _row(lru_b_out[j])
            b_in = _row(lru_b_in[j])

            g_s = _proj(xn_s, lru_w_in, j, 0, d_rnn, "none", tm=T, tn=PROJ_TN, bias=b_in)
            xr_s, tail_s = _proj(xn_s, lru_w_in, j, d_rnn, d_rnn, "conv", tn=PROJ_TN, bias=b_in,
                                 conv=(p["conv_w"], p["conv_b"], jnp.zeros((1, SUBLANES, d_rnn), F32)),
                                 **small["conv"])
            h_samp = _lru_step(xr_s, state_lru_conv[j], state_lru_h[j], p, nb=nb)
            u_meta, h_meta = _lru_chunk(xr_s, jnp.zeros((1, 1, d_rnn), F32), p,
                                        nseq=1, nchunk=1, row_block0=meta_blk, n_valid=n_meta)
            conv_meta = tail_s[meta_blk:meta_blk + 1]
            s_lru_conv.append(jnp.concatenate([state_lru_conv[j][:, 1:], xr_s[:nb, None, :]], axis=1))
            s_lru_h.append(h_samp)
            u_s = jnp.concatenate([h_samp, u_meta], axis=0)

            g_p = _proj(xn_p, lru_w_in, j, 0, 2 * d_rnn, "none", tm=tm_p, tn=PROJ_TN, bias=b_in)
            u_p, h_p = _lru_seq(g_p, conv_meta, h_meta, p, nseq=B, tt=LRU_TT)
            p_lru_conv.append(g_p.reshape(B, S, 2 * d_rnn)[:, S - (CONV_WIDTH - 1):, d_rnn:])
            p_lru_h.append(h_p)

        g_next = _row(norm_mix_pre[i + 1]) if i + 1 < depth else None
        ffn = (i, wo_all, j, b_out, _row(norm_mix_post[i]), _row(norm_ffn_pre[i]), w1_all, w2_all,
               _row(norm_ffn_post[i]), g_next)
        xs, xn_s = _out_ffn(u_s, g_s, ng, xs, *ffn, mix=mix, tm=FFN_TM)
        xp, xn_p = _out_ffn(u_p, g_p, ng, xp, *ffn, mix=mix, tm=FFN_TM)

    return (xp.reshape(B, S, D), xs[:nb].reshape(nb, 1, D),
            jnp.stack(p_ssd_conv), jnp.stack(p_ssd_h), jnp.stack(p_lru_conv), jnp.stack(p_lru_h),
            jnp.stack(s_ssd_conv), s_ssd_h.reshape(state_ssd_h.shape), jnp.stack(s_lru_conv), jnp.stack(s_lru_h))
```

```python
import functools
import math

import jax
import jax.numpy as jnp
from jax import lax
from jax.experimental import pallas as pl
from jax.experimental.pallas import tpu as pltpu

F32 = jnp.float32
BF16 = jnp.bfloat16

EPS = 1e-6
CONV_WIDTH = 4
SSD_HEAD_DIM = 64
SSD_GROUPS = 8
SSD_STATE = 128
LRU_BLOCKS = 8
LRU_C = 8.0
CHUNK = 128
LANES = 128
SUBLANES = 8
MXU_COLS = 256
VMEM_LIMIT = 56 * 1024 * 1024
PROJ_TM = 1024
PROJ_TN = 1024
FFN_TM = 512
LRU_TT = 64
SSD_SEQ_PER_STEP = 4
FFN_ROW_SPLIT = 2


def _cparams(sem):
    return pltpu.CompilerParams(dimension_semantics=sem, vmem_limit_bytes=VMEM_LIMIT)


def _rms(x, g):
    return x * lax.rsqrt(jnp.mean(x * x, axis=-1, keepdims=True) + EPS) * g


def _softplus(x):
    return jnp.maximum(x, 0.0) + jnp.log1p(jnp.exp(-jnp.abs(x)))


def _silu(x):
    return x * jax.nn.sigmoid(x)


def _gelu_tanh(x):
    return 0.5 * x * (1.0 + jnp.tanh(math.sqrt(2.0 / math.pi) * (x + 0.044715 * (x * x * x))))


def _dot(a, b):
    return jnp.dot(a, b, preferred_element_type=F32)


def _dot_nt(a, b):
    return lax.dot_general(a, b, (((1,), (1,)), ((), ())), preferred_element_type=F32)


def _pair_expand(q, j, lane):
    return jnp.where(lane < SSD_HEAD_DIM, q[:, 2 * j:2 * j + 1], q[:, 2 * j + 1:2 * j + 2])


def _conv_rows(x, prev8, cw, cb):
    row8 = lax.broadcasted_iota(jnp.int32, prev8.shape, 0)
    out = cb + cw[CONV_WIDTH - 1:CONV_WIDTH, :] * x
    for k in range(1, CONV_WIDTH):
        r = pltpu.roll(x, k, 0)
        head = jnp.where(row8 < k, pltpu.roll(prev8, k, 0), r[0:SUBLANES, :])
        shifted = jnp.concatenate([head, r[SUBLANES:, :]], axis=0)
        out = out + cw[CONV_WIDTH - 1 - k:CONV_WIDTH - k, :] * shifted
    return out


def _conv_step(x, prevs, cw_ref, cb_ref):
    out = cb_ref[...] + cw_ref[CONV_WIDTH - 1:CONV_WIDTH, :] * x
    for k in range(CONV_WIDTH - 1):
        out = out + cw_ref[k:k + 1, :] * prevs[k]
    return out


def _ssd_gate_norm(y, zs, ng_ref):
    y = y * zs
    gw = y.shape[1] // SSD_GROUPS
    parts = []
    for g in range(SSD_GROUPS):
        seg = y[:, g * gw:(g + 1) * gw]
        parts.append(seg * lax.rsqrt(jnp.mean(seg * seg, axis=-1, keepdims=True) + EPS))
    return jnp.concatenate(parts, axis=1) * ng_ref[...]


def _lru_gates(xr, wax_ref, ba_ref, bx_ref, lam_ref):
    bw = xr.shape[1] // LRU_BLOCKS
    ra, ix = [], []
    for k in range(LRU_BLOCKS):
        g = _dot(xr[:, k * bw:(k + 1) * bw].astype(BF16), wax_ref[k])
        ra.append(g[:, :bw])
        ix.append(g[:, bw:])
    r = jax.nn.sigmoid(jnp.concatenate(ra, axis=1) + ba_ref[...])
    i = jax.nn.sigmoid(jnp.concatenate(ix, axis=1) + bx_ref[...])
    log_a = (-LRU_C) * r * _softplus(-lam_ref[...])
    a = jnp.exp(log_a)
    v = -jnp.tanh(log_a) * (a * a + 1.0)
    mult = jnp.where(v > 0.0, v * lax.rsqrt(v), 0.0)
    return a, mult * i * xr


def _norm_kernel(x_ref, g_ref, o_ref):
    o_ref[...] = _rms(x_ref[...], g_ref[...]).astype(o_ref.dtype)


def _norm_bf16(x, g, *, tm):
    M, D = x.shape
    tm = min(tm, M)
    assert M % tm == 0
    return pl.pallas_call(
        _norm_kernel,
        grid=(M // tm,),
        in_specs=[pl.BlockSpec((tm, D), lambda i: (i, 0)), pl.BlockSpec((1, D), lambda i: (0, 0))],
        out_specs=pl.BlockSpec((tm, D), lambda i: (i, 0)),
        out_shape=jax.ShapeDtypeStruct((M, D), BF16),
        compiler_params=_cparams(("parallel",)),
        name="norm",
    )(x, g)


def _proj_kernel(*refs, mode, has_bias, n_raw, tiles_per_seq, tail_lo):
    has_conv = mode in ("conv", "conv_silu")
    it = iter(refs)
    xn_ref, w_ref = next(it), next(it)
    b_ref = next(it) if has_bias else None
    cw_ref, cb_ref, conv0_ref = (next(it), next(it), next(it)) if has_conv else (None, None, None)
    o_ref = next(it)
    tail_ref = next(it) if has_conv else None
    wb_s = next(it)
    carry_s = next(it) if has_conv else None
    i = pl.program_id(1)

    @pl.when(i == 0)
    def _():
        wb_s[...] = w_ref[...].astype(BF16)

    tm, tn = o_ref.shape
    for n in range(tn // MXU_COLS):
        sl = pl.ds(n * MXU_COLS, MXU_COLS)
        acc = _dot(xn_ref[...], wb_s[:, sl])
        if has_bias:
            acc = acc + b_ref[:, sl]
        if not has_conv:
            o_ref[:, sl] = acc
            continue
        tail_ref[0, :, sl] = acc[tail_lo:tail_lo + SUBLANES, :]

        def conv_tile(acc=acc, sl=sl):
            start = lax.rem(i - n_raw, tiles_per_seq) == 0
            prev8 = jnp.where(start, conv0_ref[0, :, sl], carry_s[:, sl])
            out = _conv_rows(acc, prev8, cw_ref[:, sl], cb_ref[:, sl])
            carry_s[:, sl] = acc[tm - SUBLANES:tm, :]
            o_ref[:, sl] = _silu(out) if mode == "conv_silu" else out

        if n_raw == 0:
            conv_tile()
        else:
            @pl.when(i < n_raw)
            def _(acc=acc, sl=sl):
                o_ref[:, sl] = acc

            pl.when(i >= n_raw)(conv_tile)


def _proj(xn, w, layer, col0, n_out, mode, *, tm, tn, bias=None, conv=None, n_raw=0, tiles_per_seq=1,
          tail_lo=0):
    M, D = xn.shape
    tm = min(tm, M)
    assert M % tm == 0 and n_out % tn == 0 and col0 % tn == 0 and tn % MXU_COLS == 0
    assert (conv is not None) == (mode in ("conv", "conv_silu"))
    n_i, n_j, jb = M // tm, n_out // tn, col0 // tn
    in_specs = [pl.BlockSpec((tm, D), lambda j, i: (i, 0)),
                pl.BlockSpec((D, tn), lambda j, i: (layer, jb + j))]
    args = [xn, w.reshape(-1, w.shape[-1])]
    if bias is not None:
        in_specs.append(pl.BlockSpec((1, tn), lambda j, i: (0, jb + j)))
        args.append(bias)
    out_shape = [jax.ShapeDtypeStruct((M, n_out), F32)]
    out_specs = [pl.BlockSpec((tm, tn), lambda j, i: (i, j))]
    scratch = [pltpu.VMEM((D, tn), BF16)]
    if conv is not None:
        cw, cb, conv0 = conv
        in_specs += [pl.BlockSpec((CONV_WIDTH, tn), lambda j, i: (0, j)),
                     pl.BlockSpec((1, tn), lambda j, i: (0, j)),
                     pl.BlockSpec((1, SUBLANES, tn), lambda j, i: (0, 0, j))]
        args += [cw, cb, conv0]
        out_shape.append(jax.ShapeDtypeStruct((n_i, SUBLANES, n_out), F32))
        out_specs.append(pl.BlockSpec((1, SUBLANES, tn), lambda j, i: (i, 0, j)))
        scratch.append(pltpu.VMEM((SUBLANES, tn), F32))
    res = pl.pallas_call(
        functools.partial(_proj_kernel, mode=mode, has_bias=bias is not None, n_raw=n_raw,
                          tiles_per_seq=tiles_per_seq, tail_lo=tail_lo),
        grid=(n_j, n_i),
        in_specs=in_specs, out_specs=out_specs, out_shape=out_shape, scratch_shapes=scratch,
        compiler_params=_cparams(("arbitrary", "arbitrary")),
        name="proj_" + mode,
    )(*args)
    return res if conv is not None else res[0]


def _dt_kernel(xn_ref, w_ref, dtb_ref, alog_ref, dtv_ref, cs_ref, csT_ref, dtT_ref, *, n_raw, n_valid):
    i = pl.program_id(0)
    tm = xn_ref.shape[0]
    T = CHUNK
    dtv = _softplus(_dot(xn_ref[...], w_ref[...].astype(BF16)) + dtb_ref[...])
    if n_valid < tm:
        keep = jnp.logical_or(lax.broadcasted_iota(jnp.int32, dtv.shape, 0) < n_valid, i < n_raw)
        dtv = jnp.where(keep, dtv, 0.0)
    dtv_ref[...] = dtv
    a = dtv * (-jnp.exp(alog_ref[...]))
    tri = (lax.broadcasted_iota(jnp.int32, (T, T), 0) >= lax.broadcasted_iota(jnp.int32, (T, T), 1)).astype(F32)
    for r in range(tm // T):
        cs = jnp.dot(tri, a[r * T:(r + 1) * T, :], precision=lax.Precision.HIGHEST, preferred_element_type=F32)
        cs_ref[r * T:(r + 1) * T, :] = cs
        csT_ref[r] = cs.T
        dtT_ref[r] = dtv[r * T:(r + 1) * T, :].T


def _dt_proj(xn, w_dt, dtb, alog, *, tm, n_raw, n_valid):
    M, D = xn.shape
    T = CHUNK
    tm = min(tm, M)
    assert M % tm == 0 and tm % T == 0 and T == LANES and (n_valid >= tm or tm == T)
    const = lambda i: (0, 0)
    rows = pl.BlockSpec((tm, LANES), lambda i: (i, 0))
    rowsT = pl.BlockSpec((tm // T, LANES, T), lambda i: (i, 0, 0))
    return pl.pallas_call(
        functools.partial(_dt_kernel, n_raw=n_raw, n_valid=n_valid),
        grid=(M // tm,),
        in_specs=[pl.BlockSpec((tm, D), lambda i: (i, 0)),
                  pl.BlockSpec((D, LANES), const),
                  pl.BlockSpec((1, LANES), const),
                  pl.BlockSpec((1, LANES), const)],
        out_specs=[rows, rows, rowsT, rowsT],
        out_shape=[jax.ShapeDtypeStruct((M, LANES), F32), jax.ShapeDtypeStruct((M, LANES), F32),
                   jax.ShapeDtypeStruct((M // T, LANES, T), F32), jax.ShapeDtypeStruct((M // T, LANES, T), F32)],
        compiler_params=_cparams(("parallel",)),
        name="dt_proj",
    )(xn, w_dt, dtb, alog)


def _out_ffn_kernel(*refs, fc, mix, has_next):
    it = iter(refs)
    u_ref, g_ref = next(it), next(it)
    ng_ref = next(it) if mix == "ssd" else None
    x_ref, wo_ref, bo_ref, gpost_ref, gpre_ref, w1_ref, w2_ref, gfpost_ref = (next(it) for _ in range(8))
    gnext_ref = next(it) if has_next else None
    o_ref = next(it)
    xn_ref = next(it) if has_next else None
    tm = o_ref.shape[0]
    dff = w1_ref.shape[1]
    ns = FFN_ROW_SPLIT if tm % (FFN_ROW_SPLIT * 2 * SUBLANES) == 0 else 1
    R = [pl.ds(k * (tm // ns), tm // ns) for k in range(ns)]
    if mix == "ssd":
        ys = [_ssd_gate_norm(u_ref[r, :], _silu(g_ref[r, :]), ng_ref).astype(BF16) for r in R]
    else:
        ys = [(u_ref[r, :] * _gelu_tanh(g_ref[r, :])).astype(BF16) for r in R]
    ms = [_dot(y, wo_ref[...]) + bo_ref[...] for y in ys]
    x1s = [x_ref[r, :] + _rms(m, gpost_ref[...]) for r, m in zip(R, ms)]
    hns = [_rms(x1, gpre_ref[...]).astype(BF16) for x1 in x1s]
    accs = [None] * ns
    for c in range(dff // fc):
        hs = [jnp.maximum(_dot(hn, w1_ref[:, c * fc:(c + 1) * fc]), 0.0) for hn in hns]
        parts = [_dot((h * h).astype(BF16), w2_ref[c * fc:(c + 1) * fc, :]) for h in hs]
        accs = [p if a is None else a + p for a, p in zip(accs, parts)]
    for r, x1, acc in zip(R, x1s, accs):
        x2 = x1 + _rms(acc, gfpost_ref[...])
        o_ref[r, :] = x2
        if has_next:
            xn_ref[r, :] = _rms(x2, gnext_ref[...]).astype(xn_ref.dtype)


def _out_ffn(u, g, ng, x, layer, wo, wo_layer, bo, gpost, gpre, w1, w2, gfpost, gnext, *, mix, tm, fc=1024):
    M, D = x.shape
    K = u.shape[1]
    dff = w1.shape[2]
    tm = min(tm, M)
    assert M % tm == 0 and dff % fc == 0
    const = lambda i: (0, 0)
    once = dict(pipeline_mode=pl.Buffered(1))
    rows = lambda w: pl.BlockSpec((tm, w), lambda i: (i, 0))
    has_next = gnext is not None
    in_specs = [rows(K), rows(K)]
    args = [u, g]
    if mix == "ssd":
        in_specs.append(pl.BlockSpec((1, K), const))
        args.append(ng)
    in_specs += [rows(D),
                 pl.BlockSpec((None, K, D), lambda i: (wo_layer, 0, 0), **once),
                 pl.BlockSpec((1, D), const),
                 pl.BlockSpec((1, D), const),
                 pl.BlockSpec((1, D), const),
                 pl.BlockSpec((None, D, dff), lambda i: (layer, 0, 0), **once),
                 pl.BlockSpec((None, dff, D), lambda i: (layer, 0, 0), **once),
                 pl.BlockSpec((1, D), const)]
    args += [x, wo, bo, gpost, gpre, w1, w2, gfpost]
    out_specs = [rows(D)]
    out_shape = [jax.ShapeDtypeStruct((M, D), F32)]
    if has_next:
        in_specs.append(pl.BlockSpec((1, D), const))
        args.append(gnext)
        out_specs.append(rows(D))
        out_shape.append(jax.ShapeDtypeStruct((M, D), BF16))
    res = pl.pallas_call(
        functools.partial(_out_ffn_kernel, fc=fc, mix=mix, has_next=has_next),
        grid=(M // tm,),
        in_specs=in_specs, out_specs=out_specs, out_shape=out_shape,
        compiler_params=_cparams(("parallel",)),
        name="out_ffn_" + mix,
    )(*args)
    return (res[0], res[1]) if has_next else (res[0], None)


def _ssd_chunk_kernel(xbc_ref, cs_ref, csT_ref, dtT_ref, h0_ref, dexp_ref, y_ref, hout_ref, hT_s, *, T):
    nq = y_ref.shape[0]
    c = pl.program_id(1)

    @pl.when(c == 0)
    def _():
        for q in range(nq):
            for g in range(SSD_GROUPS):
                hT_s[q, g] = h0_ref[0, g].T

    for q in range(nq):
        _ssd_chunk_body(xbc_ref.at[q], cs_ref.at[q], csT_ref.at[q], dtT_ref.at[q], dexp_ref, y_ref.at[q],
                        hT_s.at[q], T=T)

    @pl.when(c == pl.num_programs(1) - 1)
    def _():
        for q in range(nq):
            for g in range(SSD_GROUPS):
                hout_ref[q, g] = hT_s[q, g].T


def _ssd_chunk_body(xbc_ref, cs_ref, csT_ref, dtT_ref, dexp_ref, y_ref, hT_s, *, T):
    d_inner = y_ref.shape[1]
    gs = SSD_GROUPS * SSD_STATE
    gw = d_inner // SSD_GROUPS

    row = lax.broadcasted_iota(jnp.int32, (T, T), 0)
    col = lax.broadcasted_iota(jnp.int32, (T, T), 1)
    causal = row >= col
    lane = lax.broadcasted_iota(jnp.int32, (T, LANES), 1)
    lo_half = lane < SSD_HEAD_DIM

    cs = cs_ref[...]
    csT = csT_ref[...]
    dtT = dtT_ref[...]
    wdT = dtT * jnp.exp(csT[:, T - 1:T] - csT)
    dec_tot = jnp.exp(cs[T - 1:T, :])

    ys = []
    for g in range(SSD_GROUPS):
        Bg = xbc_ref[:, d_inner + g * SSD_STATE:d_inner + (g + 1) * SSD_STATE]
        Cg = xbc_ref[:, d_inner + gs + g * SSD_STATE:d_inner + gs + (g + 1) * SSD_STATE].astype(BF16)
        BgT = Bg.T
        cb = _dot(Cg, BgT.astype(BF16))
        hTg = hT_s[g]
        yoff = _dot(Cg, hTg.astype(BF16))
        for jj in range(gw // LANES):
            j = g * (gw // LANES) + jj
            xpair = xbc_ref[:, j * LANES:(j + 1) * LANES]
            s_parts, bw_parts, ecs_parts = [], [], []
            for hh in (2 * j, 2 * j + 1):
                csl = jnp.broadcast_to(cs[:, hh:hh + 1], (T, T))
                L = jnp.where(causal, jnp.exp(csl - csT[hh:hh + 1, :]), 0.0)
                s_parts.append((cb * L * dtT[hh:hh + 1, :]).astype(BF16))
                bw_parts.append((BgT * wdT[hh:hh + 1, :]).astype(BF16))
                ecs_parts.append(jnp.exp(csl))
            x2 = jnp.concatenate([jnp.where(lo_half, xpair, 0.0).astype(BF16),
                                  jnp.where(lo_half, 0.0, xpair).astype(BF16)], axis=0)
            ydiag = _dot(jnp.concatenate(s_parts, axis=1), x2)
            upd = _dot(jnp.concatenate(bw_parts, axis=1), x2)
            ecs = jnp.where(lo_half, ecs_parts[0], ecs_parts[1])
            ys.append(ydiag + yoff[:, jj * LANES:(jj + 1) * LANES] * ecs
                      + xpair * dexp_ref[:, j * LANES:(j + 1) * LANES])
            dect = _pair_expand(dec_tot, j, lane[0:1, :])
            hT_s[g, :, jj * LANES:(jj + 1) * LANES] = hTg[:, jj * LANES:(jj + 1) * LANES] * dect + upd

    y_ref[...] = jnp.concatenate(ys, axis=1)


def _ssd_chunk(xbc, dtq, h0, p, *, nseq, nchunk, row_block0=0):
    T = CHUNK
    assert T == LANES and h0.shape[0] == 1 and (nseq == 1 or row_block0 == 0)
    d_inner = p["dexp"].shape[1]
    gw = d_inner // SSD_GROUPS
    _, cs, csT, dtT = dtq
    nq = SSD_SEQ_PER_STEP if nseq % SSD_SEQ_PER_STEP == 0 else 1
    ng = nseq // nq
    v4 = lambda a: a.reshape(ng, nq, a.shape[0] // nseq, a.shape[1])
    v5 = lambda a: a.reshape(ng, nq, a.shape[0] // nseq, a.shape[1], a.shape[2])
    rows = lambda w: pl.BlockSpec((None, nq, T, w), lambda b, c: (b, 0, row_block0 + c, 0))
    rowsT = pl.BlockSpec((None, nq, None, LANES, T), lambda b, c: (b, 0, row_block0 + c, 0, 0))
    hshape = (SSD_GROUPS, gw, SSD_STATE)
    y, hout = pl.pallas_call(
        functools.partial(_ssd_chunk_kernel, T=T),
        grid=(ng, nchunk),
        in_specs=[rows(xbc.shape[1]), rows(LANES), rowsT, rowsT,
                  pl.BlockSpec((1,) + hshape, lambda b, c: (0, 0, 0, 0)),
                  pl.BlockSpec((1, d_inner), lambda b, c: (0, 0))],
        out_specs=[pl.BlockSpec((None, nq, T, d_inner), lambda b, c: (b, 0, c, 0)),
                   pl.BlockSpec((None, nq) + hshape, lambda b, c: (b, 0, 0, 0, 0))],
        out_shape=[jax.ShapeDtypeStruct((ng, nq, nchunk * T, d_inner), F32),
                   jax.ShapeDtypeStruct((ng, nq) + hshape, F32)],
        scratch_shapes=[pltpu.VMEM((nq, SSD_GROUPS, SSD_STATE, gw), F32)],
        compiler_params=_cparams(("parallel", "arbitrary")),
        name="ssd_chunk",
    )(v4(xbc), v4(cs), v5(csT), v5(dtT), h0, p["dexp"])
    return y.reshape(nseq * nchunk * T, d_inner), hout.reshape((nseq,) + hshape)


def _ssd_step_kernel(*refs, seq_per_step, has_prev):
    xbc_ref, dt_ref, c0_ref, c1_ref, c2_ref, h_ref, cw_ref, cb_ref, alog_ref, dexp_ref = refs[:10]
    y_ref, hout_ref, xdtT_s, decT_s, b_s, c_s, xs_s, yrow_s = refs[10 + has_prev:]
    s = pl.program_id(0)
    nb, d_inner = xs_s.shape
    gs = SSD_GROUPS * SSD_STATE
    gw = d_inner // SSD_GROUPS

    @pl.when(s == 0)
    def _():
        xbc = _silu(_conv_step(xbc_ref[...], (c0_ref[...], c1_ref[...], c2_ref[...]), cw_ref, cb_ref))
        xs = xbc[:, :d_inner]
        xs_s[...] = xs
        b_s[...] = xbc[:, d_inner:d_inner + gs]
        c_s[...] = xbc[:, d_inner + gs:]
        dtv = dt_ref[...]
        decT_s[...] = jnp.exp(dtv * (-jnp.exp(alog_ref[...]))).T
        dtT = dtv.T
        for j in range(d_inner // LANES):
            xT = xs[:, j * LANES:(j + 1) * LANES].T
            for k in range(2):
                hh = 2 * j + k
                xdtT_s[hh * SSD_HEAD_DIM:(hh + 1) * SSD_HEAD_DIM, :] = (
                    xT[k * SSD_HEAD_DIM:(k + 1) * SSD_HEAD_DIM, :] * dtT[hh:hh + 1, :])

    for q in range(seq_per_step):
        b = s * seq_per_step + q
        sel = lax.broadcasted_iota(jnp.int32, (1, nb), 1) == b
        dec_col = jnp.sum(jnp.where(sel, decT_s[...], 0.0), axis=1, keepdims=True)
        brow = b_s[pl.ds(b, 1), :]
        crow = c_s[pl.ds(b, 1), :]
        for g in range(SSD_GROUPS):
            xcol = jnp.sum(jnp.where(sel, xdtT_s[g * gw:(g + 1) * gw, :], 0.0), axis=1, keepdims=True)
            dcol = jnp.concatenate(
                [jnp.broadcast_to(dec_col[hh:hh + 1, :], (SSD_HEAD_DIM, 1))
                 for hh in range(g * (gw // SSD_HEAD_DIM), (g + 1) * (gw // SSD_HEAD_DIM))], axis=0)
            hn = h_ref[q, g] * dcol + xcol * brow[:, g * SSD_STATE:(g + 1) * SSD_STATE]
            hout_ref[q, g] = hn
            c8 = jnp.broadcast_to(crow[:, g * SSD_STATE:(g + 1) * SSD_STATE], (SUBLANES, SSD_STATE)).astype(BF16)
            yg = _dot_nt(c8, hn.astype(BF16))
            yrow_s[pl.ds(b, 1), g * gw:(g + 1) * gw] = yg[0:1, :]

    @pl.when(s == pl.num_programs(0) - 1)
    def _():
        y_ref[...] = yrow_s[...] + xs_s[...] * dexp_ref[...]


def _ssd_step(xbc, dt, conv_state, h_all, h_out_prev, p, *, nb, layer, seq_per_step=4):
    d_inner = p["dexp"].shape[1]
    cdim = p["conv_w"].shape[1]
    gw = d_inner // SSD_GROUPS
    gs = SSD_GROUPS * SSD_STATE
    assert nb % seq_per_step == 0
    blk0 = layer * (nb // seq_per_step)
    const = lambda s: (0, 0)
    prev = lambda k: pl.BlockSpec((nb, cdim), lambda s: (0, k))
    conv_state = conv_state.reshape(nb, -1)
    hspec = pl.BlockSpec((seq_per_step, SSD_GROUPS, gw, SSD_STATE), lambda s: (blk0 + s, 0, 0, 0))
    has_prev = h_out_prev is not None
    in_specs = [pl.BlockSpec((nb, cdim), const),
                pl.BlockSpec((nb, LANES), const),
                prev(0), prev(1), prev(2),
                hspec,
                pl.BlockSpec((CONV_WIDTH, cdim), const),
                pl.BlockSpec((1, cdim), const),
                pl.BlockSpec((1, LANES), const),
                pl.BlockSpec((1, d_inner), const)]
    args = [xbc, dt, conv_state, conv_state, conv_state, h_all, p["conv_w"], p["conv_b"], p["a_log"], p["dexp"]]
    if has_prev:
        in_specs.append(pl.BlockSpec(memory_space=pl.ANY))
        args.append(h_out_prev)
    return pl.pallas_call(
        functools.partial(_ssd_step_kernel, seq_per_step=seq_per_step, has_prev=has_prev),
        grid=(nb // seq_per_step,),
        in_specs=in_specs,
        out_specs=[pl.BlockSpec((nb, d_inner), const), hspec],
        out_shape=[jax.ShapeDtypeStruct((nb, d_inner), F32),
                   jax.ShapeDtypeStruct(h_all.shape, F32)],
        scratch_shapes=[pltpu.VMEM((d_inner, nb), F32),
                        pltpu.VMEM((LANES, nb), F32),
                        pltpu.VMEM((nb, gs), F32),
                        pltpu.VMEM((nb, gs), F32),
                        pltpu.VMEM((nb, d_inner), F32),
                        pltpu.VMEM((nb, d_inner), F32)],
        input_output_aliases={len(args) - 1: 1} if has_prev else {},
        compiler_params=_cparams(("arbitrary",)),
        name="ssd_step",
    )(*args)


def _shift_rows(x, s, fill, row):
    if s % SUBLANES == 0:
        return jnp.concatenate([jnp.full((s, x.shape[1]), fill, x.dtype), x[:x.shape[0] - s, :]], axis=0)
    return jnp.where(row >= s, pltpu.roll(x, s, 0), fill)


def _lru_chunk_kernel(xr_ref, h0_ref, wax_ref, ba_ref, bx_ref, lam_ref, y_ref, hout_ref, hc_s, *, T, n_valid):
    c = pl.program_id(1)

    @pl.when(c == 0)
    def _():
        hc_s[...] = h0_ref[0]

    a, bt = _lru_gates(xr_ref[...], wax_ref, ba_ref, bx_ref, lam_ref)
    row = lax.broadcasted_iota(jnp.int32, a.shape, 0)
    if n_valid < T:
        a = jnp.where(row < n_valid, a, 1.0)
        bt = jnp.where(row < n_valid, bt, 0.0)
    s = 1
    while s < T:
        bt = a * _shift_rows(bt, s, 0.0, row) + bt
        a = a * _shift_rows(a, s, 1.0, row)
        s *= 2
    h = a * hc_s[...] + bt
    hc_s[...] = h[T - 1:T, :]
    y_ref[...] = h

    @pl.when(c == pl.num_programs(1) - 1)
    def _():
        hout_ref[0] = h[T - 1:T, :]


def _lru_chunk(xr, h0, p, *, nseq, nchunk, row_block0, n_valid):
    T = CHUNK
    d = xr.shape[1]
    bw = d // LRU_BLOCKS
    rows = lambda b, c: (row_block0 + b * nchunk + c, 0)
    shared_h = (lambda b, c: (0, 0, 0)) if h0.shape[0] == 1 else (lambda b, c: (b, 0, 0))
    const = lambda b, c: (0, 0)
    return pl.pallas_call(
        functools.partial(_lru_chunk_kernel, T=T, n_valid=n_valid),
        grid=(nseq, nchunk),
        in_specs=[pl.BlockSpec((T, d), rows),
                  pl.BlockSpec((1, 1, d), shared_h),
                  pl.BlockSpec((LRU_BLOCKS, bw, 2 * bw), lambda b, c: (0, 0, 0)),
                  pl.BlockSpec((1, d), const),
                  pl.BlockSpec((1, d), const),
                  pl.BlockSpec((1, d), const)],
        out_specs=[pl.BlockSpec((T, d), lambda b, c: (b * nchunk + c, 0)),
                   pl.BlockSpec((1, 1, d), lambda b, c: (b, 0, 0))],
        out_shape=[jax.ShapeDtypeStruct((nseq * nchunk * T, d), F32),
                   jax.ShapeDtypeStruct((nseq, 1, d), F32)],
        scratch_shapes=[pltpu.VMEM((1, d), F32)],
        compiler_params=_cparams(("parallel", "arbitrary")),
        name="lru_chunk",
    )(xr, h0, p["wax"], p["b_a"], p["b_x"], p["lam"])


def _lru_seq_kernel(xr_ref, conv0_ref, h0_ref, cw_ref, cb_ref, wax_ref, ba_ref, bx_ref, lam_ref,
                    y_ref, hout_ref, xc_s, prev_s, h_s):
    i = pl.program_id(0)
    nseq, Tt, d = xr_ref.shape

    @pl.when(i == 0)
    def _():
        for k in range(CONV_WIDTH - 1):
            prev_s[k] = jnp.broadcast_to(conv0_ref[0, SUBLANES - CONV_WIDTH + 1 + k:SUBLANES - CONV_WIDTH + 2 + k, :],
                                         (nseq, d))
        h_s[...] = jnp.broadcast_to(h0_ref[0], (nseq, d))

    taps = [prev_s[k] for k in range(CONV_WIDTH - 1)]
    for t in range(Tt):
        x_t = xr_ref[:, t, :]
        c_t = cb_ref[...] + cw_ref[CONV_WIDTH - 1:CONV_WIDTH, :] * x_t
        for k in range(CONV_WIDTH - 1):
            c_t = c_t + cw_ref[k:k + 1, :] * taps[k]
        xc_s[t * nseq:(t + 1) * nseq, :] = c_t
        taps = taps[1:] + [x_t]
    for k in range(CONV_WIDTH - 1):
        prev_s[k] = taps[k]

    a, bt = _lru_gates(xc_s[...], wax_ref, ba_ref, bx_ref, lam_ref)
    h = h_s[...]
    for t in range(Tt):
        h = a[t * nseq:(t + 1) * nseq, :] * h + bt[t * nseq:(t + 1) * nseq, :]
        y_ref[:, t, :] = h
    h_s[...] = h

    @pl.when(i == pl.num_programs(0) - 1)
    def _():
        hout_ref[...] = h


def _lru_seq(gx, conv0, h0, p, *, nseq, tt):
    M, d2 = gx.shape
    d = d2 // 2
    S = M // nseq
    bw = d // LRU_BLOCKS
    tt = min(tt, S)
    assert S % tt == 0 and nseq == SUBLANES and tt % SUBLANES == 0
    const = lambda i: (0, 0)
    y, hout = pl.pallas_call(
        _lru_seq_kernel,
        grid=(S // tt,),
        in_specs=[pl.BlockSpec((nseq, tt, d), lambda i: (0, i, 1)),
                  pl.BlockSpec((1, SUBLANES, d), lambda i: (0, 0, 0)),
                  pl.BlockSpec((1, 1, d), lambda i: (0, 0, 0)),
                  pl.BlockSpec((CONV_WIDTH, d), const),
                  pl.BlockSpec((1, d), const),
                  pl.BlockSpec((LRU_BLOCKS, bw, 2 * bw), lambda i: (0, 0, 0)),
                  pl.BlockSpec((1, d), const),
                  pl.BlockSpec((1, d), const),
                  pl.BlockSpec((1, d), const)],
        out_specs=[pl.BlockSpec((nseq, tt, d), lambda i: (0, i, 0)), pl.BlockSpec((nseq, d), const)],
        out_shape=[jax.ShapeDtypeStruct((nseq, S, d), F32), jax.ShapeDtypeStruct((nseq, d), F32)],
        scratch_shapes=[pltpu.VMEM((tt * nseq, d), F32),
                        pltpu.VMEM((CONV_WIDTH - 1, nseq, d), F32),
                        pltpu.VMEM((nseq, d), F32)],
        compiler_params=_cparams(("arbitrary",)),
        name="lru_seq",
    )(gx.reshape(nseq, S, d2), conv0, h0, p["conv_w"], p["conv_b"], p["wax"], p["b_a"], p["b_x"], p["lam"])
    return y.reshape(M, d), hout


def _lru_step_kernel(xr_ref, c0_ref, c1_ref, c2_ref, h0_ref, cw_ref, cb_ref, wax_ref, ba_ref, bx_ref, lam_ref, hout_ref):
    xr = _conv_step(xr_ref[...], (c0_ref[...], c1_ref[...], c2_ref[...]), cw_ref, cb_ref)
    a, bt = _lru_gates(xr, wax_ref, ba_ref, bx_ref, lam_ref)
    hout_ref[...] = a * h0_ref[...] + bt


def _lru_step(xr, conv_state, h0, p, *, nb):
    d = xr.shape[1]
    bw = d // LRU_BLOCKS
    const = lambda i: (0, 0)
    prev = lambda k: pl.BlockSpec((nb, d), lambda i: (0, k))
    conv_state = conv_state.reshape(nb, -1)
    return pl.pallas_call(
        _lru_step_kernel,
        grid=(1,),
        in_specs=[pl.BlockSpec((nb, d), const),
                  prev(0), prev(1), prev(2),
                  pl.BlockSpec((nb, d), const),
                  pl.BlockSpec((CONV_WIDTH, d), const),
                  pl.BlockSpec((1, d), const),
                  pl.BlockSpec((LRU_BLOCKS, bw, 2 * bw), lambda i: (0, 0, 0)),
                  pl.BlockSpec((1, d), const),
                  pl.BlockSpec((1, d), const),
                  pl.BlockSpec((1, d), const)],
        out_specs=pl.BlockSpec((nb, d), const),
        out_shape=jax.ShapeDtypeStruct((nb, d), F32),
        compiler_params=_cparams(("arbitrary",)),
        name="lru_step",
    )(xr, conv_state, conv_state, conv_state, h0, p["conv_w"], p["conv_b"], p["wax"], p["b_a"], p["b_x"], p["lam"])


def _row(v):
    return v.reshape(1, -1).astype(F32)


def _pad_lanes(v):
    return jnp.pad(v.astype(F32), (0, LANES - v.shape[0])).reshape(1, LANES)


def kernel(x_prompt, x_sample, state_ssd_conv, state_ssd_h, state_lru_conv, state_lru_h, meta_tokens, norm_mix_pre, norm_mix_post, norm_ffn_pre, norm_ffn_post, ssd_w_in, ssd_conv_w, ssd_conv_b, ssd_dt_bias, ssd_a_log, ssd_d, ssd_norm, ssd_w_out, lru_w_in, lru_b_in, lru_conv_w, lru_conv_b, lru_w_a, lru_b_a, lru_w_x, lru_b_x, lru_lambda, lru_w_out, lru_b_out, ffn_w1, ffn_w2):
    B, S, D = x_prompt.shape
    nb = x_sample.shape[0]
    n_meta = meta_tokens.shape[0]
    depth = norm_mix_pre.shape[0]
    T = CHUNK
    n_heads = ssd_dt_bias.shape[1]
    d_inner = n_heads * SSD_HEAD_DIM
    cdim = ssd_conv_w.shape[2]
    d_rnn = lru_conv_w.shape[2]
    gw = d_inner // SSD_GROUPS
    tm_p = min(PROJ_TM, S)
    assert x_sample.shape[1] == 1 and nb == T and S % tm_p == 0 and tm_p % T == 0 and B == SUBLANES
    assert SUBLANES <= n_meta <= T and n_meta % SUBLANES == 0
    nchunk = S // T
    meta_blk = nb // T

    xs = jnp.concatenate([x_sample[:, 0, :], meta_tokens, jnp.zeros((T - n_meta, D), F32)], axis=0)
    xp = x_prompt.reshape(B * S, D)
    xn_s = _norm_bf16(xs, _row(norm_mix_pre[0]), tm=PROJ_TM)
    xn_p = _norm_bf16(xp, _row(norm_mix_pre[0]), tm=PROJ_TM)

    small = dict(tm=T, n_raw=meta_blk, n_valid=n_meta,
                 conv=dict(tm=T, n_raw=meta_blk, tiles_per_seq=1, tail_lo=n_meta - SUBLANES))
    prompt = dict(tm=tm_p, n_raw=0, n_valid=tm_p,
                  conv=dict(tm=tm_p, n_raw=0, tiles_per_seq=S // tm_p, tail_lo=tm_p - SUBLANES))
    last3 = lambda tail: tail.reshape(B, S // tm_p, SUBLANES, -1)[:, -1, SUBLANES - (CONV_WIDTH - 1):, :]

    w1_all, w2_all = ffn_w1.astype(BF16), ffn_w2.astype(BF16)
    ssd_wo_all, lru_wo_all = ssd_w_out.astype(BF16), lru_w_out.astype(BF16)

    p_ssd_conv, p_ssd_h, p_lru_conv, p_lru_h = [], [], [], []
    s_ssd_conv, s_lru_conv, s_lru_h = [], [], []
    s_ssd_h = None
    for i in range(depth):
        j = i // 2
        if i % 2 == 0:
            w_dt = jnp.pad(ssd_w_in[j][:, d_inner + cdim:], ((0, 0), (0, LANES - n_heads)))
            p = dict(conv_w=ssd_conv_w[j], conv_b=_row(ssd_conv_b[j]), dt_bias=_pad_lanes(ssd_dt_bias[j]),
                     a_log=_pad_lanes(ssd_a_log[j]), dexp=_row(jnp.repeat(ssd_d[j], SSD_HEAD_DIM)))
            mix, ng, wo_all, b_out = "ssd", _row(ssd_norm[j]), ssd_wo_all, jnp.zeros((1, D), F32)

            def run_proj(xn, conv0, v):
                xbc, tail = _proj(xn, ssd_w_in, j, d_inner, cdim, "conv_silu", tn=PROJ_TN,
                                  conv=(p["conv_w"], p["conv_b"], conv0), **v["conv"])
                z = _proj(xn, ssd_w_in, j, 0, d_inner, "none", tm=v["tm"], tn=PROJ_TN)
                dtq = _dt_proj(xn, w_dt, p["dt_bias"], p["a_log"], tm=v["tm"], n_raw=v["n_raw"],
                               n_valid=v["n_valid"])
                return xbc, tail, z, dtq

            xbc_s, tail_s, g_s, dtq_s = run_proj(xn_s, jnp.zeros((1, SUBLANES, cdim), F32), small)
            u_samp, s_ssd_h = _ssd_step(xbc_s, dtq_s[0], state_ssd_conv[j],
                                        state_ssd_h.reshape(-1, SSD_GROUPS, gw, SSD_STATE), s_ssd_h, p,
                                        nb=nb, layer=j)
            u_meta, h_meta = _ssd_chunk(xbc_s, dtq_s, jnp.zeros((1, SSD_GROUPS, gw, SSD_STATE), F32), p,
                                        nseq=1, nchunk=1, row_block0=meta_blk)
            conv_meta = tail_s[meta_blk:meta_blk + 1]
            s_ssd_conv.append(jnp.concatenate([state_ssd_conv[j][:, 1:], xbc_s[:nb, None, :]], axis=1))
            u_s = jnp.concatenate([u_samp, u_meta], axis=0)

            xbc_p, tail_p, g_p, dtq_p = run_proj(xn_p, conv_meta, prompt)
            u_p, h_p = _ssd_chunk(xbc_p, dtq_p, h_meta, p, nseq=B, nchunk=nchunk)
            p_ssd_conv.append(last3(tail_p))
            p_ssd_h.append(h_p.reshape(B, n_heads, SSD_HEAD_DIM, SSD_STATE))
        else:
            p = dict(conv_w=lru_conv_w[j], conv_b=_row(lru_conv_b[j]),
                     wax=jnp.concatenate([lru_w_a[j], lru_w_x[j]], axis=-1).astype(BF16),
                     b_a=_row(lru_b_a[j]), b_x=_row(lru_b_x[j]), lam=_row(lru_lambda[j]))
            mix, ng, wo_all, b_out = "lru", None, lru_wo_all, _row(lru_b_out[j])
            b_in = _row(lru_b_in[j])

            g_s = _proj(xn_s, lru_w_in, j, 0, d_rnn, "none", tm=T, tn=PROJ_TN, bias=b_in)
            xr_s, tail_s = _proj(xn_s, lru_w_in, j, d_rnn, d_rnn, "conv", tn=PROJ_TN, bias=b_in,
                                 conv=(p["conv_w"], p["conv_b"], jnp.zeros((1, SUBLANES, d_rnn), F32)),
                                 **small["conv"])
            h_samp = _lru_step(xr_s, state_lru_conv[j], state_lru_h[j], p, nb=nb)
            u_meta, h_meta = _lru_chunk(xr_s, jnp.zeros((1, 1, d_rnn), F32), p,
                                        nseq=1, nchunk=1, row_block0=meta_blk, n_valid=n_meta)
            conv_meta = tail_s[meta_blk:meta_blk + 1]
            s_lru_conv.append(jnp.concatenate([state_lru_conv[j][:, 1:], xr_s[:nb, None, :]], axis=1))
            s_lru_h.append(h_samp)
            u_s = jnp.concatenate([h_samp, u_meta], axis=0)

            g_p = _proj(xn_p, lru_w_in, j, 0, 2 * d_rnn, "none", tm=tm_p, tn=PROJ_TN, bias=b_in)
            u_p, h_p = _lru_seq(g_p, conv_meta, h_meta, p, nseq=B, tt=LRU_TT)
            p_lru_conv.append(g_p.reshape(B, S, 2 * d_rnn)[:, S - (CONV_WIDTH - 1):, d_rnn:])
            p_lru_h.append(h_p)

        g_next = _row(norm_mix_pre[i + 1]) if i + 1 < depth else None
        ffn = (i, wo_all, j, b_out, _row(norm_mix_post[i]), _row(norm_ffn_pre[i]), w1_all, w2_all,
               _row(norm_ffn_post[i]), g_next)
        xs, xn_s = _out_ffn(u_s, g_s, ng, xs, *ffn, mix=mix, tm=FFN_TM)
        xp, xn_p = _out_ffn(u_p, g_p, ng, xp, *ffn, mix=mix, tm=FFN_TM)

    return (xp.reshape(B, S, D), xs[:nb].reshape(nb, 1, D),
            jnp.stack(p_ssd_conv), jnp.stack(p_ssd_h), jnp.stack(p_lru_conv), jnp.stack(p_lru_h),
            jnp.stack(s_ssd_conv), s_ssd_h.reshape(state_ssd_h.shape), jnp.stack(s_lru_conv), jnp.stack(s_lru_h))
```

```python
import functools
import math

import jax
import jax.numpy as jnp
from jax import lax
from jax.experimental import pallas as pl
from jax.experimental.pallas import tpu as pltpu

F32 = jnp.float32
BF16 = jnp.bfloat16

EPS = 1e-6
CONV_WIDTH = 4
SSD_HEAD_DIM = 64
SSD_GROUPS = 8
SSD_STATE = 128
LRU_BLOCKS = 8
LRU_C = 8.0
CHUNK = 128
LANES = 128
SUBLANES = 8
MXU_COLS = 256
VMEM_LIMIT = 56 * 1024 * 1024
PROJ_TM = 1024
PROJ_TN = 1024
FFN_TM = 512
LRU_TT = 64
SSD_SEQ_PER_STEP = 4
FFN_ROW_SPLIT = 2


def _cparams(sem):
    return pltpu.CompilerParams(dimension_semantics=sem, vmem_limit_bytes=VMEM_LIMIT)


def _rms(x, g):
    return x * lax.rsqrt(jnp.mean(x * x, axis=-1, keepdims=True) + EPS) * g


def _softplus(x):
    return jnp.maximum(x, 0.0) + jnp.log1p(jnp.exp(-jnp.abs(x)))


def _silu(x):
    return x * jax.nn.sigmoid(x)


def _gelu_tanh(x):
    return 0.5 * x * (1.0 + jnp.tanh(math.sqrt(2.0 / math.pi) * (x + 0.044715 * (x * x * x))))


def _dot(a, b):
    return jnp.dot(a, b, preferred_element_type=F32)


def _dot_nt(a, b):
    return lax.dot_general(a, b, (((1,), (1,)), ((), ())), preferred_element_type=F32)


def _pair_expand(q, j, lane):
    return jnp.where(lane < SSD_HEAD_DIM, q[:, 2 * j:2 * j + 1], q[:, 2 * j + 1:2 * j + 2])


def _conv_rows(x, prev8, cw, cb):
    row8 = lax.broadcasted_iota(jnp.int32, prev8.shape, 0)
    out = cb + cw[CONV_WIDTH - 1:CONV_WIDTH, :] * x
    for k in range(1, CONV_WIDTH):
        r = pltpu.roll(x, k, 0)
        head = jnp.where(row8 < k, pltpu.roll(prev8, k, 0), r[0:SUBLANES, :])
        shifted = jnp.concatenate([head, r[SUBLANES:, :]], axis=0)
        out = out + cw[CONV_WIDTH - 1 - k:CONV_WIDTH - k, :] * shifted
    return out


def _conv_step(x, prevs, cw_ref, cb_ref):
    out = cb_ref[...] + cw_ref[CONV_WIDTH - 1:CONV_WIDTH, :] * x
    for k in range(CONV_WIDTH - 1):
        out = out + cw_ref[k:k + 1, :] * prevs[k]
    return out


def _ssd_gate_norm(y, zs, ng_ref):
    y = y * zs
    gw = y.shape[1] // SSD_GROUPS
    parts = []
    for g in range(SSD_GROUPS):
        seg = y[:, g * gw:(g + 1) * gw]
        parts.append(seg * lax.rsqrt(jnp.mean(seg * seg, axis=-1, keepdims=True) + EPS))
    return jnp.concatenate(parts, axis=1) * ng_ref[...]


def _lru_gates(xr, wax_ref, ba_ref, bx_ref, lam_ref):
    bw = xr.shape[1] // LRU_BLOCKS
    ra, ix = [], []
    for k in range(LRU_BLOCKS):
        g = _dot(xr[:, k * bw:(k + 1) * bw].astype(BF16), wax_ref[k])
        ra.append(g[:, :bw])
        ix.append(g[:, bw:])
    r = jax.nn.sigmoid(jnp.concatenate(ra, axis=1) + ba_ref[...])
    i = jax.nn.sigmoid(jnp.concatenate(ix, axis=1) + bx_ref[...])
    log_a = (-LRU_C) * r * _softplus(-lam_ref[...])
    a = jnp.exp(log_a)
    v = -jnp.tanh(log_a) * (a * a + 1.0)
    mult = jnp.where(v > 0.0, v * lax.rsqrt(v), 0.0)
    return a, mult * i * xr


def _norm_kernel(x_ref, g_ref, o_ref):
    o_ref[...] = _rms(x_ref[...], g_ref[...]).astype(o_ref.dtype)


def _norm_bf16(x, g, *, tm):
    M, D = x.shape
    tm = min(tm, M)
    assert M % tm == 0
    return pl.pallas_call(
        _norm_kernel,
        grid=(M // tm,),
        in_specs=[pl.BlockSpec((tm, D), lambda i: (i, 0)), pl.BlockSpec((1, D), lambda i: (0, 0))],
        out_specs=pl.BlockSpec((tm, D), lambda i: (i, 0)),
        out_shape=jax.ShapeDtypeStruct((M, D), BF16),
        compiler_params=_cparams(("parallel",)),
        name="norm",
    )(x, g)


def _proj_kernel(*refs, mode, has_bias, n_raw, tiles_per_seq, tail_lo):
    has_conv = mode in ("conv", "conv_silu")
    it = iter(refs)
    xn_ref, w_ref = next(it), next(it)
    b_ref = next(it) if has_bias else None
    cw_ref, cb_ref, conv0_ref = (next(it), next(it), next(it)) if has_conv else (None, None, None)
    o_ref = next(it)
    tail_ref = next(it) if has_conv else None
    wb_s = next(it)
    carry_s = next(it) if has_conv else None
    i = pl.program_id(1)

    @pl.when(i == 0)
    def _():
        wb_s[...] = w_ref[...].astype(BF16)

    tm, tn = o_ref.shape
    for n in range(tn // MXU_COLS):
        sl = pl.ds(n * MXU_COLS, MXU_COLS)
        acc = _dot(xn_ref[...], wb_s[:, sl])
        if has_bias:
            acc = acc + b_ref[:, sl]
        if not has_conv:
            o_ref[:, sl] = acc
            continue
        tail_ref[0, :, sl] = acc[tail_lo:tail_lo + SUBLANES, :]

        def conv_tile(acc=acc, sl=sl):
            start = lax.rem(i - n_raw, tiles_per_seq) == 0
            prev8 = jnp.where(start, conv0_ref[0, :, sl], carry_s[:, sl])
            out = _conv_rows(acc, prev8, cw_ref[:, sl], cb_ref[:, sl])
            carry_s[:, sl] = acc[tm - SUBLANES:tm, :]
            o_ref[:, sl] = _silu(out) if mode == "conv_silu" else out

        if n_raw == 0:
            conv_tile()
        else:
            @pl.when(i < n_raw)
            def _(acc=acc, sl=sl):
                o_ref[:, sl] = acc

            pl.when(i >= n_raw)(conv_tile)


def _proj(xn, w, layer, col0, n_out, mode, *, tm, tn, bias=None, conv=None, n_raw=0, tiles_per_seq=1,
          tail_lo=0):
    M, D = xn.shape
    tm = min(tm, M)
    assert M % tm == 0 and n_out % tn == 0 and col0 % tn == 0 and tn % MXU_COLS == 0
    assert (conv is not None) == (mode in ("conv", "conv_silu"))
    n_i, n_j, jb = M // tm, n_out // tn, col0 // tn
    in_specs = [pl.BlockSpec((tm, D), lambda j, i: (i, 0)),
                pl.BlockSpec((D, tn), lambda j, i: (layer, jb + j))]
    args = [xn, w.reshape(-1, w.shape[-1])]
    if bias is not None:
        in_specs.append(pl.BlockSpec((1, tn), lambda j, i: (0, jb + j)))
        args.append(bias)
    out_shape = [jax.ShapeDtypeStruct((M, n_out), F32)]
    out_specs = [pl.BlockSpec((tm, tn), lambda j, i: (i, j))]
    scratch = [pltpu.VMEM((D, tn), BF16)]
    if conv is not None:
        cw, cb, conv0 = conv
        in_specs += [pl.BlockSpec((CONV_WIDTH, tn), lambda j, i: (0, j)),
                     pl.BlockSpec((1, tn), lambda j, i: (0, j)),
                     pl.BlockSpec((1, SUBLANES, tn), lambda j, i: (0, 0, j))]
        args += [cw, cb, conv0]
        out_shape.append(jax.ShapeDtypeStruct((n_i, SUBLANES, n_out), F32))
        out_specs.append(pl.BlockSpec((1, SUBLANES, tn), lambda j, i: (i, 0, j)))
        scratch.append(pltpu.VMEM((SUBLANES, tn), F32))
    res = pl.pallas_call(
        functools.partial(_proj_kernel, mode=mode, has_bias=bias is not None, n_raw=n_raw,
                          tiles_per_seq=tiles_per_seq, tail_lo=tail_lo),
        grid=(n_j, n_i),
        in_specs=in_specs, out_specs=out_specs, out_shape=out_shape, scratch_shapes=scratch,
        compiler_params=_cparams(("arbitrary", "arbitrary")),
        name="proj_" + mode,
    )(*args)
    return res if conv is not None else res[0]


def _dt_kernel(xn_ref, w_ref, dtb_ref, alog_ref, dtv_ref, cs_ref, csT_ref, dtT_ref, *, n_raw, n_valid):
    i = pl.program_id(0)
    tm = xn_ref.shape[0]
    T = CHUNK
    dtv = _softplus(_dot(xn_ref[...], w_ref[...].astype(BF16)) + dtb_ref[...])
    if n_valid < tm:
        keep = jnp.logical_or(lax.broadcasted_iota(jnp.int32, dtv.shape, 0) < n_valid, i < n_raw)
        dtv = jnp.where(keep, dtv, 0.0)
    dtv_ref[...] = dtv
    a = dtv * (-jnp.exp(alog_ref[...]))
    tri = (lax.broadcasted_iota(jnp.int32, (T, T), 0) >= lax.broadcasted_iota(jnp.int32, (T, T), 1)).astype(F32)
    for r in range(tm // T):
        cs = jnp.dot(tri, a[r * T:(r + 1) * T, :], precision=lax.Precision.HIGHEST, preferred_element_type=F32)
        cs_ref[r * T:(r + 1) * T, :] = cs
        csT_ref[r] = cs.T
        dtT_ref[r] = dtv[r * T:(r + 1) * T, :].T


def _dt_proj(xn, w_dt, dtb, alog, *, tm, n_raw, n_valid):
    M, D = xn.shape
    T = CHUNK
    tm = min(tm, M)
    assert M % tm == 0 and tm % T == 0 and T == LANES and (n_valid >= tm or tm == T)
    const = lambda i: (0, 0)
    rows = pl.BlockSpec((tm, LANES), lambda i: (i, 0))
    rowsT = pl.BlockSpec((tm // T, LANES, T), lambda i: (i, 0, 0))
    return pl.pallas_call(
        functools.partial(_dt_kernel, n_raw=n_raw, n_valid=n_valid),
        grid=(M // tm,),
        in_specs=[pl.BlockSpec((tm, D), lambda i: (i, 0)),
                  pl.BlockSpec((D, LANES), const),
                  pl.BlockSpec((1, LANES), const),
                  pl.BlockSpec((1, LANES), const)],
        out_specs=[rows, rows, rowsT, rowsT],
        out_shape=[jax.ShapeDtypeStruct((M, LANES), F32), jax.ShapeDtypeStruct((M, LANES), F32),
                   jax.ShapeDtypeStruct((M // T, LANES, T), F32), jax.ShapeDtypeStruct((M // T, LANES, T), F32)],
        compiler_params=_cparams(("parallel",)),
        name="dt_proj",
    )(xn, w_dt, dtb, alog)


def _out_ffn_kernel(*refs, fc, mix, has_next):
    it = iter(refs)
    u_ref, g_ref = next(it), next(it)
    ng_ref = next(it) if mix == "ssd" else None
    x_ref, wo_ref, bo_ref, gpost_ref, gpre_ref, w1_ref, w2_ref, gfpost_ref = (next(it) for _ in range(8))
    gnext_ref = next(it) if has_next else None
    o_ref = next(it)
    xn_ref = next(it) if has_next else None
    tm = o_ref.shape[0]
    dff = w1_ref.shape[1]
    ns = FFN_ROW_SPLIT if tm % (FFN_ROW_SPLIT * 2 * SUBLANES) == 0 else 1
    R = [pl.ds(k * (tm // ns), tm // ns) for k in range(ns)]
    if mix == "ssd":
        ys = [_ssd_gate_norm(u_ref[r, :], _silu(g_ref[r, :]), ng_ref).astype(BF16) for r in R]
    else:
        ys = [(u_ref[r, :] * _gelu_tanh(g_ref[r, :])).astype(BF16) for r in R]
    ms = [_dot(y, wo_ref[...]) + bo_ref[...] for y in ys]
    x1s = [x_ref[r, :] + _rms(m, gpost_ref[...]) for r, m in zip(R, ms)]
    hns = [_rms(x1, gpre_ref[...]).astype(BF16) for x1 in x1s]
    accs = [None] * ns
    for c in range(dff // fc):
        hs = [jnp.maximum(_dot(hn, w1_ref[:, c * fc:(c + 1) * fc]), 0.0) for hn in hns]
        parts = [_dot((h * h).astype(BF16), w2_ref[c * fc:(c + 1) * fc, :]) for h in hs]
        accs = [p if a is None else a + p for a, p in zip(accs, parts)]
    for r, x1, acc in zip(R, x1s, accs):
        x2 = x1 + _rms(acc, gfpost_ref[...])
        o_ref[r, :] = x2
        if has_next:
            xn_ref[r, :] = _rms(x2, gnext_ref[...]).astype(xn_ref.dtype)


def _out_ffn(u, g, ng, x, layer, wo, wo_layer, bo, gpost, gpre, w1, w2, gfpost, gnext, *, mix, tm, fc=1024):
    M, D = x.shape
    K = u.shape[1]
    dff = w1.shape[2]
    tm = min(tm, M)
    assert M % tm == 0 and dff % fc == 0
    const = lambda i: (0, 0)
    once = dict(pipeline_mode=pl.Buffered(1))
    rows = lambda w: pl.BlockSpec((tm, w), lambda i: (i, 0))
    has_next = gnext is not None
    in_specs = [rows(K), rows(K)]
    args = [u, g]
    if mix == "ssd":
        in_specs.append(pl.BlockSpec((1, K), const))
        args.append(ng)
    in_specs += [rows(D),
                 pl.BlockSpec((None, K, D), lambda i: (wo_layer, 0, 0), **once),
                 pl.BlockSpec((1, D), const),
                 pl.BlockSpec((1, D), const),
                 pl.BlockSpec((1, D), const),
                 pl.BlockSpec((None, D, dff), lambda i: (layer, 0, 0), **once),
                 pl.BlockSpec((None, dff, D), lambda i: (layer, 0, 0), **once),
                 pl.BlockSpec((1, D), const)]
    args += [x, wo, bo, gpost, gpre, w1, w2, gfpost]
    out_specs = [rows(D)]
    out_shape = [jax.ShapeDtypeStruct((M, D), F32)]
    if has_next:
        in_specs.append(pl.BlockSpec((1, D), const))
        args.append(gnext)
        out_specs.append(rows(D))
        out_shape.append(jax.ShapeDtypeStruct((M, D), BF16))
    res = pl.pallas_call(
        functools.partial(_out_ffn_kernel, fc=fc, mix=mix, has_next=has_next),
        grid=(M // tm,),
        in_specs=in_specs, out_specs=out_specs, out_shape=out_shape,
        compiler_params=_cparams(("parallel",)),
        name="out_ffn_" + mix,
    )(*args)
    return (res[0], res[1]) if has_next else (res[0], None)


def _ssd_chunk_kernel(xbc_ref, cs_ref, csT_ref, dtT_ref, h0_ref, dexp_ref, y_ref, hout_ref, hT_s, *, T):
    nq = y_ref.shape[0]
    c = pl.program_id(1)

    @pl.when(c == 0)
    def _():
        for q in range(nq):
            for g in range(SSD_GROUPS):
                hT_s[q, g] = h0_ref[0, g].T

    for q in range(nq):
        _ssd_chunk_body(xbc_ref.at[q], cs_ref.at[q], csT_ref.at[q], dtT_ref.at[q], dexp_ref, y_ref.at[q],
                        hT_s.at[q], T=T)

    @pl.when(c == pl.num_programs(1) - 1)
    def _():
        for q in range(nq):
            for g in range(SSD_GROUPS):
                hout_ref[q, g] = hT_s[q, g].T


def _ssd_chunk_body(xbc_ref, cs_ref, csT_ref, dtT_ref, dexp_ref, y_ref, hT_s, *, T):
    d_inner = y_ref.shape[1]
    gs = SSD_GROUPS * SSD_STATE
    gw = d_inner // SSD_GROUPS

    row = lax.broadcasted_iota(jnp.int32, (T, T), 0)
    col = lax.broadcasted_iota(jnp.int32, (T, T), 1)
    causal = row >= col
    lane = lax.broadcasted_iota(jnp.int32, (T, LANES), 1)
    lo_half = lane < SSD_HEAD_DIM

    cs = cs_ref[...]
    csT = csT_ref[...]
    dtT = dtT_ref[...]
    wdT = dtT * jnp.exp(csT[:, T - 1:T] - csT)
    dec_tot = jnp.exp(cs[T - 1:T, :])

    ys = []
    for g in range(SSD_GROUPS):
        Bg = xbc_ref[:, d_inner + g * SSD_STATE:d_inner + (g + 1) * SSD_STATE]
        Cg = xbc_ref[:, d_inner + gs + g * SSD_STATE:d_inner + gs + (g + 1) * SSD_STATE].astype(BF16)
        BgT = Bg.T
        cb = _dot(Cg, BgT.astype(BF16))
        hTg = hT_s[g]
        yoff = _dot(Cg, hTg.astype(BF16))
        for jj in range(gw // LANES):
            j = g * (gw // LANES) + jj
            xpair = xbc_ref[:, j * LANES:(j + 1) * LANES]
            s_parts, bw_parts, ecs_parts = [], [], []
            for hh in (2 * j, 2 * j + 1):
                csl = jnp.broadcast_to(cs[:, hh:hh + 1], (T, T))
                L = jnp.where(causal, jnp.exp(csl - csT[hh:hh + 1, :]), 0.0)
                s_parts.append((cb * L * dtT[hh:hh + 1, :]).astype(BF16))
                bw_parts.append((BgT * wdT[hh:hh + 1, :]).astype(BF16))
                ecs_parts.append(jnp.exp(csl))
            x2 = jnp.concatenate([jnp.where(lo_half, xpair, 0.0).astype(BF16),
                                  jnp.where(lo_half, 0.0, xpair).astype(BF16)], axis=0)
            ydiag = _dot(jnp.concatenate(s_parts, axis=1), x2)
            upd = _dot(jnp.concatenate(bw_parts, axis=1), x2)
            ecs = jnp.where(lo_half, ecs_parts[0], ecs_parts[1])
            ys.append(ydiag + yoff[:, jj * LANES:(jj + 1) * LANES] * ecs
                      + xpair * dexp_ref[:, j * LANES:(j + 1) * LANES])
            dect = _pair_expand(dec_tot, j, lane[0:1, :])
            hT_s[g, :, jj * LANES:(jj + 1) * LANES] = hTg[:, jj * LANES:(jj + 1) * LANES] * dect + upd

    y_ref[...] = jnp.concatenate(ys, axis=1)


def _ssd_chunk(xbc, dtq, h0, p, *, nseq, nchunk, row_block0=0):
    T = CHUNK
    assert T == LANES and h0.shape[0] == 1 and (nseq == 1 or row_block0 == 0)
    d_inner = p["dexp"].shape[1]
    gw = d_inner // SSD_GROUPS
    _, cs, csT, dtT = dtq
    nq = SSD_SEQ_PER_STEP if nseq % SSD_SEQ_PER_STEP == 0 else 1
    ng = nseq // nq
    v4 = lambda a: a.reshape(ng, nq, a.shape[0] // nseq, a.shape[1])
    v5 = lambda a: a.reshape(ng, nq, a.shape[0] // nseq, a.shape[1], a.shape[2])
    rows = lambda w: pl.BlockSpec((None, nq, T, w), lambda b, c: (b, 0, row_block0 + c, 0))
    rowsT = pl.BlockSpec((None, nq, None, LANES, T), lambda b, c: (b, 0, row_block0 + c, 0, 0))
    hshape = (SSD_GROUPS, gw, SSD_STATE)
    y, hout = pl.pallas_call(
        functools.partial(_ssd_chunk_kernel, T=T),
        grid=(ng, nchunk),
        in_specs=[rows(xbc.shape[1]), rows(LANES), rowsT, rowsT,
                  pl.BlockSpec((1,) + hshape, lambda b, c: (0, 0, 0, 0)),
                  pl.BlockSpec((1, d_inner), lambda b, c: (0, 0))],
        out_specs=[pl.BlockSpec((None, nq, T, d_inner), lambda b, c: (b, 0, c, 0)),
                   pl.BlockSpec((None, nq) + hshape, lambda b, c: (b, 0, 0, 0, 0))],
        out_shape=[jax.ShapeDtypeStruct((ng, nq, nchunk * T, d_inner), F32),
                   jax.ShapeDtypeStruct((ng, nq) + hshape, F32)],
        scratch_shapes=[pltpu.VMEM((nq, SSD_GROUPS, SSD_STATE, gw), F32)],
        compiler_params=_cparams(("parallel", "arbitrary")),
        name="ssd_chunk",
    )(v4(xbc), v4(cs), v5(csT), v5(dtT), h0, p["dexp"])
    return y.reshape(nseq * nchunk * T, d_inner), hout.reshape((nseq,) + hshape)


def _ssd_step_kernel(*refs, seq_per_step, has_prev):
    xbc_ref, dt_ref, c0_ref, c1_ref, c2_ref, h_ref, cw_ref, cb_ref, alog_ref, dexp_ref = refs[:10]
    y_ref, hout_ref, xdtT_s, decT_s, b_s, c_s, xs_s, yrow_s = refs[10 + has_prev:]
    s = pl.program_id(0)
    nb, d_inner = xs_s.shape
    gs = SSD_GROUPS * SSD_STATE
    gw = d_inner // SSD_GROUPS
    n_own = nb // seq_per_step

    if not has_prev:
        @pl.when(s >= n_own)
        def _():
            hout_ref[...] = jnp.zeros_like(hout_ref)

    @pl.when(s == 0)
    def _():
        xbc = _silu(_conv_step(xbc_ref[...], (c0_ref[...], c1_ref[...], c2_ref[...]), cw_ref, cb_ref))
        xs = xbc[:, :d_inner]
        xs_s[...] = xs
        b_s[...] = xbc[:, d_inner:d_inner + gs]
        c_s[...] = xbc[:, d_inner + gs:]
        dtv = dt_ref[...]
        decT_s[...] = jnp.exp(dtv * (-jnp.exp(alog_ref[...]))).T
        dtT = dtv.T
        for j in range(d_inner // LANES):
            xT = xs[:, j * LANES:(j + 1) * LANES].T
            for k in range(2):
                hh = 2 * j + k
                xdtT_s[hh * SSD_HEAD_DIM:(hh + 1) * SSD_HEAD_DIM, :] = (
                    xT[k * SSD_HEAD_DIM:(k + 1) * SSD_HEAD_DIM, :] * dtT[hh:hh + 1, :])

    pl.when(s < n_own)(functools.partial(_ssd_step_update, s, seq_per_step, h_ref, hout_ref, xdtT_s, decT_s, b_s,
                                        c_s, yrow_s))

    @pl.when(s == n_own - 1)
    def _():
        y_ref[...] = yrow_s[...] + xs_s[...] * dexp_ref[...]


def _ssd_step_update(s, seq_per_step, h_ref, hout_ref, xdtT_s, decT_s, b_s, c_s, yrow_s):
    nb, d_inner = yrow_s.shape
    gw = d_inner // SSD_GROUPS
    for q in range(seq_per_step):
        b = s * seq_per_step + q
        sel = lax.broadcasted_iota(jnp.int32, (1, nb), 1) == b
        dec_col = jnp.sum(jnp.where(sel, decT_s[...], 0.0), axis=1, keepdims=True)
        brow = b_s[pl.ds(b, 1), :]
        crow = c_s[pl.ds(b, 1), :]
        for g in range(SSD_GROUPS):
            xcol = jnp.sum(jnp.where(sel, xdtT_s[g * gw:(g + 1) * gw, :], 0.0), axis=1, keepdims=True)
            dcol = jnp.concatenate(
                [jnp.broadcast_to(dec_col[hh:hh + 1, :], (SSD_HEAD_DIM, 1))
                 for hh in range(g * (gw // SSD_HEAD_DIM), (g + 1) * (gw // SSD_HEAD_DIM))], axis=0)
            hn = h_ref[q, g] * dcol + xcol * brow[:, g * SSD_STATE:(g + 1) * SSD_STATE]
            hout_ref[q, g] = hn
            c8 = jnp.broadcast_to(crow[:, g * SSD_STATE:(g + 1) * SSD_STATE], (SUBLANES, SSD_STATE)).astype(BF16)
            yg = _dot_nt(c8, hn.astype(BF16))
            yrow_s[pl.ds(b, 1), g * gw:(g + 1) * gw] = yg[0:1, :]


def _ssd_step(xbc, dt, conv_state, h_all, h_out_prev, p, *, nb, layer, seq_per_step=4):
    d_inner = p["dexp"].shape[1]
    cdim = p["conv_w"].shape[1]
    gw = d_inner // SSD_GROUPS
    gs = SSD_GROUPS * SSD_STATE
    assert nb % seq_per_step == 0
    n_own = nb // seq_per_step
    n_all = h_all.shape[0] // seq_per_step
    blk0 = layer * n_own
    const = lambda s: (0, 0)
    prev = lambda k: pl.BlockSpec((nb, cdim), lambda s: (0, k))
    conv_state = conv_state.reshape(nb, -1)
    has_prev = h_out_prev is not None
    n_steps = n_own if has_prev else n_all
    hblock = (seq_per_step, SSD_GROUPS, gw, SSD_STATE)
    hspec_in = pl.BlockSpec(hblock, lambda s: (blk0 + jnp.minimum(s, n_own - 1), 0, 0, 0))
    hspec_out = pl.BlockSpec(hblock, lambda s: (lax.rem(blk0 + s, n_all), 0, 0, 0))
    in_specs = [pl.BlockSpec((nb, cdim), const),
                pl.BlockSpec((nb, LANES), const),
                prev(0), prev(1), prev(2),
                hspec_in,
                pl.BlockSpec((CONV_WIDTH, cdim), const),
                pl.BlockSpec((1, cdim), const),
                pl.BlockSpec((1, LANES), const),
                pl.BlockSpec((1, d_inner), const)]
    args = [xbc, dt, conv_state, conv_state, conv_state, h_all, p["conv_w"], p["conv_b"], p["a_log"], p["dexp"]]
    if has_prev:
        in_specs.append(pl.BlockSpec(memory_space=pl.ANY))
        args.append(h_out_prev)
    return pl.pallas_call(
        functools.partial(_ssd_step_kernel, seq_per_step=seq_per_step, has_prev=has_prev),
        grid=(n_steps,),
        in_specs=in_specs,
        out_specs=[pl.BlockSpec((nb, d_inner), const), hspec_out],
        out_shape=[jax.ShapeDtypeStruct((nb, d_inner), F32),
                   jax.ShapeDtypeStruct(h_all.shape, F32)],
        scratch_shapes=[pltpu.VMEM((d_inner, nb), F32),
                        pltpu.VMEM((LANES, nb), F32),
                        pltpu.VMEM((nb, gs), F32),
                        pltpu.VMEM((nb, gs), F32),
                        pltpu.VMEM((nb, d_inner), F32),
                        pltpu.VMEM((nb, d_inner), F32)],
        input_output_aliases={len(args) - 1: 1} if has_prev else {},
        compiler_params=_cparams(("arbitrary",)),
        name="ssd_step",
    )(*args)


def _shift_rows(x, s, fill, row):
    if s % SUBLANES == 0:
        return jnp.concatenate([jnp.full((s, x.shape[1]), fill, x.dtype), x[:x.shape[0] - s, :]], axis=0)
    return jnp.where(row >= s, pltpu.roll(x, s, 0), fill)


def _lru_chunk_kernel(xr_ref, h0_ref, wax_ref, ba_ref, bx_ref, lam_ref, y_ref, hout_ref, hc_s, *, T, n_valid):
    c = pl.program_id(1)

    @pl.when(c == 0)
    def _():
        hc_s[...] = h0_ref[0]

    a, bt = _lru_gates(xr_ref[...], wax_ref, ba_ref, bx_ref, lam_ref)
    row = lax.broadcasted_iota(jnp.int32, a.shape, 0)
    if n_valid < T:
        a = jnp.where(row < n_valid, a, 1.0)
        bt = jnp.where(row < n_valid, bt, 0.0)
    s = 1
    while s < T:
        bt = a * _shift_rows(bt, s, 0.0, row) + bt
        a = a * _shift_rows(a, s, 1.0, row)
        s *= 2
    h = a * hc_s[...] + bt
    hc_s[...] = h[T - 1:T, :]
    y_ref[...] = h

    @pl.when(c == pl.num_programs(1) - 1)
    def _():
        hout_ref[0] = h[T - 1:T, :]


def _lru_chunk(xr, h0, p, *, nseq, nchunk, row_block0, n_valid):
    T = CHUNK
    d = xr.shape[1]
    bw = d // LRU_BLOCKS
    rows = lambda b, c: (row_block0 + b * nchunk + c, 0)
    shared_h = (lambda b, c: (0, 0, 0)) if h0.shape[0] == 1 else (lambda b, c: (b, 0, 0))
    const = lambda b, c: (0, 0)
    return pl.pallas_call(
        functools.partial(_lru_chunk_kernel, T=T, n_valid=n_valid),
        grid=(nseq, nchunk),
        in_specs=[pl.BlockSpec((T, d), rows),
                  pl.BlockSpec((1, 1, d), shared_h),
                  pl.BlockSpec((LRU_BLOCKS, bw, 2 * bw), lambda b, c: (0, 0, 0)),
                  pl.BlockSpec((1, d), const),
                  pl.BlockSpec((1, d), const),
                  pl.BlockSpec((1, d), const)],
        out_specs=[pl.BlockSpec((T, d), lambda b, c: (b * nchunk + c, 0)),
                   pl.BlockSpec((1, 1, d), lambda b, c: (b, 0, 0))],
        out_shape=[jax.ShapeDtypeStruct((nseq * nchunk * T, d), F32),
                   jax.ShapeDtypeStruct((nseq, 1, d), F32)],
        scratch_shapes=[pltpu.VMEM((1, d), F32)],
        compiler_params=_cparams(("parallel", "arbitrary")),
        name="lru_chunk",
    )(xr, h0, p["wax"], p["b_a"], p["b_x"], p["lam"])


def _lru_seq_kernel(xr_ref, conv0_ref, h0_ref, cw_ref, cb_ref, wax_ref, ba_ref, bx_ref, lam_ref,
                    y_ref, hout_ref, xc_s, prev_s, h_s):
    i = pl.program_id(0)
    nseq, Tt, d = xr_ref.shape

    @pl.when(i == 0)
    def _():
        for k in range(CONV_WIDTH - 1):
            prev_s[k] = jnp.broadcast_to(conv0_ref[0, SUBLANES - CONV_WIDTH + 1 + k:SUBLANES - CONV_WIDTH + 2 + k, :],
                                         (nseq, d))
        h_s[...] = jnp.broadcast_to(h0_ref[0], (nseq, d))

    taps = [prev_s[k] for k in range(CONV_WIDTH - 1)]
    for t in range(Tt):
        x_t = xr_ref[:, t, :]
        c_t = cb_ref[...] + cw_ref[CONV_WIDTH - 1:CONV_WIDTH, :] * x_t
        for k in range(CONV_WIDTH - 1):
            c_t = c_t + cw_ref[k:k + 1, :] * taps[k]
        xc_s[t * nseq:(t + 1) * nseq, :] = c_t
        taps = taps[1:] + [x_t]
    for k in range(CONV_WIDTH - 1):
        prev_s[k] = taps[k]

    a, bt = _lru_gates(xc_s[...], wax_ref, ba_ref, bx_ref, lam_ref)
    h = h_s[...]
    for t in range(Tt):
        h = a[t * nseq:(t + 1) * nseq, :] * h + bt[t * nseq:(t + 1) * nseq, :]
        y_ref[:, t, :] = h
    h_s[...] = h

    @pl.when(i == pl.num_programs(0) - 1)
    def _():
        hout_ref[...] = h


def _lru_seq(gx, conv0, h0, p, *, nseq, tt):
    M, d2 = gx.shape
    d = d2 // 2
    S = M // nseq
    bw = d // LRU_BLOCKS
    tt = min(tt, S)
    assert S % tt == 0 and nseq == SUBLANES and tt % SUBLANES == 0
    const = lambda i: (0, 0)
    y, hout = pl.pallas_call(
        _lru_seq_kernel,
        grid=(S // tt,),
        in_specs=[pl.BlockSpec((nseq, tt, d), lambda i: (0, i, 1)),
                  pl.BlockSpec((1, SUBLANES, d), lambda i: (0, 0, 0)),
                  pl.BlockSpec((1, 1, d), lambda i: (0, 0, 0)),
                  pl.BlockSpec((CONV_WIDTH, d), const),
                  pl.BlockSpec((1, d), const),
                  pl.BlockSpec((LRU_BLOCKS, bw, 2 * bw), lambda i: (0, 0, 0)),
                  pl.BlockSpec((1, d), const),
                  pl.BlockSpec((1, d), const),
                  pl.BlockSpec((1, d), const)],
        out_specs=[pl.BlockSpec((nseq, tt, d), lambda i: (0, i, 0)), pl.BlockSpec((nseq, d), const)],
        out_shape=[jax.ShapeDtypeStruct((nseq, S, d), F32), jax.ShapeDtypeStruct((nseq, d), F32)],
        scratch_shapes=[pltpu.VMEM((tt * nseq, d), F32),
                        pltpu.VMEM((CONV_WIDTH - 1, nseq, d), F32),
                        pltpu.VMEM((nseq, d), F32)],
        compiler_params=_cparams(("arbitrary",)),
        name="lru_seq",
    )(gx.reshape(nseq, S, d2), conv0, h0, p["conv_w"], p["conv_b"], p["wax"], p["b_a"], p["b_x"], p["lam"])
    return y.reshape(M, d), hout


def _lru_step_kernel(xr_ref, c0_ref, c1_ref, c2_ref, h0_ref, cw_ref, cb_ref, wax_ref, ba_ref, bx_ref, lam_ref, hout_ref):
    xr = _conv_step(xr_ref[...], (c0_ref[...], c1_ref[...], c2_ref[...]), cw_ref, cb_ref)
    a, bt = _lru_gates(xr, wax_ref, ba_ref, bx_ref, lam_ref)
    hout_ref[...] = a * h0_ref[...] + bt


def _lru_step(xr, conv_state, h0, p, *, nb):
    d = xr.shape[1]
    bw = d // LRU_BLOCKS
    const = lambda i: (0, 0)
    prev = lambda k: pl.BlockSpec((nb, d), lambda i: (0, k))
    conv_state = conv_state.reshape(nb, -1)
    return pl.pallas_call(
        _lru_step_kernel,
        grid=(1,),
        in_specs=[pl.BlockSpec((nb, d), const),
                  prev(0), prev(1), prev(2),
                  pl.BlockSpec((nb, d), const),
                  pl.BlockSpec((CONV_WIDTH, d), const),
                  pl.BlockSpec((1, d), const),
                  pl.BlockSpec((LRU_BLOCKS, bw, 2 * bw), lambda i: (0, 0, 0)),
                  pl.BlockSpec((1, d), const),
                  pl.BlockSpec((1, d), const),
                  pl.BlockSpec((1, d), const)],
        out_specs=pl.BlockSpec((nb, d), const),
        out_shape=jax.ShapeDtypeStruct((nb, d), F32),
        compiler_params=_cparams(("arbitrary",)),
        name="lru_step",
    )(xr, conv_state, conv_state, conv_state, h0, p["conv_w"], p["conv_b"], p["wax"], p["b_a"], p["b_x"], p["lam"])


def _row(v):
    return v.reshape(1, -1).astype(F32)


def _pad_lanes(v):
    return jnp.pad(v.astype(F32), (0, LANES - v.shape[0])).reshape(1, LANES)


def kernel(x_prompt, x_sample, state_ssd_conv, state_ssd_h, state_lru_conv, state_lru_h, meta_tokens, norm_mix_pre, norm_mix_post, norm_ffn_pre, norm_ffn_post, ssd_w_in, ssd_conv_w, ssd_conv_b, ssd_dt_bias, ssd_a_log, ssd_d, ssd_norm, ssd_w_out, lru_w_in, lru_b_in, lru_conv_w, lru_conv_b, lru_w_a, lru_b_a, lru_w_x, lru_b_x, lru_lambda, lru_w_out, lru_b_out, ffn_w1, ffn_w2):
    B, S, D = x_prompt.shape
    nb = x_sample.shape[0]
    n_meta = meta_tokens.shape[0]
    depth = norm_mix_pre.shape[0]
    T = CHUNK
    n_heads = ssd_dt_bias.shape[1]
    d_inner = n_heads * SSD_HEAD_DIM
    cdim = ssd_conv_w.shape[2]
    d_rnn = lru_conv_w.shape[2]
    gw = d_inner // SSD_GROUPS
    tm_p = min(PROJ_TM, S)
    assert x_sample.shape[1] == 1 and nb == T and S % tm_p == 0 and tm_p % T == 0 and B == SUBLANES
    assert SUBLANES <= n_meta <= T and n_meta % SUBLANES == 0
    nchunk = S // T
    meta_blk = nb // T

    xs = jnp.concatenate([x_sample[:, 0, :], meta_tokens, jnp.zeros((T - n_meta, D), F32)], axis=0)
    xp = x_prompt.reshape(B * S, D)
    xn_s = _norm_bf16(xs, _row(norm_mix_pre[0]), tm=PROJ_TM)
    xn_p = _norm_bf16(xp, _row(norm_mix_pre[0]), tm=PROJ_TM)

    small = dict(tm=T, n_raw=meta_blk, n_valid=n_meta,
                 conv=dict(tm=T, n_raw=meta_blk, tiles_per_seq=1, tail_lo=n_meta - SUBLANES))
    prompt = dict(tm=tm_p, n_raw=0, n_valid=tm_p,
                  conv=dict(tm=tm_p, n_raw=0, tiles_per_seq=S // tm_p, tail_lo=tm_p - SUBLANES))
    last3 = lambda tail: tail.reshape(B, S // tm_p, SUBLANES, -1)[:, -1, SUBLANES - (CONV_WIDTH - 1):, :]

    w1_all, w2_all = ffn_w1.astype(BF16), ffn_w2.astype(BF16)
    ssd_wo_all, lru_wo_all = ssd_w_out.astype(BF16), lru_w_out.astype(BF16)

    p_ssd_conv, p_ssd_h, p_lru_conv, p_lru_h = [], [], [], []
    s_ssd_conv, s_lru_conv, s_lru_h = [], [], []
    s_ssd_h = None
    for i in range(depth):
        j = i // 2
        if i % 2 == 0:
            w_dt = jnp.pad(ssd_w_in[j][:, d_inner + cdim:], ((0, 0), (0, LANES - n_heads)))
            p = dict(conv_w=ssd_conv_w[j], conv_b=_row(ssd_conv_b[j]), dt_bias=_pad_lanes(ssd_dt_bias[j]),
                     a_log=_pad_lanes(ssd_a_log[j]), dexp=_row(jnp.repeat(ssd_d[j], SSD_HEAD_DIM)))
            mix, ng, wo_all, b_out = "ssd", _row(ssd_norm[j]), ssd_wo_all, jnp.zeros((1, D), F32)

            def run_proj(xn, conv0, v):
                xbc, tail = _proj(xn, ssd_w_in, j, d_inner, cdim, "conv_silu", tn=PROJ_TN,
                                  conv=(p["conv_w"], p["conv_b"], conv0), **v["conv"])
                z = _proj(xn, ssd_w_in, j, 0, d_inner, "none", tm=v["tm"], tn=PROJ_TN)
                dtq = _dt_proj(xn, w_dt, p["dt_bias"], p["a_log"], tm=v["tm"], n_raw=v["n_raw"],
                               n_valid=v["n_valid"])
                return xbc, tail, z, dtq

            xbc_s, tail_s, g_s, dtq_s = run_proj(xn_s, jnp.zeros((1, SUBLANES, cdim), F32), small)
            u_samp, s_ssd_h = _ssd_step(xbc_s, dtq_s[0], state_ssd_conv[j],
                                        state_ssd_h.reshape(-1, SSD_GROUPS, gw, SSD_STATE), s_ssd_h, p,
                                        nb=nb, layer=j)
            u_meta, h_meta = _ssd_chunk(xbc_s, dtq_s, jnp.zeros((1, SSD_GROUPS, gw, SSD_STATE), F32), p,
                                        nseq=1, nchunk=1, row_block0=meta_blk)
            conv_meta = tail_s[meta_blk:meta_blk + 1]
            s_ssd_conv.append(jnp.concatenate([state_ssd_conv[j][:, 1:], xbc_s[:nb, None, :]], axis=1))
            u_s = jnp.concatenate([u_samp, u_meta], axis=0)

            xbc_p, tail_p, g_p, dtq_p = run_proj(xn_p, conv_meta, prompt)
            u_p, h_p = _ssd_chunk(xbc_p, dtq_p, h_meta, p, nseq=B, nchunk=nchunk)
            p_ssd_conv.append(last3(tail_p))
            p_ssd_h.append(h_p.reshape(B, n_heads, SSD_HEAD_DIM, SSD_STATE))
        else:
            p = dict(conv_w=lru_conv_w[j], conv_b=_row(lru_conv_b[j]),
                     wax=jnp.concatenate([lru_w_a[j], lru_w_x[j]], axis=-1).astype(BF16),
                     b_a=_row(lru_b_a[j]), b_x=_row(lru_b_x[j]), lam=_row(lru_lambda[j]))
            mix, ng, wo_all, b_out = "lru", None, lru_wo_all, _row(lru_b_out[j])
            b_in = _row(lru_b_in[j])

            g_s = _proj(xn_s, lru_w_in, j, 0, d_rnn, "none", tm=T, tn=PROJ_TN, bias=b_in)
            xr_s, tail_s = _proj(xn_s, lru_w_in, j, d_rnn, d_rnn, "conv", tn=PROJ_TN, bias=b_in,
                                 conv=(p["conv_w"], p["conv_b"], jnp.zeros((1, SUBLANES, d_rnn), F32)),
                                 **small["conv"])
            h_samp = _lru_step(xr_s, state_lru_conv[j], state_lru_h[j], p, nb=nb)
            u_meta, h_meta = _lru_chunk(xr_s, jnp.zeros((1, 1, d_rnn), F32), p,
                                        nseq=1, nchunk=1, row_block0=meta_blk, n_valid=n_meta)
            conv_meta = tail_s[meta_blk:meta_blk + 1]
            s_lru_conv.append(jnp.concatenate([state_lru_conv[j][:, 1:], xr_s[:nb, None, :]], axis=1))
            s_lru_h.append(h_samp)
            u_s = jnp.concatenate([h_samp, u_meta], axis=0)

            g_p = _proj(xn_p, lru_w_in, j, 0, 2 * d_rnn, "none", tm=tm_p, tn=PROJ_TN, bias=b_in)
            u_p, h_p = _lru_seq(g_p, conv_meta, h_meta, p, nseq=B, tt=LRU_TT)
            p_lru_conv.append(g_p.reshape(B, S, 2 * d_rnn)[:, S - (CONV_WIDTH - 1):, d_rnn:])
            p_lru_h.append(h_p)

        g_next = _row(norm_mix_pre[i + 1]) if i + 1 < depth else None
        ffn = (i, wo_all, j, b_out, _row(norm_mix_post[i]), _row(norm_ffn_pre[i]), w1_all, w2_all,
               _row(norm_ffn_post[i]), g_next)
        xs, xn_s = _out_ffn(u_s, g_s, ng, xs, *ffn, mix=mix, tm=FFN_TM)
        xp, xn_p = _out_ffn(u_p, g_p, ng, xp, *ffn, mix=mix, tm=FFN_TM)

    return (xp.reshape(B, S, D), xs[:nb].reshape(nb, 1, D),
            jnp.stack(p_ssd_conv), jnp.stack(p_ssd_h), jnp.stack(p_lru_conv), jnp.stack(p_lru_h),
            jnp.stack(s_ssd_conv), s_ssd_h.reshape(state_ssd_h.shape), jnp.stack(s_lru_conv), jnp.stack(s_lru_h))
```

```python
import functools
import math

import jax
import jax.numpy as jnp
from jax import lax
from jax.experimental import pallas as pl
from jax.experimental.pallas import tpu as pltpu

F32 = jnp.float32
BF16 = jnp.bfloat16

EPS = 1e-6
CONV_WIDTH = 4
SSD_HEAD_DIM = 64
SSD_GROUPS = 8
SSD_STATE = 128
LRU_BLOCKS = 8
LRU_C = 8.0
CHUNK = 128
LANES = 128
SUBLANES = 8
MXU_COLS = 256
VMEM_LIMIT = 56 * 1024 * 1024
PROJ_TM = 1024
PROJ_TN = 1024
FFN_TM = 512
LRU_TT = 64
SSD_SEQ_PER_STEP = 4
FFN_ROW_SPLIT = 2


def _cparams(sem):
    return pltpu.CompilerParams(dimension_semantics=sem, vmem_limit_bytes=VMEM_LIMIT)


def _rms(x, g):
    return x * lax.rsqrt(jnp.mean(x * x, axis=-1, keepdims=True) + EPS) * g


def _softplus(x):
    return jnp.maximum(x, 0.0) + jnp.log1p(jnp.exp(-jnp.abs(x)))


def _silu(x):
    return x * jax.nn.sigmoid(x)


def _gelu_tanh(x):
    return 0.5 * x * (1.0 + jnp.tanh(math.sqrt(2.0 / math.pi) * (x + 0.044715 * (x * x * x))))


def _dot(a, b):
    return jnp.dot(a, b, preferred_element_type=F32)


def _dot_nt(a, b):
    return lax.dot_general(a, b, (((1,), (1,)), ((), ())), preferred_element_type=F32)


def _pair_expand(q, j, lane):
    return jnp.where(lane < SSD_HEAD_DIM, q[:, 2 * j:2 * j + 1], q[:, 2 * j + 1:2 * j + 2])


def _conv_rows(x, prev8, cw, cb):
    row8 = lax.broadcasted_iota(jnp.int32, prev8.shape, 0)
    out = cb + cw[CONV_WIDTH - 1:CONV_WIDTH, :] * x
    for k in range(1, CONV_WIDTH):
        r = pltpu.roll(x, k, 0)
        head = jnp.where(row8 < k, pltpu.roll(prev8, k, 0), r[0:SUBLANES, :])
        shifted = jnp.concatenate([head, r[SUBLANES:, :]], axis=0)
        out = out + cw[CONV_WIDTH - 1 - k:CONV_WIDTH - k, :] * shifted
    return out


def _conv_step(x, prevs, cw_ref, cb_ref):
    out = cb_ref[...] + cw_ref[CONV_WIDTH - 1:CONV_WIDTH, :] * x
    for k in range(CONV_WIDTH - 1):
        out = out + cw_ref[k:k + 1, :] * prevs[k]
    return out


def _ssd_gate_norm(y, zs, ng_ref):
    y = y * zs
    gw = y.shape[1] // SSD_GROUPS
    parts = []
    for g in range(SSD_GROUPS):
        seg = y[:, g * gw:(g + 1) * gw]
        parts.append(seg * lax.rsqrt(jnp.mean(seg * seg, axis=-1, keepdims=True) + EPS))
    return jnp.concatenate(parts, axis=1) * ng_ref[...]


def _lru_gates(xr, wax_ref, ba_ref, bx_ref, lam_ref):
    bw = xr.shape[1] // LRU_BLOCKS
    ra, ix = [], []
    for k in range(LRU_BLOCKS):
        g = _dot(xr[:, k * bw:(k + 1) * bw].astype(BF16), wax_ref[k])
        ra.append(g[:, :bw])
        ix.append(g[:, bw:])
    r = jax.nn.sigmoid(jnp.concatenate(ra, axis=1) + ba_ref[...])
    i = jax.nn.sigmoid(jnp.concatenate(ix, axis=1) + bx_ref[...])
    log_a = (-LRU_C) * r * _softplus(-lam_ref[...])
    a = jnp.exp(log_a)
    v = -jnp.tanh(log_a) * (a * a + 1.0)
    mult = jnp.where(v > 0.0, v * lax.rsqrt(v), 0.0)
    return a, mult * i * xr


def _norm_kernel(x_ref, g_ref, o_ref):
    o_ref[...] = _rms(x_ref[...], g_ref[...]).astype(o_ref.dtype)


def _norm_bf16(x, g, *, tm):
    M, D = x.shape
    tm = min(tm, M)
    assert M % tm == 0
    return pl.pallas_call(
        _norm_kernel,
        grid=(M // tm,),
        in_specs=[pl.BlockSpec((tm, D), lambda i: (i, 0)), pl.BlockSpec((1, D), lambda i: (0, 0))],
        out_specs=pl.BlockSpec((tm, D), lambda i: (i, 0)),
        out_shape=jax.ShapeDtypeStruct((M, D), BF16),
        compiler_params=_cparams(("parallel",)),
        name="norm",
    )(x, g)


def _proj_kernel(*refs, mode, has_bias, n_raw, tiles_per_seq, tail_lo):
    has_conv = mode in ("conv", "conv_silu")
    it = iter(refs)
    xn_ref, w_ref = next(it), next(it)
    b_ref = next(it) if has_bias else None
    cw_ref, cb_ref, conv0_ref = (next(it), next(it), next(it)) if has_conv else (None, None, None)
    o_ref = next(it)
    tail_ref = next(it) if has_conv else None
    wb_s = next(it)
    carry_s = next(it) if has_conv else None
    i = pl.program_id(1)

    @pl.when(i == 0)
    def _():
        wb_s[...] = w_ref[...].astype(BF16)

    tm, tn = o_ref.shape
    for n in range(tn // MXU_COLS):
        sl = pl.ds(n * MXU_COLS, MXU_COLS)
        acc = _dot(xn_ref[...], wb_s[:, sl])
        if has_bias:
            acc = acc + b_ref[:, sl]
        if not has_conv:
            o_ref[:, sl] = acc
            continue
        tail_ref[0, :, sl] = acc[tail_lo:tail_lo + SUBLANES, :]

        def conv_tile(acc=acc, sl=sl):
            start = lax.rem(i - n_raw, tiles_per_seq) == 0
            prev8 = jnp.where(start, conv0_ref[0, :, sl], carry_s[:, sl])
            out = _conv_rows(acc, prev8, cw_ref[:, sl], cb_ref[:, sl])
            carry_s[:, sl] = acc[tm - SUBLANES:tm, :]
            o_ref[:, sl] = _silu(out) if mode == "conv_silu" else out

        if n_raw == 0:
            conv_tile()
        else:
            @pl.when(i < n_raw)
            def _(acc=acc, sl=sl):
                o_ref[:, sl] = acc

            pl.when(i >= n_raw)(conv_tile)


def _proj(xn, w, layer, col0, n_out, mode, *, tm, tn, bias=None, conv=None, n_raw=0, tiles_per_seq=1,
          tail_lo=0):
    M, D = xn.shape
    tm = min(tm, M)
    assert M % tm == 0 and n_out % tn == 0 and col0 % tn == 0 and tn % MXU_COLS == 0
    assert (conv is not None) == (mode in ("conv", "conv_silu"))
    n_i, n_j, jb = M // tm, n_out // tn, col0 // tn
    in_specs = [pl.BlockSpec((tm, D), lambda j, i: (i, 0)),
                pl.BlockSpec((D, tn), lambda j, i: (layer, jb + j))]
    args = [xn, w.reshape(-1, w.shape[-1])]
    if bias is not None:
        in_specs.append(pl.BlockSpec((1, tn), lambda j, i: (0, jb + j)))
        args.append(bias)
    out_shape = [jax.ShapeDtypeStruct((M, n_out), F32)]
    out_specs = [pl.BlockSpec((tm, tn), lambda j, i: (i, j))]
    scratch = [pltpu.VMEM((D, tn), BF16)]
    if conv is not None:
        cw, cb, conv0 = conv
        in_specs += [pl.BlockSpec((CONV_WIDTH, tn), lambda j, i: (0, j)),
                     pl.BlockSpec((1, tn), lambda j, i: (0, j)),
                     pl.BlockSpec((1, SUBLANES, tn), lambda j, i: (0, 0, j))]
        args += [cw, cb, conv0]
        out_shape.append(jax.ShapeDtypeStruct((n_i, SUBLANES, n_out), F32))
        out_specs.append(pl.BlockSpec((1, SUBLANES, tn), lambda j, i: (i, 0, j)))
        scratch.append(pltpu.VMEM((SUBLANES, tn), F32))
    res = pl.pallas_call(
        functools.partial(_proj_kernel, mode=mode, has_bias=bias is not None, n_raw=n_raw,
                          tiles_per_seq=tiles_per_seq, tail_lo=tail_lo),
        grid=(n_j, n_i),
        in_specs=in_specs, out_specs=out_specs, out_shape=out_shape, scratch_shapes=scratch,
        compiler_params=_cparams(("arbitrary", "arbitrary")),
        name="proj_" + mode,
    )(*args)
    return res if conv is not None else res[0]


def _dt_kernel(xn_ref, w_ref, dtb_ref, alog_ref, dtv_ref, cs_ref, csT_ref, dtT_ref, *, n_raw, n_valid):
    i = pl.program_id(0)
    tm = xn_ref.shape[0]
    T = CHUNK
    dtv = _softplus(_dot(xn_ref[...], w_ref[...].astype(BF16)) + dtb_ref[...])
    if n_valid < tm:
        keep = jnp.logical_or(lax.broadcasted_iota(jnp.int32, dtv.shape, 0) < n_valid, i < n_raw)
        dtv = jnp.where(keep, dtv, 0.0)
    dtv_ref[...] = dtv
    a = dtv * (-jnp.exp(alog_ref[...]))
    tri = (lax.broadcasted_iota(jnp.int32, (T, T), 0) >= lax.broadcasted_iota(jnp.int32, (T, T), 1)).astype(F32)
    for r in range(tm // T):
        cs = jnp.dot(tri, a[r * T:(r + 1) * T, :], precision=lax.Precision.HIGHEST, preferred_element_type=F32)
        cs_ref[r * T:(r + 1) * T, :] = cs
        csT_ref[r] = cs.T
        dtT_ref[r] = dtv[r * T:(r + 1) * T, :].T


def _dt_proj(xn, w_dt, dtb, alog, *, tm, n_raw, n_valid):
    M, D = xn.shape
    T = CHUNK
    tm = min(tm, M)
    assert M % tm == 0 and tm % T == 0 and T == LANES and (n_valid >= tm or tm == T)
    const = lambda i: (0, 0)
    rows = pl.BlockSpec((tm, LANES), lambda i: (i, 0))
    rowsT = pl.BlockSpec((tm // T, LANES, T), lambda i: (i, 0, 0))
    return pl.pallas_call(
        functools.partial(_dt_kernel, n_raw=n_raw, n_valid=n_valid),
        grid=(M // tm,),
        in_specs=[pl.BlockSpec((tm, D), lambda i: (i, 0)),
                  pl.BlockSpec((D, LANES), const),
                  pl.BlockSpec((1, LANES), const),
                  pl.BlockSpec((1, LANES), const)],
        out_specs=[rows, rows, rowsT, rowsT],
        out_shape=[jax.ShapeDtypeStruct((M, LANES), F32), jax.ShapeDtypeStruct((M, LANES), F32),
                   jax.ShapeDtypeStruct((M // T, LANES, T), F32), jax.ShapeDtypeStruct((M // T, LANES, T), F32)],
        compiler_params=_cparams(("parallel",)),
        name="dt_proj",
    )(xn, w_dt, dtb, alog)


def _out_ffn_kernel(*refs, fc, mix, has_next):
    it = iter(refs)
    u_ref, g_ref = next(it), next(it)
    ng_ref = next(it) if mix == "ssd" else None
    x_ref, wo_ref, bo_ref, gpost_ref, gpre_ref, w1_ref, w2_ref, gfpost_ref = (next(it) for _ in range(8))
    gnext_ref = next(it) if has_next else None
    o_ref = next(it)
    xn_ref = next(it) if has_next else None
    tm = o_ref.shape[0]
    dff = w1_ref.shape[1]
    ns = FFN_ROW_SPLIT if tm % (FFN_ROW_SPLIT * 2 * SUBLANES) == 0 else 1
    R = [pl.ds(k * (tm // ns), tm // ns) for k in range(ns)]
    if mix == "ssd":
        ys = [_ssd_gate_norm(u_ref[r, :], _silu(g_ref[r, :]), ng_ref).astype(BF16) for r in R]
    else:
        ys = [(u_ref[r, :] * _gelu_tanh(g_ref[r, :])).astype(BF16) for r in R]
    ms = [_dot(y, wo_ref[...]) + bo_ref[...] for y in ys]
    x1s = [x_ref[r, :] + _rms(m, gpost_ref[...]) for r, m in zip(R, ms)]
    hns = [_rms(x1, gpre_ref[...]).astype(BF16) for x1 in x1s]
    accs = [None] * ns
    for c in range(dff // fc):
        hs = [jnp.maximum(_dot(hn, w1_ref[:, c * fc:(c + 1) * fc]), 0.0) for hn in hns]
        parts = [_dot((h * h).astype(BF16), w2_ref[c * fc:(c + 1) * fc, :]) for h in hs]
        accs = [p if a is None else a + p for a, p in zip(accs, parts)]
    for r, x1, acc in zip(R, x1s, accs):
        x2 = x1 + _rms(acc, gfpost_ref[...])
        o_ref[r, :] = x2
        if has_next:
            xn_ref[r, :] = _rms(x2, gnext_ref[...]).astype(xn_ref.dtype)


def _out_ffn(u, g, ng, x, layer, wo, wo_layer, bo, gpost, gpre, w1, w2, gfpost, gnext, *, mix, tm, fc=1024):
    M, D = x.shape
    K = u.shape[1]
    dff = w1.shape[2]
    tm = min(tm, M)
    assert M % tm == 0 and dff % fc == 0
    const = lambda i: (0, 0)
    once = dict(pipeline_mode=pl.Buffered(1))
    rows = lambda w: pl.BlockSpec((tm, w), lambda i: (i, 0))
    has_next = gnext is not None
    in_specs = [rows(K), rows(K)]
    args = [u, g]
    if mix == "ssd":
        in_specs.append(pl.BlockSpec((1, K), const))
        args.append(ng)
    in_specs += [rows(D),
                 pl.BlockSpec((None, K, D), lambda i: (wo_layer, 0, 0), **once),
                 pl.BlockSpec((1, D), const),
                 pl.BlockSpec((1, D), const),
                 pl.BlockSpec((1, D), const),
                 pl.BlockSpec((None, D, dff), lambda i: (layer, 0, 0), **once),
                 pl.BlockSpec((None, dff, D), lambda i: (layer, 0, 0), **once),
                 pl.BlockSpec((1, D), const)]
    args += [x, wo, bo, gpost, gpre, w1, w2, gfpost]
    out_specs = [rows(D)]
    out_shape = [jax.ShapeDtypeStruct((M, D), F32)]
    if has_next:
        in_specs.append(pl.BlockSpec((1, D), const))
        args.append(gnext)
        out_specs.append(rows(D))
        out_shape.append(jax.ShapeDtypeStruct((M, D), BF16))
    res = pl.pallas_call(
        functools.partial(_out_ffn_kernel, fc=fc, mix=mix, has_next=has_next),
        grid=(M // tm,),
        in_specs=in_specs, out_specs=out_specs, out_shape=out_shape,
        compiler_params=_cparams(("parallel",)),
        name="out_ffn_" + mix,
    )(*args)
    return (res[0], res[1]) if has_next else (res[0], None)


def _ssd_chunk_kernel(xbc_ref, cs_ref, csT_ref, dtT_ref, h0_ref, dexp_ref, y_ref, hout_ref, hT_s, *, T):
    nq = y_ref.shape[0]
    c = pl.program_id(1)

    @pl.when(c == 0)
    def _():
        for q in range(nq):
            for g in range(SSD_GROUPS):
                hT_s[q, g] = h0_ref[0, g].T

    for q in range(nq):
        _ssd_chunk_body(xbc_ref.at[q], cs_ref.at[q], csT_ref.at[q], dtT_ref.at[q], dexp_ref, y_ref.at[q],
                        hT_s.at[q], T=T)

    @pl.when(c == pl.num_programs(1) - 1)
    def _():
        for q in range(nq):
            for g in range(SSD_GROUPS):
                hout_ref[q, g] = hT_s[q, g].T


def _ssd_chunk_body(xbc_ref, cs_ref, csT_ref, dtT_ref, dexp_ref, y_ref, hT_s, *, T):
    d_inner = y_ref.shape[1]
    gs = SSD_GROUPS * SSD_STATE
    gw = d_inner // SSD_GROUPS

    row = lax.broadcasted_iota(jnp.int32, (T, T), 0)
    col = lax.broadcasted_iota(jnp.int32, (T, T), 1)
    causal = row >= col
    lane = lax.broadcasted_iota(jnp.int32, (T, LANES), 1)
    lo_half = lane < SSD_HEAD_DIM

    cs = cs_ref[...]
    csT = csT_ref[...]
    dtT = dtT_ref[...]
    wdT = dtT * jnp.exp(csT[:, T - 1:T] - csT)
    dec_tot = jnp.exp(cs[T - 1:T, :])

    ys = []
    for g in range(SSD_GROUPS):
        Bg = xbc_ref[:, d_inner + g * SSD_STATE:d_inner + (g + 1) * SSD_STATE]
        Cg = xbc_ref[:, d_inner + gs + g * SSD_STATE:d_inner + gs + (g + 1) * SSD_STATE].astype(BF16)
        BgT = Bg.T
        cb = _dot(Cg, BgT.astype(BF16))
        hTg = hT_s[g]
        yoff = _dot(Cg, hTg.astype(BF16))
        for jj in range(gw // LANES):
            j = g * (gw // LANES) + jj
            xpair = xbc_ref[:, j * LANES:(j + 1) * LANES]
            s_parts, bw_parts, ecs_parts = [], [], []
            for hh in (2 * j, 2 * j + 1):
                csl = jnp.broadcast_to(cs[:, hh:hh + 1], (T, T))
                L = jnp.where(causal, jnp.exp(csl - csT[hh:hh + 1, :]), 0.0)
                s_parts.append((cb * L * dtT[hh:hh + 1, :]).astype(BF16))
                bw_parts.append((BgT * wdT[hh:hh + 1, :]).astype(BF16))
                ecs_parts.append(jnp.exp(csl))
            x2 = jnp.concatenate([jnp.where(lo_half, xpair, 0.0).astype(BF16),
                                  jnp.where(lo_half, 0.0, xpair).astype(BF16)], axis=0)
            ydiag = _dot(jnp.concatenate(s_parts, axis=1), x2)
            upd = _dot(jnp.concatenate(bw_parts, axis=1), x2)
            ecs = jnp.where(lo_half, ecs_parts[0], ecs_parts[1])
            ys.append(ydiag + yoff[:, jj * LANES:(jj + 1) * LANES] * ecs
                      + xpair * dexp_ref[:, j * LANES:(j + 1) * LANES])
            dect = _pair_expand(dec_tot, j, lane[0:1, :])
            hT_s[g, :, jj * LANES:(jj + 1) * LANES] = hTg[:, jj * LANES:(jj + 1) * LANES] * dect + upd

    y_ref[...] = jnp.concatenate(ys, axis=1)


def _ssd_chunk(xbc, dtq, h0, p, *, nseq, nchunk, row_block0=0):
    T = CHUNK
    assert T == LANES and h0.shape[0] == 1 and (nseq == 1 or row_block0 == 0)
    d_inner = p["dexp"].shape[1]
    gw = d_inner // SSD_GROUPS
    _, cs, csT, dtT = dtq
    nq = SSD_SEQ_PER_STEP if nseq % SSD_SEQ_PER_STEP == 0 else 1
    ng = nseq // nq
    v4 = lambda a: a.reshape(ng, nq, a.shape[0] // nseq, a.shape[1])
    v5 = lambda a: a.reshape(ng, nq, a.shape[0] // nseq, a.shape[1], a.shape[2])
    rows = lambda w: pl.BlockSpec((None, nq, T, w), lambda b, c: (b, 0, row_block0 + c, 0))
    rowsT = pl.BlockSpec((None, nq, None, LANES, T), lambda b, c: (b, 0, row_block0 + c, 0, 0))
    hshape = (SSD_GROUPS, gw, SSD_STATE)
    y, hout = pl.pallas_call(
        functools.partial(_ssd_chunk_kernel, T=T),
        grid=(ng, nchunk),
        in_specs=[rows(xbc.shape[1]), rows(LANES), rowsT, rowsT,
                  pl.BlockSpec((1,) + hshape, lambda b, c: (0, 0, 0, 0)),
                  pl.BlockSpec((1, d_inner), lambda b, c: (0, 0))],
        out_specs=[pl.BlockSpec((None, nq, T, d_inner), lambda b, c: (b, 0, c, 0)),
                   pl.BlockSpec((None, nq) + hshape, lambda b, c: (b, 0, 0, 0, 0))],
        out_shape=[jax.ShapeDtypeStruct((ng, nq, nchunk * T, d_inner), F32),
                   jax.ShapeDtypeStruct((ng, nq) + hshape, F32)],
        scratch_shapes=[pltpu.VMEM((nq, SSD_GROUPS, SSD_STATE, gw), F32)],
        compiler_params=_cparams(("parallel", "arbitrary")),
        name="ssd_chunk",
    )(v4(xbc), v4(cs), v5(csT), v5(dtT), h0, p["dexp"])
    return y.reshape(nseq * nchunk * T, d_inner), hout.reshape((nseq,) + hshape)


def _ssd_step_kernel(*refs, seq_per_step, has_prev):
    xbc_ref, dt_ref, c0_ref, c1_ref, c2_ref, h_ref, cw_ref, cb_ref, alog_ref, dexp_ref = refs[:10]
    y_ref, hout_ref, xdtT_s, decT_s, b_s, c_s, xs_s, yrow_s = refs[10 + has_prev:]
    s = pl.program_id(0)
    nb, d_inner = xs_s.shape
    gs = SSD_GROUPS * SSD_STATE
    gw = d_inner // SSD_GROUPS
    n_own = nb // seq_per_step

    if not has_prev:
        @pl.when(s >= n_own)
        def _():
            hout_ref[...] = jnp.zeros_like(hout_ref)

    @pl.when(s == 0)
    def _():
        xbc = _silu(_conv_step(xbc_ref[...], (c0_ref[...], c1_ref[...], c2_ref[...]), cw_ref, cb_ref))
        xs = xbc[:, :d_inner]
        xs_s[...] = xs
        b_s[...] = xbc[:, d_inner:d_inner + gs]
        c_s[...] = xbc[:, d_inner + gs:]
        dtv = dt_ref[...]
        decT_s[...] = jnp.exp(dtv * (-jnp.exp(alog_ref[...]))).T
        dtT = dtv.T
        for j in range(d_inner // LANES):
            xT = xs[:, j * LANES:(j + 1) * LANES].T
            for k in range(2):
                hh = 2 * j + k
                xdtT_s[hh * SSD_HEAD_DIM:(hh + 1) * SSD_HEAD_DIM, :] = (
                    xT[k * SSD_HEAD_DIM:(k + 1) * SSD_HEAD_DIM, :] * dtT[hh:hh + 1, :])

    pl.when(s < n_own)(functools.partial(_ssd_step_update, s, seq_per_step, h_ref, hout_ref, xdtT_s, decT_s, b_s,
                                        c_s, yrow_s))

    @pl.when(s == n_own - 1)
    def _():
        y_ref[...] = yrow_s[...] + xs_s[...] * dexp_ref[...]


def _ssd_step_update(s, seq_per_step, h_ref, hout_ref, xdtT_s, decT_s, b_s, c_s, yrow_s):
    nb, d_inner = yrow_s.shape
    gw = d_inner // SSD_GROUPS
    for q in range(seq_per_step):
        b = s * seq_per_step + q
        sel = lax.broadcasted_iota(jnp.int32, (1, nb), 1) == b
        dec_col = jnp.sum(jnp.where(sel, decT_s[...], 0.0), axis=1, keepdims=True)
        brow = b_s[pl.ds(b, 1), :]
        crow = c_s[pl.ds(b, 1), :]
        for g in range(SSD_GROUPS):
            xcol = jnp.sum(jnp.where(sel, xdtT_s[g * gw:(g + 1) * gw, :], 0.0), axis=1, keepdims=True)
            dcol = jnp.concatenate(
                [jnp.broadcast_to(dec_col[hh:hh + 1, :], (SSD_HEAD_DIM, 1))
                 for hh in range(g * (gw // SSD_HEAD_DIM), (g + 1) * (gw // SSD_HEAD_DIM))], axis=0)
            hn = h_ref[q, g] * dcol + xcol * brow[:, g * SSD_STATE:(g + 1) * SSD_STATE]
            hout_ref[q, g] = hn
            c8 = jnp.broadcast_to(crow[:, g * SSD_STATE:(g + 1) * SSD_STATE], (SUBLANES, SSD_STATE)).astype(BF16)
            yg = _dot_nt(c8, hn.astype(BF16))
            yrow_s[pl.ds(b, 1), g * gw:(g + 1) * gw] = yg[0:1, :]


def _ssd_step(xbc, dt, conv_state, h_all, h_out_prev, p, *, nb, layer, seq_per_step=8):
    d_inner = p["dexp"].shape[1]
    cdim = p["conv_w"].shape[1]
    gw = d_inner // SSD_GROUPS
    gs = SSD_GROUPS * SSD_STATE
    assert nb % seq_per_step == 0
    n_own = nb // seq_per_step
    n_all = h_all.shape[0] // seq_per_step
    blk0 = layer * n_own
    const = lambda s: (0, 0)
    prev = lambda k: pl.BlockSpec((nb, cdim), lambda s: (0, k))
    conv_state = conv_state.reshape(nb, -1)
    has_prev = h_out_prev is not None
    n_steps = n_own if has_prev else n_all
    hblock = (seq_per_step, SSD_GROUPS, gw, SSD_STATE)
    hspec_in = pl.BlockSpec(hblock, lambda s: (blk0 + jnp.minimum(s, n_own - 1), 0, 0, 0))
    hspec_out = pl.BlockSpec(hblock, lambda s: (lax.rem(blk0 + s, n_all), 0, 0, 0))
    in_specs = [pl.BlockSpec((nb, cdim), const),
                pl.BlockSpec((nb, LANES), const),
                prev(0), prev(1), prev(2),
                hspec_in,
                pl.BlockSpec((CONV_WIDTH, cdim), const),
                pl.BlockSpec((1, cdim), const),
                pl.BlockSpec((1, LANES), const),
                pl.BlockSpec((1, d_inner), const)]
    args = [xbc, dt, conv_state, conv_state, conv_state, h_all, p["conv_w"], p["conv_b"], p["a_log"], p["dexp"]]
    if has_prev:
        in_specs.append(pl.BlockSpec(memory_space=pl.ANY))
        args.append(h_out_prev)
    return pl.pallas_call(
        functools.partial(_ssd_step_kernel, seq_per_step=seq_per_step, has_prev=has_prev),
        grid=(n_steps,),
        in_specs=in_specs,
        out_specs=[pl.BlockSpec((nb, d_inner), const), hspec_out],
        out_shape=[jax.ShapeDtypeStruct((nb, d_inner), F32),
                   jax.ShapeDtypeStruct(h_all.shape, F32)],
        scratch_shapes=[pltpu.VMEM((d_inner, nb), F32),
                        pltpu.VMEM((LANES, nb), F32),
                        pltpu.VMEM((nb, gs), F32),
                        pltpu.VMEM((nb, gs), F32),
                        pltpu.VMEM((nb, d_inner), F32),
                        pltpu.VMEM((nb, d_inner), F32)],
        input_output_aliases={len(args) - 1: 1} if has_prev else {},
        compiler_params=_cparams(("arbitrary",)),
        name="ssd_step",
    )(*args)


def _shift_rows(x, s, fill, row):
    if s % SUBLANES == 0:
        return jnp.concatenate([jnp.full((s, x.shape[1]), fill, x.dtype), x[:x.shape[0] - s, :]], axis=0)
    return jnp.where(row >= s, pltpu.roll(x, s, 0), fill)


def _lru_chunk_kernel(xr_ref, h0_ref, wax_ref, ba_ref, bx_ref, lam_ref, y_ref, hout_ref, hc_s, *, T, n_valid):
    c = pl.program_id(1)

    @pl.when(c == 0)
    def _():
        hc_s[...] = h0_ref[0]

    a, bt = _lru_gates(xr_ref[...], wax_ref, ba_ref, bx_ref, lam_ref)
    row = lax.broadcasted_iota(jnp.int32, a.shape, 0)
    if n_valid < T:
        a = jnp.where(row < n_valid, a, 1.0)
        bt = jnp.where(row < n_valid, bt, 0.0)
    s = 1
    while s < T:
        bt = a * _shift_rows(bt, s, 0.0, row) + bt
        a = a * _shift_rows(a, s, 1.0, row)
        s *= 2
    h = a * hc_s[...] + bt
    hc_s[...] = h[T - 1:T, :]
    y_ref[...] = h

    @pl.when(c == pl.num_programs(1) - 1)
    def _():
        hout_ref[0] = h[T - 1:T, :]


def _lru_chunk(xr, h0, p, *, nseq, nchunk, row_block0, n_valid):
    T = CHUNK
    d = xr.shape[1]
    bw = d // LRU_BLOCKS
    rows = lambda b, c: (row_block0 + b * nchunk + c, 0)
    shared_h = (lambda b, c: (0, 0, 0)) if h0.shape[0] == 1 else (lambda b, c: (b, 0, 0))
    const = lambda b, c: (0, 0)
    return pl.pallas_call(
        functools.partial(_lru_chunk_kernel, T=T, n_valid=n_valid),
        grid=(nseq, nchunk),
        in_specs=[pl.BlockSpec((T, d), rows),
                  pl.BlockSpec((1, 1, d), shared_h),
                  pl.BlockSpec((LRU_BLOCKS, bw, 2 * bw), lambda b, c: (0, 0, 0)),
                  pl.BlockSpec((1, d), const),
                  pl.BlockSpec((1, d), const),
                  pl.BlockSpec((1, d), const)],
        out_specs=[pl.BlockSpec((T, d), lambda b, c: (b * nchunk + c, 0)),
                   pl.BlockSpec((1, 1, d), lambda b, c: (b, 0, 0))],
        out_shape=[jax.ShapeDtypeStruct((nseq * nchunk * T, d), F32),
                   jax.ShapeDtypeStruct((nseq, 1, d), F32)],
        scratch_shapes=[pltpu.VMEM((1, d), F32)],
        compiler_params=_cparams(("parallel", "arbitrary")),
        name="lru_chunk",
    )(xr, h0, p["wax"], p["b_a"], p["b_x"], p["lam"])


def _lru_seq_kernel(xr_ref, conv0_ref, h0_ref, cw_ref, cb_ref, wax_ref, ba_ref, bx_ref, lam_ref,
                    y_ref, hout_ref, xc_s, prev_s, h_s):
    i = pl.program_id(0)
    nseq, Tt, d = xr_ref.shape

    @pl.when(i == 0)
    def _():
        for k in range(CONV_WIDTH - 1):
            prev_s[k] = jnp.broadcast_to(conv0_ref[0, SUBLANES - CONV_WIDTH + 1 + k:SUBLANES - CONV_WIDTH + 2 + k, :],
                                         (nseq, d))
        h_s[...] = jnp.broadcast_to(h0_ref[0], (nseq, d))

    taps = [prev_s[k] for k in range(CONV_WIDTH - 1)]
    for t in range(Tt):
        x_t = xr_ref[:, t, :]
        c_t = cb_ref[...] + cw_ref[CONV_WIDTH - 1:CONV_WIDTH, :] * x_t
        for k in range(CONV_WIDTH - 1):
            c_t = c_t + cw_ref[k:k + 1, :] * taps[k]
        xc_s[t * nseq:(t + 1) * nseq, :] = c_t
        taps = taps[1:] + [x_t]
    for k in range(CONV_WIDTH - 1):
        prev_s[k] = taps[k]

    a, bt = _lru_gates(xc_s[...], wax_ref, ba_ref, bx_ref, lam_ref)
    h = h_s[...]
    for t in range(Tt):
        h = a[t * nseq:(t + 1) * nseq, :] * h + bt[t * nseq:(t + 1) * nseq, :]
        y_ref[:, t, :] = h
    h_s[...] = h

    @pl.when(i == pl.num_programs(0) - 1)
    def _():
        hout_ref[...] = h


def _lru_seq(gx, conv0, h0, p, *, nseq, tt):
    M, d2 = gx.shape
    d = d2 // 2
    S = M // nseq
    bw = d // LRU_BLOCKS
    tt = min(tt, S)
    assert S % tt == 0 and nseq == SUBLANES and tt % SUBLANES == 0
    const = lambda i: (0, 0)
    y, hout = pl.pallas_call(
        _lru_seq_kernel,
        grid=(S // tt,),
        in_specs=[pl.BlockSpec((nseq, tt, d), lambda i: (0, i, 1)),
                  pl.BlockSpec((1, SUBLANES, d), lambda i: (0, 0, 0)),
                  pl.BlockSpec((1, 1, d), lambda i: (0, 0, 0)),
                  pl.BlockSpec((CONV_WIDTH, d), const),
                  pl.BlockSpec((1, d), const),
                  pl.BlockSpec((LRU_BLOCKS, bw, 2 * bw), lambda i: (0, 0, 0)),
                  pl.BlockSpec((1, d), const),
                  pl.BlockSpec((1, d), const),
                  pl.BlockSpec((1, d), const)],
        out_specs=[pl.BlockSpec((nseq, tt, d), lambda i: (0, i, 0)), pl.BlockSpec((nseq, d), const)],
        out_shape=[jax.ShapeDtypeStruct((nseq, S, d), F32), jax.ShapeDtypeStruct((nseq, d), F32)],
        scratch_shapes=[pltpu.VMEM((tt * nseq, d), F32),
                        pltpu.VMEM((CONV_WIDTH - 1, nseq, d), F32),
                        pltpu.VMEM((nseq, d), F32)],
        compiler_params=_cparams(("arbitrary",)),
        name="lru_seq",
    )(gx.reshape(nseq, S, d2), conv0, h0, p["conv_w"], p["conv_b"], p["wax"], p["b_a"], p["b_x"], p["lam"])
    return y.reshape(M, d), hout


def _lru_step_kernel(xr_ref, c0_ref, c1_ref, c2_ref, h0_ref, cw_ref, cb_ref, wax_ref, ba_ref, bx_ref, lam_ref, hout_ref):
    xr = _conv_step(xr_ref[...], (c0_ref[...], c1_ref[...], c2_ref[...]), cw_ref, cb_ref)
    a, bt = _lru_gates(xr, wax_ref, ba_ref, bx_ref, lam_ref)
    hout_ref[...] = a * h0_ref[...] + bt


def _lru_step(xr, conv_state, h0, p, *, nb):
    d = xr.shape[1]
    bw = d // LRU_BLOCKS
    const = lambda i: (0, 0)
    prev = lambda k: pl.BlockSpec((nb, d), lambda i: (0, k))
    conv_state = conv_state.reshape(nb, -1)
    return pl.pallas_call(
        _lru_step_kernel,
        grid=(1,),
        in_specs=[pl.BlockSpec((nb, d), const),
                  prev(0), prev(1), prev(2),
                  pl.BlockSpec((nb, d), const),
                  pl.BlockSpec((CONV_WIDTH, d), const),
                  pl.BlockSpec((1, d), const),
                  pl.BlockSpec((LRU_BLOCKS, bw, 2 * bw), lambda i: (0, 0, 0)),
                  pl.BlockSpec((1, d), const),
                  pl.BlockSpec((1, d), const),
                  pl.BlockSpec((1, d), const)],
        out_specs=pl.BlockSpec((nb, d), const),
        out_shape=jax.ShapeDtypeStruct((nb, d), F32),
        compiler_params=_cparams(("arbitrary",)),
        name="lru_step",
    )(xr, conv_state, conv_state, conv_state, h0, p["conv_w"], p["conv_b"], p["wax"], p["b_a"], p["b_x"], p["lam"])


def _row(v):
    return v.reshape(1, -1).astype(F32)


def _pad_lanes(v):
    return jnp.pad(v.astype(F32), (0, LANES - v.shape[0])).reshape(1, LANES)


def kernel(x_prompt, x_sample, state_ssd_conv, state_ssd_h, state_lru_conv, state_lru_h, meta_tokens, norm_mix_pre, norm_mix_post, norm_ffn_pre, norm_ffn_post, ssd_w_in, ssd_conv_w, ssd_conv_b, ssd_dt_bias, ssd_a_log, ssd_d, ssd_norm, ssd_w_out, lru_w_in, lru_b_in, lru_conv_w, lru_conv_b, lru_w_a, lru_b_a, lru_w_x, lru_b_x, lru_lambda, lru_w_out, lru_b_out, ffn_w1, ffn_w2):
    B, S, D = x_prompt.shape
    nb = x_sample.shape[0]
    n_meta = meta_tokens.shape[0]
    depth = norm_mix_pre.shape[0]
    T = CHUNK
    n_heads = ssd_dt_bias.shape[1]
    d_inner = n_heads * SSD_HEAD_DIM
    cdim = ssd_conv_w.shape[2]
    d_rnn = lru_conv_w.shape[2]
    gw = d_inner // SSD_GROUPS
    tm_p = min(PROJ_TM, S)
    assert x_sample.shape[1] == 1 and nb == T and S % tm_p == 0 and tm_p % T == 0 and B == SUBLANES
    assert SUBLANES <= n_meta <= T and n_meta % SUBLANES == 0
    nchunk = S // T
    meta_blk = nb // T

    xs = jnp.concatenate([x_sample[:, 0, :], meta_tokens, jnp.zeros((T - n_meta, D), F32)], axis=0)
    xp = x_prompt.reshape(B * S, D)
    xn_s = _norm_bf16(xs, _row(norm_mix_pre[0]), tm=PROJ_TM)
    xn_p = _norm_bf16(xp, _row(norm_mix_pre[0]), tm=PROJ_TM)

    small = dict(tm=T, n_raw=meta_blk, n_valid=n_meta,
                 conv=dict(tm=T, n_raw=meta_blk, tiles_per_seq=1, tail_lo=n_meta - SUBLANES))
    prompt = dict(tm=tm_p, n_raw=0, n_valid=tm_p,
                  conv=dict(tm=tm_p, n_raw=0, tiles_per_seq=S // tm_p, tail_lo=tm_p - SUBLANES))
    last3 = lambda tail: tail.reshape(B, S // tm_p, SUBLANES, -1)[:, -1, SUBLANES - (CONV_WIDTH - 1):, :]

    w1_all, w2_all = ffn_w1.astype(BF16), ffn_w2.astype(BF16)
    ssd_wo_all, lru_wo_all = ssd_w_out.astype(BF16), lru_w_out.astype(BF16)

    p_ssd_conv, p_ssd_h, p_lru_conv, p_lru_h = [], [], [], []
    s_ssd_conv, s_lru_conv, s_lru_h = [], [], []
    s_ssd_h = None
    for i in range(depth):
        j = i // 2
        if i % 2 == 0:
            w_dt = jnp.pad(ssd_w_in[j][:, d_inner + cdim:], ((0, 0), (0, LANES - n_heads)))
            p = dict(conv_w=ssd_conv_w[j], conv_b=_row(ssd_conv_b[j]), dt_bias=_pad_lanes(ssd_dt_bias[j]),
                     a_log=_pad_lanes(ssd_a_log[j]), dexp=_row(jnp.repeat(ssd_d[j], SSD_HEAD_DIM)))
            mix, ng, wo_all, b_out = "ssd", _row(ssd_norm[j]), ssd_wo_all, jnp.zeros((1, D), F32)

            def run_proj(xn, conv0, v):
                xbc, tail = _proj(xn, ssd_w_in, j, d_inner, cdim, "conv_silu", tn=PROJ_TN,
                                  conv=(p["conv_w"], p["conv_b"], conv0), **v["conv"])
                z = _proj(xn, ssd_w_in, j, 0, d_inner, "none", tm=v["tm"], tn=PROJ_TN)
                dtq = _dt_proj(xn, w_dt, p["dt_bias"], p["a_log"], tm=v["tm"], n_raw=v["n_raw"],
                               n_valid=v["n_valid"])
                return xbc, tail, z, dtq

            xbc_s, tail_s, g_s, dtq_s = run_proj(xn_s, jnp.zeros((1, SUBLANES, cdim), F32), small)
            u_samp, s_ssd_h = _ssd_step(xbc_s, dtq_s[0], state_ssd_conv[j],
                                        state_ssd_h.reshape(-1, SSD_GROUPS, gw, SSD_STATE), s_ssd_h, p,
                                        nb=nb, layer=j)
            u_meta, h_meta = _ssd_chunk(xbc_s, dtq_s, jnp.zeros((1, SSD_GROUPS, gw, SSD_STATE), F32), p,
                                        nseq=1, nchunk=1, row_block0=meta_blk)
            conv_meta = tail_s[meta_blk:meta_blk + 1]
            s_ssd_conv.append(jnp.concatenate([state_ssd_conv[j][:, 1:], xbc_s[:nb, None, :]], axis=1))
            u_s = jnp.concatenate([u_samp, u_meta], axis=0)

            xbc_p, tail_p, g_p, dtq_p = run_proj(xn_p, conv_meta, prompt)
            u_p, h_p = _ssd_chunk(xbc_p, dtq_p, h_meta, p, nseq=B, nchunk=nchunk)
            p_ssd_conv.append(last3(tail_p))
            p_ssd_h.append(h_p.reshape(B, n_heads, SSD_HEAD_DIM, SSD_STATE))
        else:
            p = dict(conv_w=lru_conv_w[j], conv_b=_row(lru_conv_b[j]),
                     wax=jnp.concatenate([lru_w_a[j], lru_w_x[j]], axis=-1).astype(BF16),
                     b_a=_row(lru_b_a[j]), b_x=_row(lru_b_x[j]), lam=_row(lru_lambda[j]))
            mix, ng, wo_all, b_out = "lru", None, lru_wo_all, _row(lru_b_out[j])
            b_in = _row(lru_b_in[j])

            g_s = _proj(xn_s, lru_w_in, j, 0, d_rnn, "none", tm=T, tn=PROJ_TN, bias=b_in)
            xr_s, tail_s = _proj(xn_s, lru_w_in, j, d_rnn, d_rnn, "conv", tn=PROJ_TN, bias=b_in,
                                 conv=(p["conv_w"], p["conv_b"], jnp.zeros((1, SUBLANES, d_rnn), F32)),
                                 **small["conv"])
            h_samp = _lru_step(xr_s, state_lru_conv[j], state_lru_h[j], p, nb=nb)
            u_meta, h_meta = _lru_chunk(xr_s, jnp.zeros((1, 1, d_rnn), F32), p,
                                        nseq=1, nchunk=1, row_block0=meta_blk, n_valid=n_meta)
            conv_meta = tail_s[meta_blk:meta_blk + 1]
            s_lru_conv.append(jnp.concatenate([state_lru_conv[j][:, 1:], xr_s[:nb, None, :]], axis=1))
            s_lru_h.append(h_samp)
            u_s = jnp.concatenate([h_samp, u_meta], axis=0)

            g_p = _proj(xn_p, lru_w_in, j, 0, 2 * d_rnn, "none", tm=tm_p, tn=PROJ_TN, bias=b_in)
            u_p, h_p = _lru_seq(g_p, conv_meta, h_meta, p, nseq=B, tt=LRU_TT)
            p_lru_conv.append(g_p.reshape(B, S, 2 * d_rnn)[:, S - (CONV_WIDTH - 1):, d_rnn:])
            p_lru_h.append(h_p)

        g_next = _row(norm_mix_pre[i + 1]) if i + 1 < depth else None
        ffn = (i, wo_all, j, b_out, _row(norm_mix_post[i]), _row(norm_ffn_pre[i]), w1_all, w2_all,
               _row(norm_ffn_post[i]), g_next)
        xs, xn_s = _out_ffn(u_s, g_s, ng, xs, *ffn, mix=mix, tm=FFN_TM)
        xp, xn_p = _out_ffn(u_p, g_p, ng, xp, *ffn, mix=mix, tm=FFN_TM)

    return (xp.reshape(B, S, D), xs[:nb].reshape(nb, 1, D),
            jnp.stack(p_ssd_conv), jnp.stack(p_ssd_h), jnp.stack(p_lru_conv), jnp.stack(p_lru_h),
            jnp.stack(s_ssd_conv), s_ssd_h.reshape(state_ssd_h.shape), jnp.stack(s_lru_conv), jnp.stack(s_lru_h))
```
